```python
import math
import jax, jax.numpy as jnp
from jax import lax
import numpy as np

D_MODEL = 2048
BATCH = 4
SEQ = 2048
DEPTH = 4
DEC_BATCH = 8
DEC_SEQ = 1
PAST_LEN = 16384
PAGE_SIZE = 128

SSM_W = 1024
SSM_GROUP = 16
SSM_GROUPS = SSM_W // SSM_GROUP
SSM_P = 64
DT_MIN = 0.001
DT_MAX = 0.1
N_HEADS = 8
N_KV_HEADS = 4
HEAD_DIM = 128
GQA_GROUP = N_HEADS // N_KV_HEADS
ATT_W = N_HEADS * HEAD_DIM
KV_W = N_KV_HEADS * HEAD_DIM
N_IDX_HEADS = 16
IDX_DIM = 64
IDX_SCALE = (IDX_DIM * N_IDX_HEADS) ** -0.5
TOPK_MAX = 256
Q_BLOCK = 128
ROPE_THETA = 10000.0
N_BRANCH = 2
BR_W = SSM_W
IN_SIZES = (SSM_W, ATT_W, KV_W, KV_W, N_IDX_HEADS * IDX_DIM, IDX_DIM, N_IDX_HEADS)
IN_COLS = SSM_W + ATT_W + 2 * KV_W + N_IDX_HEADS * IDX_DIM + IDX_DIM + N_IDX_HEADS + N_BRANCH * D_MODEL
D_FF = -(-8 * D_MODEL // (3 * 256)) * 256
EPS = 1e-6

kernel_name = "hybrid_s5_dsa_gated_decoder_step"


def rmsnorm(x, g):
    xf = x.astype(jnp.float32)
    y = xf * lax.rsqrt(jnp.mean(xf * xf, axis=-1, keepdims=True) + EPS)
    return (y * g.astype(jnp.float32)).astype(x.dtype)


def rope(x, pos):
    half = x.shape[-1] // 2
    freqs = ROPE_THETA ** (-jnp.arange(half, dtype=jnp.float32) / half)
    ang = pos.astype(jnp.float32)[:, None] * freqs[None, :]
    cos = jnp.cos(ang)[None, :, None, :]
    sin = jnp.sin(ang)[None, :, None, :]
    xf = x.astype(jnp.float32)
    x1, x2 = xf[..., :half], xf[..., half:]
    return jnp.concatenate([x1 * cos - x2 * sin, x2 * cos + x1 * sin], axis=-1).astype(x.dtype)


def s5_discretize(a_re, a_im, log_dt, b_re, b_im):
    dt = jnp.exp(log_dt.astype(jnp.float32))[:, None]
    lr, li = a_re.astype(jnp.float32), a_im.astype(jnp.float32)
    mag = jnp.exp(lr * dt)
    ab_re, ab_im = mag * jnp.cos(li * dt), mag * jnp.sin(li * dt)
    zr, zi = ab_re - 1.0, ab_im
    den = lr * lr + li * li
    fr = (zr * lr + zi * li) / den
    fi = (zi * lr - zr * li) / den
    br, bi = b_re.astype(jnp.float32), b_im.astype(jnp.float32)
    bb_re = fr[..., None] * br - fi[..., None] * bi
    bb_im = fr[..., None] * bi + fi[..., None] * br
    return ab_re, ab_im, bb_re, bb_im


def s5_combine(e1, e2):
    a1r, a1i, b1r, b1i = e1
    a2r, a2i, b2r, b2i = e2
    return (a2r * a1r - a2i * a1i, a2r * a1i + a2i * a1r,
            a2r * b1r - a2i * b1i + b2r, a2r * b1i + a2i * b1r + b2i)


def s5_scan(u, h0_re, h0_im, a_re, a_im, log_dt, b_re, b_im, c_re, c_im, d):
    nb, s = u.shape[0], u.shape[1]
    uf = u.astype(jnp.float32).reshape(nb, s, SSM_GROUPS, SSM_GROUP)
    ab_re, ab_im, bb_re, bb_im = s5_discretize(a_re, a_im, log_dt, b_re, b_im)
    bu_re = jnp.einsum('bsgw,gpw->bsgp', uf, bb_re)
    bu_im = jnp.einsum('bsgw,gpw->bsgp', uf, bb_im)
    h0r, h0i = h0_re.astype(jnp.float32), h0_im.astype(jnp.float32)
    bu_re = bu_re.at[:, 0].add(ab_re * h0r - ab_im * h0i)
    bu_im = bu_im.at[:, 0].add(ab_re * h0i + ab_im * h0r)
    aa_re = jnp.broadcast_to(ab_re, bu_re.shape)
    aa_im = jnp.broadcast_to(ab_im, bu_im.shape)
    _, _, h_re, h_im = lax.associative_scan(s5_combine, (aa_re, aa_im, bu_re, bu_im), axis=1)
    y = (jnp.einsum('bsgp,gwp->bsgw', h_re, c_re.astype(jnp.float32))
         - jnp.einsum('bsgp,gwp->bsgw', h_im, c_im.astype(jnp.float32)))
    y = y.reshape(nb, s, SSM_W) + d.astype(jnp.float32) * u.astype(jnp.float32)
    return y.astype(u.dtype), h_re[:, -1], h_im[:, -1]


def sparse_attend(q, q_idx, w_idx, q_pos, k_idx_all, gather_kv):
    nb, t = q.shape[0], q.shape[1]
    n_keys = k_idx_all.shape[1]
    topk = min(TOPK_MAX, n_keys // 4)
    dots = jnp.einsum('bthd,bsd->bths', q_idx.astype(jnp.float32), k_idx_all.astype(jnp.float32))
    score = jnp.einsum('bth,bths->bts', w_idx.astype(jnp.float32), jax.nn.relu(dots)) * IDX_SCALE
    admissible = jnp.arange(n_keys)[None, :] <= q_pos[:, None]
    score = jnp.where(admissible[None], score, -jnp.inf)
    top_val, top_idx = lax.top_k(score, topk)
    valid = jnp.isfinite(top_val)
    k_sel, v_sel = gather_kv(top_idx)
    qg = q.reshape(nb, t, N_KV_HEADS, GQA_GROUP, HEAD_DIM).astype(jnp.float32)
    logits = jnp.einsum('btkgd,btnkd->btkgn', qg, k_sel.astype(jnp.float32)) * (HEAD_DIM ** -0.5)
    logits = jnp.where(valid[:, :, None, None, :], logits, -jnp.inf)
    p = jax.nn.softmax(logits, axis=-1)
    o = jnp.einsum('btkgn,btnkd->btkgd', p, v_sel.astype(jnp.float32))
    return o.reshape(nb, t, ATT_W).astype(q.dtype)


def attend_prompt(q, k, v, q_idx, k_idx, w_idx, pos):
    nb, s = q.shape[0], q.shape[1]
    n_blocks = s // Q_BLOCK
    bidx = jnp.arange(nb)[:, None, None]

    def gather(idx):
        return k[bidx, idx], v[bidx, idx]

    def to_blocks(a):
        return jnp.moveaxis(a.reshape((nb, n_blocks, Q_BLOCK) + a.shape[2:]), 1, 0)

    def one_block(args):
        qb, qib, wb, pb = args
        return sparse_attend(qb, qib, wb, pb, k_idx, gather)

    out = lax.map(one_block, (to_blocks(q), to_blocks(q_idx), to_blocks(w_idx), pos.reshape(n_blocks, Q_BLOCK)))
    return jnp.moveaxis(out, 0, 1).reshape(nb, s, ATT_W)


def make_sample_attend(cache_k_l, cache_v_l, cache_kidx_l, page_table):
    def attend(q, k_new, v_new, q_idx, k_idx_new, w_idx, pos):
        nb, t = q.shape[0], q.shape[1]
        past = page_table.shape[1] * PAGE_SIZE
        kidx_past = cache_kidx_l[page_table].reshape(nb, past, IDX_DIM)
        k_idx_all = jnp.concatenate([kidx_past.astype(k_idx_new.dtype), k_idx_new], axis=1)
        bidx = jnp.arange(nb)[:, None, None]

        def gather(idx):
            in_past = (idx < past)[..., None, None]
            pidx = jnp.minimum(idx, past - 1)
            phys = page_table[bidx, pidx // PAGE_SIZE]
            off = pidx % PAGE_SIZE
            nidx = jnp.clip(idx - past, 0, t - 1)
            ks = jnp.where(in_past, cache_k_l[phys, off].astype(k_new.dtype), k_new[bidx, nidx])
            vs = jnp.where(in_past, cache_v_l[phys, off].astype(v_new.dtype), v_new[bidx, nidx])
            return ks, vs

        return sparse_attend(q, q_idx, w_idx, pos, k_idx_all, gather)
    return attend


def trunk_layer(x, pos, h0_re, h0_im, attend, norm_mix, w_in, q_norm, k_norm,
                a_re, a_im, log_dt, b_re, b_im, c_re, c_im, d, w_glu, w_branch, w_out,
                norm_ffn, w_ffn_up, w_ffn_down):
    nb, s = x.shape[0], x.shape[1]
    xn = rmsnorm(x, norm_mix)
    proj = xn @ w_in
    u, q, k, v, qi, ki, wi, gl = jnp.split(proj, np.cumsum(IN_SIZES).tolist(), axis=-1)
    q = rope(rmsnorm(q.reshape(nb, s, N_HEADS, HEAD_DIM), q_norm), pos)
    k = rope(rmsnorm(k.reshape(nb, s, N_KV_HEADS, HEAD_DIM), k_norm), pos)
    v = v.reshape(nb, s, N_KV_HEADS, HEAD_DIM)
    qi = rope(qi.reshape(nb, s, N_IDX_HEADS, IDX_DIM), pos)
    ki = rope(ki.reshape(nb, s, 1, IDX_DIM), pos)[:, :, 0]
    y_att = attend(q, k, v, qi, ki, wi, pos)
    y_ssm, h_re, h_im = s5_scan(u, h0_re, h0_im, a_re, a_im, log_dt, b_re, b_im, c_re, c_im, d)
    y_ssm = jax.nn.gelu(y_ssm)
    y_ssm = y_ssm * jax.nn.sigmoid(y_ssm @ w_glu)
    gates = jax.nn.sigmoid(gl.astype(jnp.float32)).reshape(nb, s, N_BRANCH, D_MODEL)
    merged = gates[:, :, 0] * (y_ssm @ w_branch[0]) + gates[:, :, 1] * (y_att @ w_branch[1])
    x = x + merged.astype(x.dtype) @ w_out
    hn = rmsnorm(x, norm_ffn)
    gate, up = jnp.split(hn @ w_ffn_up, 2, axis=-1)
    x = x + (jax.nn.silu(gate) * up) @ w_ffn_down
    return x, k, v, ki, h_re, h_im


def setup_inputs(seed: int = 0) -> dict:
    key = jax.random.key(seed)
    ks = jax.random.split(key, 32)
    n_pages = PAST_LEN // PAGE_SIZE
    n_used = DEC_BATCH * n_pages
    n_pool = n_used + max(1, n_used // 4)
    f32 = jnp.float32

    def nrm(k, shape, scale=1.0):
        return jax.random.normal(k, shape, f32) * scale

    page_table = jax.random.permutation(ks[7], n_pool)[:n_used].reshape(DEC_BATCH, n_pages).astype(jnp.int32)
    a_im_base = math.pi * jnp.arange(SSM_P, dtype=f32)
    return {
        "x_prompt": nrm(ks[0], (BATCH, SEQ, D_MODEL)),
        "x_sample": nrm(ks[1], (DEC_BATCH, DEC_SEQ, D_MODEL)),
        "cache_k": nrm(ks[2], (DEPTH, n_pool, PAGE_SIZE, N_KV_HEADS, HEAD_DIM)),
        "cache_v": nrm(ks[3], (DEPTH, n_pool, PAGE_SIZE, N_KV_HEADS, HEAD_DIM)),
        "cache_kidx": nrm(ks[4], (DEPTH, n_pool, PAGE_SIZE, IDX_DIM)),
        "state_ssm_re": nrm(ks[5], (DEPTH, DEC_BATCH, SSM_GROUPS, SSM_P), 0.1),
        "state_ssm_im": nrm(ks[6], (DEPTH, DEC_BATCH, SSM_GROUPS, SSM_P), 0.1),
        "page_table": page_table,
        "norm_mix": 1.0 + nrm(ks[8], (DEPTH, D_MODEL), 0.02),
        "w_in": nrm(ks[9], (DEPTH, D_MODEL, IN_COLS), D_MODEL ** -0.5),
        "q_norm": 1.0 + nrm(ks[10], (DEPTH, HEAD_DIM), 0.02),
        "k_norm": 1.0 + nrm(ks[11], (DEPTH, HEAD_DIM), 0.02),
        "ssm_a_re": -0.5 + nrm(ks[12], (DEPTH, SSM_GROUPS, SSM_P), 0.01),
        "ssm_a_im": a_im_base + nrm(ks[13], (DEPTH, SSM_GROUPS, SSM_P), 0.01),
        "ssm_log_dt": jax.random.uniform(ks[14], (DEPTH, SSM_GROUPS), f32, math.log(DT_MIN), math.log(DT_MAX)),
        "ssm_b_re": nrm(ks[15], (DEPTH, SSM_GROUPS, SSM_P, SSM_GROUP), (2 * SSM_GROUP) ** -0.5),
        "ssm_b_im": nrm(ks[16], (DEPTH, SSM_GROUPS, SSM_P, SSM_GROUP), (2 * SSM_GROUP) ** -0.5),
        "ssm_c_re": nrm(ks[17], (DEPTH, SSM_GROUPS, SSM_GROUP, SSM_P), SSM_P ** -0.5),
        "ssm_c_im": nrm(ks[18], (DEPTH, SSM_GROUPS, SSM_GROUP, SSM_P), SSM_P ** -0.5),
        "ssm_d": nrm(ks[19], (DEPTH, SSM_W)),
        "w_glu": nrm(ks[20], (DEPTH, SSM_W, SSM_W), SSM_W ** -0.5),
        "w_branch": nrm(ks[21], (DEPTH, N_BRANCH, BR_W, D_MODEL), BR_W ** -0.5),
        "w_out": nrm(ks[22], (DEPTH, D_MODEL, D_MODEL), D_MODEL ** -0.5),
        "norm_ffn": 1.0 + nrm(ks[23], (DEPTH, D_MODEL), 0.02),
        "w_ffn_up": nrm(ks[24], (DEPTH, D_MODEL, 2 * D_FF), D_MODEL ** -0.5),
        "w_ffn_down": nrm(ks[25], (DEPTH, D_FF, D_MODEL), D_FF ** -0.5),
    }


def reference(x_prompt, x_sample, cache_k, cache_v, cache_kidx, state_ssm_re, state_ssm_im, page_table,
              norm_mix, w_in, q_norm, k_norm, ssm_a_re, ssm_a_im, ssm_log_dt, ssm_b_re, ssm_b_im,
              ssm_c_re, ssm_c_im, ssm_d, w_glu, w_branch, w_out, norm_ffn, w_ffn_up, w_ffn_down):
    nb = x_prompt.shape[0]
    pos_p = jnp.arange(x_prompt.shape[1], dtype=jnp.int32)
    pos_s = PAST_LEN + jnp.arange(x_sample.shape[1], dtype=jnp.int32)
    h_zero = jnp.zeros((nb, SSM_GROUPS, SSM_P), jnp.float32)
    xp, xs = x_prompt, x_sample
    kp_l, vp_l, kip_l, hrp_l, hip_l = [], [], [], [], []
    ks_l, vs_l, kis_l, hrs_l, his_l = [], [], [], [], []
    for l in range(DEPTH):
        lp = (norm_mix[l], w_in[l], q_norm[l], k_norm[l], ssm_a_re[l], ssm_a_im[l], ssm_log_dt[l],
              ssm_b_re[l], ssm_b_im[l], ssm_c_re[l], ssm_c_im[l], ssm_d[l], w_glu[l], w_branch[l],
              w_out[l], norm_ffn[l], w_ffn_up[l], w_ffn_down[l])
        xp, kp, vp, kip, hrp, hip = trunk_layer(xp, pos_p, h_zero, h_zero, attend_prompt, *lp)
        attend_s = make_sample_attend(cache_k[l], cache_v[l], cache_kidx[l], page_table)
        xs, kss, vss, kis, hrs, his = trunk_layer(xs, pos_s, state_ssm_re[l], state_ssm_im[l], attend_s, *lp)
        kp_l.append(kp); vp_l.append(vp); kip_l.append(kip); hrp_l.append(hrp); hip_l.append(hip)
        ks_l.append(kss); vs_l.append(vss); kis_l.append(kis); hrs_l.append(hrs); his_l.append(his)
    return (xp, xs,
            jnp.stack(kp_l), jnp.stack(vp_l), jnp.stack(kip_l), jnp.stack(hrp_l), jnp.stack(hip_l),
            jnp.stack(ks_l), jnp.stack(vs_l), jnp.stack(kis_l), jnp.stack(hrs_l), jnp.stack(his_l))
```

```python
import functools
import math

import jax
import jax.numpy as jnp
from jax import lax
from jax.experimental import pallas as pl
from jax.experimental.pallas import tpu as pltpu

F32 = jnp.float32
BF16 = jnp.bfloat16

SSM_W = 1024
SSM_GROUP = 16
SSM_GROUPS = SSM_W // SSM_GROUP
SSM_P = 64
SSM_STATE = SSM_GROUPS * SSM_P
N_HEADS = 8
N_KV_HEADS = 4
HEAD_DIM = 128
GQA_GROUP = N_HEADS // N_KV_HEADS
ATT_W = N_HEADS * HEAD_DIM
KV_W = N_KV_HEADS * HEAD_DIM
N_IDX_HEADS = 16
IDX_DIM = 64
IDX_W = N_IDX_HEADS * IDX_DIM
IDX_SCALE = (IDX_DIM * N_IDX_HEADS) ** -0.5
ATT_SCALE = HEAD_DIM ** -0.5
TOPK_MAX = 256
PAGE_SIZE = 128
ROPE_THETA = 10000.0
EPS = 1e-6

LANES = 128
SUBLANES = 8
VMEM_LIMIT = 56 * 1024 * 1024

COL_U = 0
COL_Q = SSM_W
COL_K = COL_Q + ATT_W
COL_V = COL_K + KV_W
COL_QI = COL_V + KV_W
COL_GATE = COL_QI + IDX_W
PROJ_TN = 768

MASKED = -1e30
INT_MIN = -(2 ** 31)
KEY_NEG_INF = -2139095041

SCAN_LANES = 512
SSM_BLOCKS = SSM_W // LANES
SSM_BLOCK_STATE = SSM_STATE // SSM_BLOCKS


def _params(**kw):
    return pltpu.CompilerParams(vmem_limit_bytes=VMEM_LIMIT, **kw)


def _pick(n, cands):
    for c in cands:
        if n % c == 0:
            return c
    return n


def _nt_dot(a, b):
    return lax.dot_general(a, b, (((1,), (1,)), ((), ())), preferred_element_type=F32)


def _dot(a, b):
    return jnp.dot(a, b, preferred_element_type=F32)


def _sigmoid(x):
    return 1.0 / (1.0 + jnp.exp(-x))


def _gelu_tanh(x):
    c = math.sqrt(2.0 / math.pi)
    return 0.5 * x * (1.0 + jnp.tanh(c * (x + 0.044715 * (x * x * x))))


def _sort_key(s):
    bits = pltpu.bitcast(s, jnp.int32)
    return bits ^ ((bits >> 31) & jnp.int32(0x7FFFFFFF))


def _rmsnorm_body(x_ref, g_ref, o_ref):
    x = x_ref[...].astype(F32)
    ms = jnp.mean(x * x, axis=-1, keepdims=True)
    o_ref[...] = ((x * lax.rsqrt(ms + EPS)) * g_ref[...]).astype(o_ref.dtype)


def _rmsnorm(x, g, out_dtype):
    m, d = x.shape
    tm = _pick(m, (512, 256, 128, 8))
    return pl.pallas_call(
        _rmsnorm_body,
        grid=(m // tm,),
        in_specs=[pl.BlockSpec((tm, d), lambda i: (i, 0)), pl.BlockSpec((1, d), lambda i: (0, 0))],
        out_specs=pl.BlockSpec((tm, d), lambda i: (i, 0)),
        out_shape=jax.ShapeDtypeStruct((m, d), out_dtype),
        compiler_params=_params(),
        name="rmsnorm",
    )(x, g.reshape(1, d))


def _mm_body(a_ref, w_ref, o_ref):
    o_ref[...] = _dot(a_ref[...].astype(BF16), w_ref[...]).astype(o_ref.dtype)


def _matmul(a, w, tn, out_dtype):
    m, k = a.shape
    n = w.shape[1]
    tm = _pick(m, (512, 256, 128, 8))
    return pl.pallas_call(
        _mm_body,
        grid=(n // tn, m // tm),
        in_specs=[pl.BlockSpec((tm, k), lambda j, i: (i, 0)), pl.BlockSpec((k, tn), lambda j, i: (0, j))],
        out_specs=pl.BlockSpec((tm, tn), lambda j, i: (i, j)),
        out_shape=jax.ShapeDtypeStruct((m, n), out_dtype),
        compiler_params=_params(),
        name="in_proj",
    )(a, w)


def _glu_body(g_ref, w_ref, o_ref):
    g = g_ref[...]
    z = _dot(g.astype(BF16), w_ref[...])
    o_ref[...] = (g * _sigmoid(z)).astype(o_ref.dtype)


def _glu(g_tb, w, nb, out_dtype):
    rows, n = g_tb.shape
    s = rows // nb
    tm = _pick(s, (512, 256, 128, 8)) if s >= SUBLANES else s
    if s < SUBLANES:
        return pl.pallas_call(
            _glu_body,
            grid=(1,),
            in_specs=[pl.BlockSpec((rows, n), lambda i: (0, 0)), pl.BlockSpec((n, n), lambda i: (0, 0))],
            out_specs=pl.BlockSpec((rows, n), lambda i: (0, 0)),
            out_shape=jax.ShapeDtypeStruct((rows, n), out_dtype),
            compiler_params=_params(),
            name="glu",
        )(g_tb, w)
    out = pl.pallas_call(
        _glu_body,
        grid=(nb, s // tm),
        in_specs=[pl.BlockSpec((tm, n), lambda b, i: (i, b)), pl.BlockSpec((n, n), lambda b, i: (0, 0))],
        out_specs=pl.BlockSpec((None, tm, n), lambda b, i: (b, i, 0)),
        out_shape=jax.ShapeDtypeStruct((nb, s, n), out_dtype),
        compiler_params=_params(),
        name="glu",
    )(g_tb.reshape(s, nb * n), w)
    return out.reshape(rows, n)


def _merge_body(a0_ref, a1_ref, w0_ref, w1_ref, g0_ref, g1_ref, o_ref):
    y0 = _dot(a0_ref[...].astype(BF16), w0_ref[...])
    y1 = _dot(a1_ref[...].astype(BF16), w1_ref[...])
    o_ref[...] = (_sigmoid(g0_ref[...]) * y0 + _sigmoid(g1_ref[...]) * y1).astype(o_ref.dtype)


def _merge(a0, a1, w0, w1, proj, out_dtype):
    m, k = a0.shape
    n = w0.shape[1]
    tm = _pick(m, (512, 256, 128, 8))
    tn = _pick(n, (1024, 512, 256, 128))
    g0_blk = COL_GATE // tn
    g1_blk = (COL_GATE + n) // tn
    return pl.pallas_call(
        _merge_body,
        grid=(n // tn, m // tm),
        in_specs=[
            pl.BlockSpec((tm, k), lambda j, i: (i, 0)),
            pl.BlockSpec((tm, k), lambda j, i: (i, 0)),
            pl.BlockSpec((k, tn), lambda j, i: (0, j)),
            pl.BlockSpec((k, tn), lambda j, i: (0, j)),
            pl.BlockSpec((tm, tn), lambda j, i: (i, g0_blk + j)),
            pl.BlockSpec((tm, tn), lambda j, i: (i, g1_blk + j)),
        ],
        out_specs=pl.BlockSpec((tm, tn), lambda j, i: (i, j)),
        out_shape=jax.ShapeDtypeStruct((m, n), out_dtype),
        compiler_params=_params(),
        name="merge",
    )(a0, a1, w0, w1, proj, proj)


def _out_norm_body(a_ref, w_ref, x_ref, g_ref, x1_ref, hn_ref):
    x1 = x_ref[...] + _dot(a_ref[...].astype(BF16), w_ref[...])
    x1_ref[...] = x1
    ms = jnp.mean(x1 * x1, axis=-1, keepdims=True)
    hn_ref[...] = ((x1 * lax.rsqrt(ms + EPS)) * g_ref[...]).astype(hn_ref.dtype)


def _out_norm(a, w, x, g, hn_dtype):
    m, k = a.shape
    n = w.shape[1]
    tm = _pick(m, (256, 128, 8))
    return pl.pallas_call(
        _out_norm_body,
        grid=(m // tm,),
        in_specs=[
            pl.BlockSpec((tm, k), lambda i: (i, 0)),
            pl.BlockSpec((k, n), lambda i: (0, 0)),
            pl.BlockSpec((tm, n), lambda i: (i, 0)),
            pl.BlockSpec((1, n), lambda i: (0, 0)),
        ],
        out_specs=[pl.BlockSpec((tm, n), lambda i: (i, 0)), pl.BlockSpec((tm, n), lambda i: (i, 0))],
        out_shape=[jax.ShapeDtypeStruct((m, n), F32), jax.ShapeDtypeStruct((m, n), hn_dtype)],
        compiler_params=_params(),
        name="out_proj_norm",
    )(a, w, x, g.reshape(1, n))


def _ffn_up_body(a_ref, wg_ref, wu_ref, o_ref):
    a = a_ref[...].astype(BF16)
    gate = _dot(a, wg_ref[...])
    up = _dot(a, wu_ref[...])
    o_ref[...] = ((gate * _sigmoid(gate)) * up).astype(o_ref.dtype)


def _ffn_up(a, w_up, out_dtype):
    m, k = a.shape
    d_ff = w_up.shape[1] // 2
    tm = _pick(m, (512, 256, 128, 8))
    tn = _pick(d_ff, (512, 256, 128))
    up_blk = d_ff // tn
    return pl.pallas_call(
        _ffn_up_body,
        grid=(d_ff // tn, m // tm),
        in_specs=[
            pl.BlockSpec((tm, k), lambda j, i: (i, 0)),
            pl.BlockSpec((k, tn), lambda j, i: (0, j)),
            pl.BlockSpec((k, tn), lambda j, i: (0, up_blk + j)),
        ],
        out_specs=pl.BlockSpec((tm, tn), lambda j, i: (i, j)),
        out_shape=jax.ShapeDtypeStruct((m, d_ff), out_dtype),
        compiler_params=_params(),
        name="ffn_up",
    )(a, w_up, w_up)


def _ffn_down_body(a_ref, w_ref, x_ref, o_ref):
    o_ref[...] = x_ref[...] + _dot(a_ref[...].astype(BF16), w_ref[...])


def _ffn_down(a, w, x):
    m, k = a.shape
    n = w.shape[1]
    tm = _pick(m, (256, 128, 8))
    tn = _pick(n, (1024, 512, 256, 128))
    return pl.pallas_call(
        _ffn_down_body,
        grid=(n // tn, m // tm),
        in_specs=[
            pl.BlockSpec((tm, k), lambda j, i: (i, 0)),
            pl.BlockSpec((k, tn), lambda j, i: (0, j)),
            pl.BlockSpec((tm, tn), lambda j, i: (i, j)),
        ],
        out_specs=pl.BlockSpec((tm, tn), lambda j, i: (i, j)),
        out_shape=jax.ShapeDtypeStruct((m, n), F32),
        compiler_params=_params(),
        name="ffn_down",
    )(a, w, x)


def _rope_tables(pos):
    def tables(dim):
        half = dim // 2
        freqs = ROPE_THETA ** (-jnp.arange(half, dtype=F32) / half)
        ang = pos.astype(F32)[:, None] * freqs[None, :]
        cos, sin = jnp.cos(ang), jnp.sin(ang)
        reps = LANES // dim
        cos_t = jnp.tile(jnp.concatenate([cos, cos], axis=1), (1, reps))
        sin_t = jnp.tile(jnp.concatenate([-sin, sin], axis=1), (1, reps))
        return cos_t, sin_t
    return tables(HEAD_DIM) + tables(IDX_DIM)


def _rope_head(x, cos_t, sin_t):
    return x * cos_t + pltpu.roll(x, HEAD_DIM // 2, axis=1) * sin_t


def _rope_idx(x, cos_t, sin_t, first_half):
    half = IDX_DIM // 2
    partner = jnp.where(first_half, pltpu.roll(x, LANES - half, axis=1), pltpu.roll(x, half, axis=1))
    return x * cos_t + partner * sin_t


def _head_norm(x, g):
    ms = jnp.mean(x * x, axis=-1, keepdims=True)
    return (x * lax.rsqrt(ms + EPS)) * g


def _qk_post_body(q_ref, k_ref, v_ref, qi_ref, kiw_ref, ch_ref, sh_ref, ci_ref, si_ref, qn_ref, kn_ref,
                  qo_ref, ko_ref, kb_ref, vo_ref, vb_ref, qio_ref, kio_ref, kilo_ref, kihi_ref):
    ch, sh, ci, si = ch_ref[...], sh_ref[...], ci_ref[...], si_ref[...]
    lane = lax.broadcasted_iota(jnp.int32, ch.shape, 1)
    first_half = (lane & (IDX_DIM - 1)) < (IDX_DIM // 2)
    for h in range(N_HEADS):
        sl = slice(h * HEAD_DIM, (h + 1) * HEAD_DIM)
        qo_ref[:, sl] = _rope_head(_head_norm(q_ref[:, sl], qn_ref[...]), ch, sh).astype(qo_ref.dtype)
    for h in range(N_KV_HEADS):
        sl = slice(h * HEAD_DIM, (h + 1) * HEAD_DIM)
        kr = _rope_head(_head_norm(k_ref[:, sl], kn_ref[...]), ch, sh)
        ko_ref[:, sl] = kr
        kb_ref[:, sl] = kr.astype(kb_ref.dtype)
    v = v_ref[...]
    vo_ref[...] = v
    vb_ref[...] = v.astype(vb_ref.dtype)
    for h in range(IDX_W // LANES):
        sl = slice(h * LANES, (h + 1) * LANES)
        qio_ref[:, sl] = _rope_idx(qi_ref[:, sl], ci, si, first_half).astype(qio_ref.dtype)
    kir = _rope_idx(kiw_ref[...], ci, si, first_half)
    kio_ref[...] = kir[:, :IDX_DIM]
    zero = jnp.zeros_like(kir)
    kilo_ref[...] = jnp.where(lane < IDX_DIM, kir, zero).astype(kilo_ref.dtype)
    kihi_ref[...] = jnp.where(lane < IDX_DIM, zero, pltpu.roll(kir, IDX_DIM, axis=1)).astype(kihi_ref.dtype)


def _qk_post(proj, ki_col, tabs, tab_rows, q_norm, k_norm, act_dtype):
    m = proj.shape[0]
    tm = _pick(min(m, tab_rows), (512, 256, 128, 8))
    n_tab = tab_rows // tm
    col = lambda off, w: (lambda i: (i, off // w))
    tab_spec = pl.BlockSpec((tm, LANES), lambda i: (i % n_tab, 0))
    row = lambda w: pl.BlockSpec((tm, w), lambda i: (i, 0))
    ki_blk = ki_col // LANES
    outs = pl.pallas_call(
        _qk_post_body,
        grid=(m // tm,),
        in_specs=[
            pl.BlockSpec((tm, ATT_W), col(COL_Q, ATT_W)),
            pl.BlockSpec((tm, KV_W), col(COL_K, KV_W)),
            pl.BlockSpec((tm, KV_W), col(COL_V, KV_W)),
            pl.BlockSpec((tm, IDX_W), col(COL_QI, IDX_W)),
            pl.BlockSpec((tm, LANES), lambda i: (i, ki_blk)),
            tab_spec, tab_spec, tab_spec, tab_spec,
            pl.BlockSpec((1, HEAD_DIM), lambda i: (0, 0)),
            pl.BlockSpec((1, HEAD_DIM), lambda i: (0, 0)),
        ],
        out_specs=[row(ATT_W), row(KV_W), row(KV_W), row(KV_W), row(KV_W), row(IDX_W), row(IDX_DIM),
                   row(LANES), row(LANES)],
        out_shape=[
            jax.ShapeDtypeStruct((m, ATT_W), act_dtype),
            jax.ShapeDtypeStruct((m, KV_W), F32),
            jax.ShapeDtypeStruct((m, KV_W), act_dtype),
            jax.ShapeDtypeStruct((m, KV_W), F32),
            jax.ShapeDtypeStruct((m, KV_W), act_dtype),
            jax.ShapeDtypeStruct((m, IDX_W), act_dtype),
            jax.ShapeDtypeStruct((m, IDX_DIM), F32),
            jax.ShapeDtypeStruct((m, LANES), act_dtype),
            jax.ShapeDtypeStruct((m, LANES), act_dtype),
        ],
        compiler_params=_params(),
        name="qk_post",
    )(proj, proj, proj, proj, proj, *tabs, q_norm.reshape(1, HEAD_DIM), k_norm.reshape(1, HEAD_DIM))
    return outs


def _kth_largest_key(key, k):
    rows = key.shape[0]

    def body(it, res):
        trial = res + jnp.left_shift(jnp.int32(1), 31 - it)
        cnt = jnp.sum(jnp.where(key >= trial, 1.0, 0.0), axis=1, keepdims=True)
        return jnp.where(cnt >= float(k), trial, res)

    return lax.fori_loop(0, 32, body, jnp.full((rows, 1), INT_MIN, jnp.int32))


def _attn_prompt_body(q_ref, qi_ref, kiw_ref, k_ref, v_ref, kilo_ref, kihi_ref, o_ref, *, topk):
    i = pl.program_id(1)
    tq = q_ref.shape[0]
    s = k_ref.shape[0]
    ki_lo, ki_hi = kilo_ref[...], kihi_ref[...]
    wi = kiw_ref[:, IDX_DIM:IDX_DIM + N_IDX_HEADS]
    score = jnp.zeros((tq, s), F32)
    for pair in range(N_IDX_HEADS // 2):
        x = qi_ref[:, pair * LANES:(pair + 1) * LANES]
        d0 = _nt_dot(x, ki_lo)
        d1 = _nt_dot(x, ki_hi)
        score = score + wi[:, 2 * pair:2 * pair + 1] * jnp.maximum(d0, 0.0)
        score = score + wi[:, 2 * pair + 1:2 * pair + 2] * jnp.maximum(d1, 0.0)
    score = score * IDX_SCALE
    row = i * tq + lax.broadcasted_iota(jnp.int32, (tq, s), 0)
    colk = lax.broadcasted_iota(jnp.int32, (tq, s), 1)
    key = jnp.where(colk <= row, _sort_key(score), INT_MIN)
    thr = _kth_largest_key(key, topk)
    sel = (key >= thr) & (key > KEY_NEG_INF)
    for g in range(N_KV_HEADS):
        kg = k_ref[:, g * HEAD_DIM:(g + 1) * HEAD_DIM]
        vg = v_ref[:, g * HEAD_DIM:(g + 1) * HEAD_DIM]
        for r in range(GQA_GROUP):
            h = g * GQA_GROUP + r
            sl = slice(h * HEAD_DIM, (h + 1) * HEAD_DIM)
            lg = jnp.where(sel, _nt_dot(q_ref[:, sl], kg) * ATT_SCALE, MASKED)
            mx = jnp.max(lg, axis=1, keepdims=True)
            p = jnp.exp(lg - mx)
            den = jnp.sum(p, axis=1, keepdims=True)
            o_ref[:, sl] = (_dot(p.astype(BF16), vg) / den).astype(o_ref.dtype)


def _attn_prompt(q, qi, proj, ki_col, k, v, ki_lo, ki_hi, nb, s):
    tq = _pick(s, (256, 128))
    nq = s // tq
    topk = min(TOPK_MAX, s // 4)
    ki_blk = ki_col // LANES
    qrow = lambda w: pl.BlockSpec((tq, w), lambda b, i: (b * nq + i, 0))
    full = lambda w: pl.BlockSpec((s, w), lambda b, i: (b, 0))
    return pl.pallas_call(
        functools.partial(_attn_prompt_body, topk=topk),
        grid=(nb, nq),
        in_specs=[qrow(ATT_W), qrow(IDX_W), pl.BlockSpec((tq, LANES), lambda b, i: (b * nq + i, ki_blk)),
                  full(KV_W), full(KV_W), full(LANES), full(LANES)],
        out_specs=qrow(ATT_W),
        out_shape=jax.ShapeDtypeStruct((nb * s, ATT_W), BF16),
        compiler_params=_params(),
        name="attn_prompt",
    )(q, qi, proj, k, v, ki_lo, ki_hi)


def _s5_discretize_body(are_ref, aim_ref, ldt_ref, bre_ref, bim_ref, abre_ref, abim_ref, bbre_ref, bbim_ref):
    dt = jnp.exp(ldt_ref[...])
    lr, li = are_ref[...], aim_ref[...]
    mag = jnp.exp(lr * dt)
    ab_re, ab_im = mag * jnp.cos(li * dt), mag * jnp.sin(li * dt)
    zr, zi = ab_re - 1.0, ab_im
    den = lr * lr + li * li
    fr = (zr * lr + zi * li) / den
    fi = (zi * lr - zr * li) / den
    abre_ref[...] = ab_re
    abim_ref[...] = ab_im
    br, bi = bre_ref[...], bim_ref[...]
    bbre_ref[...] = fr * br - fi * bi
    bbim_ref[...] = fr * bi + fi * br


def _s5_discretize(a_re, a_im, log_dt, b_re, b_im):
    depth = a_re.shape[0]
    gp = pl.BlockSpec((None, SSM_GROUPS, 1, SSM_P), lambda l: (l, 0, 0, 0))
    gwp = pl.BlockSpec((None, SSM_GROUPS, SSM_GROUP, SSM_P), lambda l: (l, 0, 0, 0))
    bt = lambda b: jnp.swapaxes(b, 2, 3)
    a4 = lambda a: a.reshape(depth, SSM_GROUPS, 1, SSM_P)
    return pl.pallas_call(
        _s5_discretize_body,
        grid=(depth,),
        in_specs=[gp, gp, pl.BlockSpec((None, SSM_GROUPS, 1, 1), lambda l: (l, 0, 0, 0)), gwp, gwp],
        out_specs=[gp, gp, gwp, gwp],
        out_shape=[jax.ShapeDtypeStruct((depth, SSM_GROUPS, 1, SSM_P), F32)] * 2
        + [jax.ShapeDtypeStruct((depth, SSM_GROUPS, SSM_GROUP, SSM_P), F32)] * 2,
        compiler_params=_params(),
        name="s5_discretize",
    )(a4(a_re), a4(a_im), log_dt.reshape(depth, SSM_GROUPS, 1, 1), bt(b_re), bt(b_im))


def _block_diag_in(bb_t):
    depth = bb_t.shape[0]
    gpb = SSM_GROUPS // SSM_BLOCKS
    x = bb_t.reshape(depth, SSM_BLOCKS, gpb, SSM_GROUP, SSM_P)
    eye = jnp.eye(gpb, dtype=bb_t.dtype)
    out = jnp.einsum('lkgwp,gh->lkgwhp', x, eye)
    return out.reshape(depth, SSM_BLOCKS, gpb * SSM_GROUP, gpb * SSM_P).astype(BF16)


def _block_diag_out(c):
    depth = c.shape[0]
    gpb = SSM_GROUPS // SSM_BLOCKS
    x = c.reshape(depth, SSM_BLOCKS, gpb, SSM_GROUP, SSM_P)
    eye = jnp.eye(gpb, dtype=c.dtype)
    out = jnp.einsum('lkgwp,gh->lkgphw', x, eye)
    return out.reshape(depth, SSM_BLOCKS, gpb * SSM_P, gpb * SSM_GROUP).astype(BF16)


def _s5_body(u_ref, bbre_ref, bbim_ref, cre_ref, cim_ref, abre_ref, abim_ref, d_ref, h0re_ref, h0im_ref,
             g_ref, htre_ref, htim_ref, bure, buim, hre, him, *, nb):
    c = pl.program_id(0)
    rows = u_ref.shape[0]
    per_tile = SUBLANES // nb

    @pl.when(c == 0)
    def _():
        hre[...] = h0re_ref[...]
        him[...] = h0im_ref[...]

    u = u_ref[...]
    ub = u.astype(BF16)
    for k in range(SSM_BLOCKS):
        uk = ub[:, k * LANES:(k + 1) * LANES]
        st = slice(k * SSM_BLOCK_STATE, (k + 1) * SSM_BLOCK_STATE)
        bure[:, st] = _dot(uk, bbre_ref[k])
        buim[:, st] = _dot(uk, bbim_ref[k])

    for cc in range(SSM_STATE // SCAN_LANES):
        sl = slice(cc * SCAN_LANES, (cc + 1) * SCAN_LANES)
        ar = jnp.broadcast_to(abre_ref[:, sl], (nb, SCAN_LANES))
        ai = jnp.broadcast_to(abim_ref[:, sl], (nb, SCAN_LANES))

        def step(j, carry, sl=sl, ar=ar, ai=ai):
            hr, hi = carry
            r0 = pl.multiple_of(j * SUBLANES, SUBLANES)
            xr = bure[pl.ds(r0, SUBLANES), sl]
            xi = buim[pl.ds(r0, SUBLANES), sl]
            out_r, out_i = [], []
            for t in range(per_tile):
                nr = ar * hr - ai * hi + xr[t * nb:(t + 1) * nb]
                ni = ar * hi + ai * hr + xi[t * nb:(t + 1) * nb]
                hr, hi = nr, ni
                out_r.append(hr)
                out_i.append(hi)
            bure[pl.ds(r0, SUBLANES), sl] = out_r[0] if per_tile == 1 else jnp.concatenate(out_r, axis=0)
            buim[pl.ds(r0, SUBLANES), sl] = out_i[0] if per_tile == 1 else jnp.concatenate(out_i, axis=0)
            return hr, hi

        hr, hi = lax.fori_loop(0, rows // SUBLANES, step, (hre[:, sl], him[:, sl]))
        hre[:, sl] = hr
        him[:, sl] = hi

    for k in range(SSM_BLOCKS):
        st = slice(k * SSM_BLOCK_STATE, (k + 1) * SSM_BLOCK_STATE)
        ln = slice(k * LANES, (k + 1) * LANES)
        y = _dot(bure[:, st].astype(BF16), cre_ref[k]) - _dot(buim[:, st].astype(BF16), cim_ref[k])
        y = y + d_ref[:, ln] * u[:, ln]
        g_ref[:, ln] = _gelu_tanh(y)

    @pl.when(c == pl.num_programs(0) - 1)
    def _():
        htre_ref[...] = hre[...]
        htim_ref[...] = him[...]


def _s5(u_tb, nb, bb_re, bb_im, c_re, c_im, ab_re, ab_im, d, h0_re, h0_im):
    rows_total = u_tb.shape[0]
    rows = min(rows_total, 512)
    full3 = lambda a: pl.BlockSpec(a.shape, lambda c: (0, 0, 0))
    full2 = lambda a: pl.BlockSpec(a.shape, lambda c: (0, 0))
    vec = lambda a: a.reshape(1, -1)
    ab_re, ab_im, d = vec(ab_re), vec(ab_im), vec(d)
    return pl.pallas_call(
        functools.partial(_s5_body, nb=nb),
        grid=(rows_total // rows,),
        in_specs=[pl.BlockSpec((rows, SSM_W), lambda c: (c, 0)),
                  full3(bb_re), full3(bb_im), full3(c_re), full3(c_im),
                  full2(ab_re), full2(ab_im), full2(d), full2(h0_re), full2(h0_im)],
        out_specs=[pl.BlockSpec((rows, SSM_W), lambda c: (c, 0)), full2(h0_re), full2(h0_im)],
        out_shape=[jax.ShapeDtypeStruct((rows_total, SSM_W), F32),
                   jax.ShapeDtypeStruct(h0_re.shape, F32), jax.ShapeDtypeStruct(h0_im.shape, F32)],
        scratch_shapes=[pltpu.VMEM((rows, SSM_STATE), F32), pltpu.VMEM((rows, SSM_STATE), F32),
                        pltpu.VMEM((nb, SSM_STATE), F32), pltpu.VMEM((nb, SSM_STATE), F32)],
        compiler_params=_params(),
        name="s5_scan",
    )(u_tb, bb_re, bb_im, c_re, c_im, ab_re, ab_im, d, h0_re, h0_im)


IDX_PAGES_PER_STEP = 16
ATT_PAGES_PER_STEP = 8


def _sample_index_body(pt_ref, qi_ref, wi_ref, kin_ref, *rest, n_pg, topk):
    del pt_ref
    pages = rest[:n_pg]
    sc_ref, thr_ref, snew_ref = rest[n_pg:]
    c = pl.program_id(1)
    qi = qi_ref[...].astype(BF16)
    wi = wi_ref[...]

    def score_of(keys):
        d = _nt_dot(qi, keys.astype(BF16))
        return jnp.sum(wi * jnp.maximum(d, 0.0), axis=0, keepdims=True) * IDX_SCALE

    for p in range(n_pg):
        sc_ref[pl.ds(c * n_pg + p, 1), :] = score_of(pages[p][...])

    @pl.when(c == pl.num_programs(1) - 1)
    def _():
        s_new = score_of(jnp.broadcast_to(kin_ref[...], (LANES, IDX_DIM)))
        snew_ref[...] = s_new
        key = _sort_key(sc_ref[...])
        key_new = _sort_key(s_new)

        def body(it, res):
            trial = res + jnp.left_shift(jnp.int32(1), 31 - it)
            cnt = jnp.sum(jnp.where(key >= trial, 1.0, 0.0), axis=0, keepdims=True)
            cnt = jnp.sum(cnt, axis=1, keepdims=True) + jnp.where(key_new >= trial, 1.0, 0.0)
            return jnp.where(cnt >= float(topk), trial, res)

        thr_ref[...] = lax.fori_loop(0, 32, body, jnp.full((1, LANES), INT_MIN, jnp.int32))


def _sample_index(page_table, cache_kidx, layer, qi, wi, ki_new):
    nb, n_pages = page_table.shape
    n_pg = _pick(n_pages, (IDX_PAGES_PER_STEP, 8, 4, 2, 1))
    topk = min(TOPK_MAX, (n_pages * PAGE_SIZE + 1) // 4)
    page_spec = lambda p: pl.BlockSpec((None, None, PAGE_SIZE, IDX_DIM),
                                       lambda b, c, pt: (layer, pt[b, c * n_pg + p], 0, 0))
    per_b = lambda shape: pl.BlockSpec((None,) + shape, lambda b, c, pt: (b, 0, 0))
    grid_spec = pltpu.PrefetchScalarGridSpec(
        num_scalar_prefetch=1,
        grid=(nb, n_pages // n_pg),
        in_specs=[per_b((N_IDX_HEADS, IDX_DIM)), per_b((N_IDX_HEADS, 1)), per_b((1, IDX_DIM))]
        + [page_spec(p) for p in range(n_pg)],
        out_specs=[per_b((n_pages, LANES)), per_b((1, LANES)), per_b((1, LANES))],
    )
    return pl.pallas_call(
        functools.partial(_sample_index_body, n_pg=n_pg, topk=topk),
        grid_spec=grid_spec,
        out_shape=[jax.ShapeDtypeStruct((nb, n_pages, LANES), F32),
                   jax.ShapeDtypeStruct((nb, 1, LANES), jnp.int32),
                   jax.ShapeDtypeStruct((nb, 1, LANES), F32)],
        compiler_params=_params(),
        name="sample_index",
    )(page_table, qi.reshape(nb, N_IDX_HEADS, IDX_DIM), wi.reshape(nb, N_IDX_HEADS, 1),
      ki_new.reshape(nb, 1, IDX_DIM), *([cache_kidx] * n_pg))


def _sample_attn_body(pt_ref, sc_ref, thr_ref, snew_ref, q_ref, kn_ref, vn_ref, *rest, n_pg):
    del pt_ref
    kpages = rest[:n_pg]
    vpages = rest[n_pg:2 * n_pg]
    o_ref, m_scr, l_scr, acc_scr = rest[2 * n_pg:]
    c = pl.program_id(1)

    @pl.when(c == 0)
    def _():
        m_scr[...] = jnp.full(m_scr.shape, MASKED, F32)
        l_scr[...] = jnp.zeros(l_scr.shape, F32)
        acc_scr[...] = jnp.zeros(acc_scr.shape, F32)

    q = q_ref[...].astype(BF16)
    thr = thr_ref[...]
    head = lax.broadcasted_iota(jnp.int32, (N_HEADS, LANES), 0)
    in_group = [(head >= g * GQA_GROUP) & (head < (g + 1) * GQA_GROUP) for g in range(N_KV_HEADS)]
    for p in range(n_pg):
        key = _sort_key(sc_ref[p:p + 1, :])
        sel = (key >= thr) & (key > KEY_NEG_INF)
        kp = kpages[p][...].astype(BF16)
        vp = vpages[p][...].astype(BF16)
        lg = jnp.zeros((N_HEADS, PAGE_SIZE), F32)
        for g in range(N_KV_HEADS):
            d = _nt_dot(q, kp[:, g * HEAD_DIM:(g + 1) * HEAD_DIM])
            lg = jnp.where(in_group[g], d, lg)
        lg = jnp.where(sel, lg * ATT_SCALE, MASKED)
        m_old = m_scr[...]
        m_new = jnp.maximum(m_old, jnp.max(lg, axis=1, keepdims=True))
        alpha = jnp.exp(m_old - m_new)
        pr = jnp.where(sel, jnp.exp(lg - m_new), 0.0)
        l_scr[...] = alpha * l_scr[...] + jnp.sum(pr, axis=1, keepdims=True)
        prb = pr.astype(BF16)
        pv = jnp.zeros((N_HEADS, HEAD_DIM), F32)
        for g in range(N_KV_HEADS):
            d = _dot(prb, vp[:, g * HEAD_DIM:(g + 1) * HEAD_DIM])
            pv = jnp.where(in_group[g], d, pv)
        acc_scr[...] = alpha * acc_scr[...] + pv
        m_scr[...] = m_new

    @pl.when(c == pl.num_programs(1) - 1)
    def _():
        key_new = _sort_key(snew_ref[...])[:, :1]
        sel_new = (key_new >= thr[:, :1]) & (key_new > KEY_NEG_INF)
        qf = q.astype(F32)
        kn = kn_ref[...].astype(BF16).astype(F32)
        vn = vn_ref[...].astype(BF16).astype(F32)
        lg = jnp.where(sel_new, jnp.sum(qf * kn, axis=1, keepdims=True) * ATT_SCALE, MASKED)
        m_old = m_scr[...]
        m_new = jnp.maximum(m_old, lg)
        alpha = jnp.exp(m_old - m_new)
        pr = jnp.where(sel_new, jnp.exp(lg - m_new), 0.0)
        den = alpha * l_scr[...] + pr
        prq = pr.astype(BF16).astype(F32)
        o_ref[...] = ((alpha * acc_scr[...] + prq * vn) / den).astype(o_ref.dtype)


def _sample_attn(page_table, cache_k, cache_v, layer, scores, thr, s_new, q, k_new, v_new):
    nb, n_pages = page_table.shape
    n_pg = _pick(n_pages, (ATT_PAGES_PER_STEP, 4, 2, 1))
    page_spec = lambda p: pl.BlockSpec((None, None, PAGE_SIZE, KV_W),
                                       lambda b, c, pt: (layer, pt[b, c * n_pg + p], 0, 0))
    per_b = lambda shape: pl.BlockSpec((None,) + shape, lambda b, c, pt: (b, 0, 0))
    per_head = lambda a: jnp.repeat(a.reshape(nb, N_KV_HEADS, HEAD_DIM), GQA_GROUP, axis=1)
    grid_spec = pltpu.PrefetchScalarGridSpec(
        num_scalar_prefetch=1,
        grid=(nb, n_pages // n_pg),
        in_specs=[pl.BlockSpec((None, n_pg, LANES), lambda b, c, pt: (b, c, 0)),
                  per_b((1, LANES)), per_b((1, LANES)),
                  per_b((N_HEADS, HEAD_DIM)), per_b((N_HEADS, HEAD_DIM)), per_b((N_HEADS, HEAD_DIM))]
        + [page_spec(p) for p in range(n_pg)] * 2,
        out_specs=per_b((N_HEADS, HEAD_DIM)),
        scratch_shapes=[pltpu.VMEM((N_HEADS, 1), F32), pltpu.VMEM((N_HEADS, 1), F32),
                        pltpu.VMEM((N_HEADS, HEAD_DIM), F32)],
    )
    ck = cache_k.reshape(cache_k.shape[:3] + (KV_W,))
    cv = cache_v.reshape(cache_v.shape[:3] + (KV_W,))
    out = pl.pallas_call(
        functools.partial(_sample_attn_body, n_pg=n_pg),
        grid_spec=grid_spec,
        out_shape=jax.ShapeDtypeStruct((nb, N_HEADS, HEAD_DIM), F32),
        compiler_params=_params(),
        name="sample_attn",
    )(page_table, scores, thr, s_new, q.reshape(nb, N_HEADS, HEAD_DIM), per_head(k_new), per_head(v_new),
      *([ck] * n_pg), *([cv] * n_pg))
    return out.reshape(nb, ATT_W)


def _layer_tail(proj, g_tb, nb, y_att, x, w, act_dtype):
    z = _glu(g_tb, w["w_glu"], nb, act_dtype)
    merged = _merge(z, y_att, w["w_branch0"], w["w_branch1"], proj, act_dtype)
    x1, hn = _out_norm(merged, w["w_out"], x, w["norm_ffn"], act_dtype)
    act = _ffn_up(hn, w["w_ffn_up"], act_dtype)
    return _ffn_down(act, w["w_ffn_down"], x1)


def _prompt_layer(x, nb, s, tabs, w):
    xn = _rmsnorm(x, w["norm_mix"], BF16)
    proj = _matmul(xn, w["w_in"], PROJ_TN, F32)
    q, k_out, k_bf, v_out, v_bf, qi, ki_out, ki_lo, ki_hi = _qk_post(
        proj, w["ki_col"], tabs, s, w["q_norm"], w["k_norm"], BF16)
    y_att = _attn_prompt(q, qi, proj, w["ki_col"], k_bf, v_bf, ki_lo, ki_hi, nb, s)
    u_tb = jnp.swapaxes(proj[:, :SSM_W].reshape(nb, s, SSM_W), 0, 1).reshape(s * nb, SSM_W)
    zeros = jnp.zeros((nb, SSM_STATE), F32)
    g_tb, h_re, h_im = _s5(u_tb, nb, w["bb_re"], w["bb_im"], w["c_re"], w["c_im"], w["ab_re"], w["ab_im"],
                           w["ssm_d"], zeros, zeros)
    x2 = _layer_tail(proj, g_tb, nb, y_att, x, w, BF16)
    return x2, k_out, v_out, ki_out, h_re, h_im


def _sample_layer(x, nb, tabs, w, layer, page_table, cache_k, cache_v, cache_kidx, h0_re, h0_im):
    xn = _rmsnorm(x, w["norm_mix"], F32)
    proj = _matmul(xn, w["w_in"], PROJ_TN, F32)
    ki_col = w["ki_col"]
    q, k_out, _, v_out, _, qi, ki_out, _, _ = _qk_post(proj, ki_col, tabs, nb, w["q_norm"], w["k_norm"], F32)
    wi = proj[:, ki_col + IDX_DIM:ki_col + IDX_DIM + N_IDX_HEADS]
    scores, thr, s_new = _sample_index(page_table, cache_kidx, layer, qi, wi, ki_out)
    y_att = _sample_attn(page_table, cache_k, cache_v, layer, scores, thr, s_new, q, k_out, v_out)
    g_tb, h_re, h_im = _s5(proj[:, :SSM_W], nb, w["bb_re"], w["bb_im"], w["c_re"], w["c_im"],
                           w["ab_re"], w["ab_im"], w["ssm_d"], h0_re, h0_im)
    x2 = _layer_tail(proj, g_tb, nb, y_att, x, w, F32)
    return x2, k_out, v_out, ki_out, h_re, h_im


def kernel(x_prompt, x_sample, cache_k, cache_v, cache_kidx, state_ssm_re, state_ssm_im, page_table, norm_mix, w_in, q_norm, k_norm, ssm_a_re, ssm_a_im, ssm_log_dt, ssm_b_re, ssm_b_im, ssm_c_re, ssm_c_im, ssm_d, w_glu, w_branch, w_out, norm_ffn, w_ffn_up, w_ffn_down):
    nb_p, s, d_model = x_prompt.shape
    nb_s, dec_seq, _ = x_sample.shape
    assert dec_seq == 1, "the sample group decodes one token per sequence"
    depth = w_in.shape[0]
    past = page_table.shape[1] * PAGE_SIZE

    ki_orig = COL_GATE
    gate_orig = ki_orig + IDX_DIM + N_IDX_HEADS
    n_gate = w_in.shape[2] - gate_orig
    ki_col = COL_GATE + n_gate
    assert ki_col % LANES == 0
    cols = ki_col + LANES
    cols_pad = -(-cols // PROJ_TN) * PROJ_TN
    w_in_r = jnp.concatenate(
        [w_in[:, :, :COL_GATE], w_in[:, :, gate_orig:], w_in[:, :, ki_orig:gate_orig],
         jnp.zeros((depth, d_model, cols_pad - ki_col - IDX_DIM - N_IDX_HEADS), w_in.dtype)], axis=2).astype(BF16)

    ab_re, ab_im, bbt_re, bbt_im = _s5_discretize(ssm_a_re, ssm_a_im, ssm_log_dt, ssm_b_re, ssm_b_im)
    bb_re, bb_im = _block_diag_in(bbt_re), _block_diag_in(bbt_im)
    c_re, c_im = _block_diag_out(ssm_c_re), _block_diag_out(ssm_c_im)
    w_glu_b, w_branch_b, w_out_b = w_glu.astype(BF16), w_branch.astype(BF16), w_out.astype(BF16)
    w_up_b, w_down_b = w_ffn_up.astype(BF16), w_ffn_down.astype(BF16)

    tabs_p = _rope_tables(jnp.arange(s, dtype=jnp.int32))
    tabs_s = _rope_tables(jnp.full((nb_s,), past, dtype=jnp.int32))

    xp = x_prompt.reshape(nb_p * s, d_model)
    xs = x_sample.reshape(nb_s, d_model)
    outs_p, outs_s = [], []
    for l in range(depth):
        w = dict(ki_col=ki_col, norm_mix=norm_mix[l], w_in=w_in_r[l], q_norm=q_norm[l], k_norm=k_norm[l],
                 bb_re=bb_re[l], bb_im=bb_im[l], c_re=c_re[l], c_im=c_im[l],
                 ab_re=ab_re[l], ab_im=ab_im[l], ssm_d=ssm_d[l], w_glu=w_glu_b[l],
                 w_branch0=w_branch_b[l, 0], w_branch1=w_branch_b[l, 1], w_out=w_out_b[l],
                 norm_ffn=norm_ffn[l], w_ffn_up=w_up_b[l], w_ffn_down=w_down_b[l])
        xp, *rest_p = _prompt_layer(xp, nb_p, s, tabs_p, w)
        xs, *rest_s = _sample_layer(xs, nb_s, tabs_s, w, l, page_table, cache_k, cache_v, cache_kidx,
                                    state_ssm_re[l].reshape(nb_s, SSM_STATE), state_ssm_im[l].reshape(nb_s, SSM_STATE))
        outs_p.append(rest_p)
        outs_s.append(rest_s)

    def stack(outs, idx, shape):
        return jnp.stack([o[idx].reshape(shape) for o in outs])

    kv_p, kv_s = (nb_p, s, N_KV_HEADS, HEAD_DIM), (nb_s, dec_seq, N_KV_HEADS, HEAD_DIM)
    st_p, st_s = (nb_p, SSM_GROUPS, SSM_P), (nb_s, SSM_GROUPS, SSM_P)
    return (xp.reshape(nb_p, s, d_model), xs.reshape(nb_s, dec_seq, d_model),
            stack(outs_p, 0, kv_p), stack(outs_p, 1, kv_p), stack(outs_p, 2, (nb_p, s, IDX_DIM)),
            stack(outs_p, 3, st_p), stack(outs_p, 4, st_p),
            stack(outs_s, 0, kv_s), stack(outs_s, 1, kv_s), stack(outs_s, 2, (nb_s, dec_seq, IDX_DIM)),
            stack(outs_s, 3, st_s), stack(outs_s, 4, st_s))
```

```python
import functools
import math

import jax
import jax.numpy as jnp
from jax import lax
from jax.experimental import pallas as pl
from jax.experimental.pallas import tpu as pltpu

F32 = jnp.float32
BF16 = jnp.bfloat16

SSM_W = 1024
SSM_GROUP = 16
SSM_GROUPS = SSM_W // SSM_GROUP
SSM_P = 64
SSM_STATE = SSM_GROUPS * SSM_P
N_HEADS = 8
N_KV_HEADS = 4
HEAD_DIM = 128
GQA_GROUP = N_HEADS // N_KV_HEADS
ATT_W = N_HEADS * HEAD_DIM
KV_W = N_KV_HEADS * HEAD_DIM
N_IDX_HEADS = 16
IDX_DIM = 64
IDX_W = N_IDX_HEADS * IDX_DIM
IDX_SCALE = (IDX_DIM * N_IDX_HEADS) ** -0.5
ATT_SCALE = HEAD_DIM ** -0.5
TOPK_MAX = 256
PAGE_SIZE = 128
ROPE_THETA = 10000.0
EPS = 1e-6

LANES = 128
SUBLANES = 8
VMEM_LIMIT = 56 * 1024 * 1024

COL_U = 0
COL_Q = SSM_W
COL_K = COL_Q + ATT_W
COL_V = COL_K + KV_W
COL_QI = COL_V + KV_W
COL_GATE = COL_QI + IDX_W
PROJ_TN = 768

MASKED = -1e30
INT_MIN = -(2 ** 31)
KEY_NEG_INF = -2139095041

SCAN_LANES = 512
SSM_BLOCKS = SSM_W // LANES
SSM_BLOCK_STATE = SSM_STATE // SSM_BLOCKS


def _params(**kw):
    return pltpu.CompilerParams(vmem_limit_bytes=VMEM_LIMIT, **kw)


def _pick(n, cands):
    for c in cands:
        if n % c == 0:
            return c
    return n


def _nt_dot(a, b):
    return lax.dot_general(a, b, (((1,), (1,)), ((), ())), preferred_element_type=F32)


def _dot(a, b):
    return jnp.dot(a, b, preferred_element_type=F32)


def _sigmoid(x):
    return 1.0 / (1.0 + jnp.exp(-x))


def _gelu_tanh(x):
    c = math.sqrt(2.0 / math.pi)
    return 0.5 * x * (1.0 + jnp.tanh(c * (x + 0.044715 * (x * x * x))))


def _sort_key(s):
    bits = pltpu.bitcast(s, jnp.int32)
    return bits ^ ((bits >> 31) & jnp.int32(0x7FFFFFFF))


def _rmsnorm_body(x_ref, g_ref, o_ref):
    x = x_ref[...].astype(F32)
    ms = jnp.mean(x * x, axis=-1, keepdims=True)
    o_ref[...] = ((x * lax.rsqrt(ms + EPS)) * g_ref[...]).astype(o_ref.dtype)


def _rmsnorm(x, g, out_dtype):
    m, d = x.shape
    tm = _pick(m, (512, 256, 128, 8))
    return pl.pallas_call(
        _rmsnorm_body,
        grid=(m // tm,),
        in_specs=[pl.BlockSpec((tm, d), lambda i: (i, 0)), pl.BlockSpec((1, d), lambda i: (0, 0))],
        out_specs=pl.BlockSpec((tm, d), lambda i: (i, 0)),
        out_shape=jax.ShapeDtypeStruct((m, d), out_dtype),
        compiler_params=_params(),
        name="rmsnorm",
    )(x, g.reshape(1, d))


def _mm_body(a_ref, w_ref, o_ref):
    o_ref[...] = _dot(a_ref[...].astype(BF16), w_ref[...]).astype(o_ref.dtype)


def _in_proj(a, w, l, tn, out_dtype):
    m, k = a.shape
    n = w.shape[2]
    tm = _pick(m, (1024, 512, 256, 128, 8))
    return pl.pallas_call(
        _mm_body,
        grid=(n // tn, m // tm),
        in_specs=[pl.BlockSpec((tm, k), lambda j, i: (i, 0)), pl.BlockSpec((None, k, tn), lambda j, i: (l, 0, j))],
        out_specs=pl.BlockSpec((tm, tn), lambda j, i: (i, j)),
        out_shape=jax.ShapeDtypeStruct((m, n), out_dtype),
        compiler_params=_params(),
        name="in_proj",
    )(a, w)


def _glu_body(g_ref, w_ref, o_ref):
    g = g_ref[...]
    z = _dot(g.astype(BF16), w_ref[...])
    o_ref[...] = (g * _sigmoid(z)).astype(o_ref.dtype)


def _glu(g, w, l, out_dtype):
    m, n = g.shape
    tm = _pick(m, (512, 256, 128, 8))
    return pl.pallas_call(
        _glu_body,
        grid=(m // tm,),
        in_specs=[pl.BlockSpec((tm, n), lambda i: (i, 0)), pl.BlockSpec((None, n, n), lambda i: (l, 0, 0))],
        out_specs=pl.BlockSpec((tm, n), lambda i: (i, 0)),
        out_shape=jax.ShapeDtypeStruct((m, n), out_dtype),
        compiler_params=_params(),
        name="glu",
    )(g, w)


def _merge_body(a0_ref, a1_ref, w0_ref, w1_ref, g0_ref, g1_ref, o_ref):
    y0 = _dot(a0_ref[...].astype(BF16), w0_ref[...])
    y1 = _dot(a1_ref[...].astype(BF16), w1_ref[...])
    o_ref[...] = (_sigmoid(g0_ref[...]) * y0 + _sigmoid(g1_ref[...]) * y1).astype(o_ref.dtype)


def _merge(a0, a1, w_branch, l, proj, out_dtype):
    m, k = a0.shape
    n = w_branch.shape[3]
    tm = _pick(m, (512, 256, 128, 8))
    tn = _pick(n, (1024, 512, 256, 128))
    g0_blk = COL_GATE // tn
    g1_blk = (COL_GATE + n) // tn
    wspec = lambda br: pl.BlockSpec((None, None, k, tn), lambda j, i: (l, br, 0, j))
    return pl.pallas_call(
        _merge_body,
        grid=(n // tn, m // tm),
        in_specs=[
            pl.BlockSpec((tm, k), lambda j, i: (i, 0)),
            pl.BlockSpec((tm, k), lambda j, i: (i, 0)),
            wspec(0), wspec(1),
            pl.BlockSpec((tm, tn), lambda j, i: (i, g0_blk + j)),
            pl.BlockSpec((tm, tn), lambda j, i: (i, g1_blk + j)),
        ],
        out_specs=pl.BlockSpec((tm, tn), lambda j, i: (i, j)),
        out_shape=jax.ShapeDtypeStruct((m, n), out_dtype),
        compiler_params=_params(),
        name="merge",
    )(a0, a1, w_branch, w_branch, proj, proj)


def _out_norm_body(a_ref, w_ref, x_ref, g_ref, x1_ref, hn_ref):
    x1 = x_ref[...] + _dot(a_ref[...].astype(BF16), w_ref[...])
    x1_ref[...] = x1
    ms = jnp.mean(x1 * x1, axis=-1, keepdims=True)
    hn_ref[...] = ((x1 * lax.rsqrt(ms + EPS)) * g_ref[...]).astype(hn_ref.dtype)


def _out_norm(a, w, l, x, g, hn_dtype):
    m, k = a.shape
    n = w.shape[2]
    tm = _pick(m, (256, 128, 8))
    return pl.pallas_call(
        _out_norm_body,
        grid=(m // tm,),
        in_specs=[
            pl.BlockSpec((tm, k), lambda i: (i, 0)),
            pl.BlockSpec((None, k, n), lambda i: (l, 0, 0)),
            pl.BlockSpec((tm, n), lambda i: (i, 0)),
            pl.BlockSpec((1, n), lambda i: (0, 0)),
        ],
        out_specs=[pl.BlockSpec((tm, n), lambda i: (i, 0)), pl.BlockSpec((tm, n), lambda i: (i, 0))],
        out_shape=[jax.ShapeDtypeStruct((m, n), F32), jax.ShapeDtypeStruct((m, n), hn_dtype)],
        compiler_params=_params(),
        name="out_proj_norm",
    )(a, w, x, g.reshape(1, n))


def _ffn_up_body(a_ref, wg_ref, wu_ref, o_ref):
    a = a_ref[...].astype(BF16)
    gate = _dot(a, wg_ref[...])
    up = _dot(a, wu_ref[...])
    o_ref[...] = ((gate * _sigmoid(gate)) * up).astype(o_ref.dtype)


def _ffn_up(a, w_up, l, out_dtype):
    m, k = a.shape
    d_ff = w_up.shape[2] // 2
    tm = _pick(m, (1024, 512, 256, 128, 8))
    tn = _pick(d_ff, (512, 256, 128))
    up_blk = d_ff // tn
    return pl.pallas_call(
        _ffn_up_body,
        grid=(d_ff // tn, m // tm),
        in_specs=[
            pl.BlockSpec((tm, k), lambda j, i: (i, 0)),
            pl.BlockSpec((None, k, tn), lambda j, i: (l, 0, j)),
            pl.BlockSpec((None, k, tn), lambda j, i: (l, 0, up_blk + j)),
        ],
        out_specs=pl.BlockSpec((tm, tn), lambda j, i: (i, j)),
        out_shape=jax.ShapeDtypeStruct((m, d_ff), out_dtype),
        compiler_params=_params(),
        name="ffn_up",
    )(a, w_up, w_up)


def _ffn_down_body(a_ref, w_ref, x_ref, o_ref):
    o_ref[...] = x_ref[...] + _dot(a_ref[...].astype(BF16), w_ref[...])


def _ffn_down(a, w, l, x):
    m, k = a.shape
    n = w.shape[2]
    tm = _pick(m, (256, 128, 8))
    tn = _pick(n, (1024, 512, 256, 128))
    return pl.pallas_call(
        _ffn_down_body,
        grid=(n // tn, m // tm),
        in_specs=[
            pl.BlockSpec((tm, k), lambda j, i: (i, 0)),
            pl.BlockSpec((None, k, tn), lambda j, i: (l, 0, j)),
            pl.BlockSpec((tm, tn), lambda j, i: (i, j)),
        ],
        out_specs=pl.BlockSpec((tm, tn), lambda j, i: (i, j)),
        out_shape=jax.ShapeDtypeStruct((m, n), F32),
        compiler_params=_params(),
        name="ffn_down",
    )(a, w, x)


def _rope_tables(pos):
    def tables(dim):
        half = dim // 2
        freqs = ROPE_THETA ** (-jnp.arange(half, dtype=F32) / half)
        ang = pos.astype(F32)[:, None] * freqs[None, :]
        cos, sin = jnp.cos(ang), jnp.sin(ang)
        reps = LANES // dim
        cos_t = jnp.tile(jnp.concatenate([cos, cos], axis=1), (1, reps))
        sin_t = jnp.tile(jnp.concatenate([-sin, sin], axis=1), (1, reps))
        return cos_t, sin_t
    return tables(HEAD_DIM) + tables(IDX_DIM)


def _rope_head(x, cos_t, sin_t):
    return x * cos_t + pltpu.roll(x, HEAD_DIM // 2, axis=1) * sin_t


def _rope_idx(x, cos_t, sin_t, first_half):
    half = IDX_DIM // 2
    partner = jnp.where(first_half, pltpu.roll(x, LANES - half, axis=1), pltpu.roll(x, half, axis=1))
    return x * cos_t + partner * sin_t


def _head_norm(x, g):
    ms = jnp.mean(x * x, axis=-1, keepdims=True)
    return (x * lax.rsqrt(ms + EPS)) * g


def _qk_post_body(q_ref, k_ref, v_ref, qi_ref, kiw_ref, ch_ref, sh_ref, ci_ref, si_ref, qn_ref, kn_ref,
                  qo_ref, ko_ref, kb_ref, vo_ref, vb_ref, qio_ref, kio_ref, kilo_ref, kihi_ref):
    ch, sh, ci, si = ch_ref[...], sh_ref[...], ci_ref[...], si_ref[...]
    lane = lax.broadcasted_iota(jnp.int32, ch.shape, 1)
    first_half = (lane & (IDX_DIM - 1)) < (IDX_DIM // 2)
    for h in range(N_HEADS):
        sl = slice(h * HEAD_DIM, (h + 1) * HEAD_DIM)
        qo_ref[:, sl] = _rope_head(_head_norm(q_ref[:, sl], qn_ref[...]), ch, sh).astype(qo_ref.dtype)
    for h in range(N_KV_HEADS):
        sl = slice(h * HEAD_DIM, (h + 1) * HEAD_DIM)
        kr = _rope_head(_head_norm(k_ref[:, sl], kn_ref[...]), ch, sh)
        ko_ref[:, sl] = kr
        kb_ref[:, sl] = kr.astype(kb_ref.dtype)
    v = v_ref[...]
    vo_ref[...] = v
    vb_ref[...] = v.astype(vb_ref.dtype)
    for h in range(IDX_W // LANES):
        sl = slice(h * LANES, (h + 1) * LANES)
        qio_ref[:, sl] = _rope_idx(qi_ref[:, sl], ci, si, first_half).astype(qio_ref.dtype)
    kir = _rope_idx(kiw_ref[...], ci, si, first_half)
    kio_ref[...] = kir[:, :IDX_DIM]
    zero = jnp.zeros_like(kir)
    kilo_ref[...] = jnp.where(lane < IDX_DIM, kir, zero).astype(kilo_ref.dtype)
    kihi_ref[...] = jnp.where(lane < IDX_DIM, zero, pltpu.roll(kir, IDX_DIM, axis=1)).astype(kihi_ref.dtype)


def _qk_post(proj, ki_col, tabs, tab_rows, q_norm, k_norm, act_dtype):
    m = proj.shape[0]
    tm = _pick(min(m, tab_rows), (512, 256, 128, 8))
    n_tab = tab_rows // tm
    col = lambda off, w: (lambda i: (i, off // w))
    tab_spec = pl.BlockSpec((tm, LANES), lambda i: (i % n_tab, 0))
    row = lambda w: pl.BlockSpec((tm, w), lambda i: (i, 0))
    ki_blk = ki_col // LANES
    outs = pl.pallas_call(
        _qk_post_body,
        grid=(m // tm,),
        in_specs=[
            pl.BlockSpec((tm, ATT_W), col(COL_Q, ATT_W)),
            pl.BlockSpec((tm, KV_W), col(COL_K, KV_W)),
            pl.BlockSpec((tm, KV_W), col(COL_V, KV_W)),
            pl.BlockSpec((tm, IDX_W), col(COL_QI, IDX_W)),
            pl.BlockSpec((tm, LANES), lambda i: (i, ki_blk)),
            tab_spec, tab_spec, tab_spec, tab_spec,
            pl.BlockSpec((1, HEAD_DIM), lambda i: (0, 0)),
            pl.BlockSpec((1, HEAD_DIM), lambda i: (0, 0)),
        ],
        out_specs=[row(ATT_W), row(KV_W), row(KV_W), row(KV_W), row(KV_W), row(IDX_W), row(IDX_DIM),
                   row(LANES), row(LANES)],
        out_shape=[
            jax.ShapeDtypeStruct((m, ATT_W), act_dtype),
            jax.ShapeDtypeStruct((m, KV_W), F32),
            jax.ShapeDtypeStruct((m, KV_W), act_dtype),
            jax.ShapeDtypeStruct((m, KV_W), F32),
            jax.ShapeDtypeStruct((m, KV_W), act_dtype),
            jax.ShapeDtypeStruct((m, IDX_W), act_dtype),
            jax.ShapeDtypeStruct((m, IDX_DIM), F32),
            jax.ShapeDtypeStruct((m, LANES), act_dtype),
            jax.ShapeDtypeStruct((m, LANES), act_dtype),
        ],
        compiler_params=_params(),
        name="qk_post",
    )(proj, proj, proj, proj, proj, *tabs, q_norm.reshape(1, HEAD_DIM), k_norm.reshape(1, HEAD_DIM))
    return outs


def _kth_largest_key(key, k):
    rows = key.shape[0]

    def body(it, res):
        trial = res + jnp.left_shift(jnp.int32(1), 31 - it)
        cnt = jnp.sum(jnp.where(key >= trial, 1.0, 0.0), axis=1, keepdims=True)
        return jnp.where(cnt >= float(k), trial, res)

    return lax.fori_loop(0, 32, body, jnp.full((rows, 1), INT_MIN, jnp.int32))


def _strict_upper(n):
    r = lax.broadcasted_iota(jnp.int32, (n, n), 0)
    c = lax.broadcasted_iota(jnp.int32, (n, n), 1)
    return jnp.where(r < c, 1.0, 0.0).astype(BF16)


def _attn_path(i, q_ref, qi_ref, kiw_ref, k_ref, v_ref, kilo_ref, kihi_ref, o_ref, bias_scr, *, topk, n_keys):
    tq = q_ref.shape[0]
    ki_lo, ki_hi = kilo_ref[:n_keys, :], kihi_ref[:n_keys, :]
    wi = kiw_ref[:, IDX_DIM:IDX_DIM + N_IDX_HEADS]
    score = jnp.zeros((tq, n_keys), F32)
    for pair in range(N_IDX_HEADS // 2):
        x = qi_ref[:, pair * LANES:(pair + 1) * LANES]
        score = score + wi[:, 2 * pair:2 * pair + 1] * jnp.maximum(_nt_dot(x, ki_lo), 0.0)
        score = score + wi[:, 2 * pair + 1:2 * pair + 2] * jnp.maximum(_nt_dot(x, ki_hi), 0.0)
    score = score * IDX_SCALE
    row = i * tq + lax.broadcasted_iota(jnp.int32, (tq, n_keys), 0)
    colk = lax.broadcasted_iota(jnp.int32, (tq, n_keys), 1)
    key = jnp.where(colk <= row, _sort_key(score), INT_MIN)
    thr = _kth_largest_key(key, topk)
    valid = key > KEY_NEG_INF
    bias_scr[:, :n_keys] = jnp.where((key >= thr) & valid, 0.0, MASKED)

    n_ge = jnp.sum(jnp.where(key >= thr, 1.0, 0.0), axis=1, keepdims=True)

    @pl.when(jnp.max(n_ge) > float(topk))
    def _():
        n_gt = jnp.sum(jnp.where(key > thr, 1.0, 0.0), axis=1, keepdims=True)
        need = float(topk) - n_gt
        upper = _strict_upper(LANES * 2)
        before = jnp.zeros((tq, 1), F32)
        for c in range(n_keys // (LANES * 2)):
            sl = slice(c * LANES * 2, (c + 1) * LANES * 2)
            eq = key[:, sl] == thr
            eqf = jnp.where(eq, 1.0, 0.0)
            rank = before + _dot(eqf.astype(BF16), upper)
            keep = ((key[:, sl] > thr) | (eq & (rank < need))) & valid[:, sl]
            bias_scr[:, sl] = jnp.where(keep, 0.0, MASKED)
            before = before + jnp.sum(eqf, axis=1, keepdims=True)

    bias = bias_scr[:, :n_keys]
    for g in range(N_KV_HEADS):
        kg = k_ref[:n_keys, g * HEAD_DIM:(g + 1) * HEAD_DIM]
        vg = v_ref[:n_keys, g * HEAD_DIM:(g + 1) * HEAD_DIM]
        for r in range(GQA_GROUP):
            h = g * GQA_GROUP + r
            sl = slice(h * HEAD_DIM, (h + 1) * HEAD_DIM)
            lg = _nt_dot(q_ref[:, sl], kg) * ATT_SCALE + bias
            mx = jnp.max(lg, axis=1, keepdims=True)
            p = jnp.exp(lg - mx)
            den = jnp.sum(p, axis=1, keepdims=True)
            o_ref[:, sl] = (_dot(p.astype(BF16), vg) / den).astype(o_ref.dtype)


def _attn_prompt_body(*refs, topk, n_paths):
    i = pl.program_id(1)
    tq = refs[0].shape[0]
    s = refs[3].shape[0]
    span = (s // tq) // n_paths
    for p in range(n_paths):
        @pl.when((i >= p * span) & (i < (p + 1) * span))
        def _(p=p):
            _attn_path(i, *refs, topk=topk, n_keys=(p + 1) * span * tq)


def _attn_prompt(q, qi, proj, ki_col, k, v, ki_lo, ki_hi, nb, s):
    tq = _pick(s, (256, 128))
    nq = s // tq
    topk = min(TOPK_MAX, s // 4)
    ki_blk = ki_col // LANES
    qrow = lambda w: pl.BlockSpec((tq, w), lambda b, i: (b * nq + i, 0))
    full = lambda w: pl.BlockSpec((s, w), lambda b, i: (b, 0))
    return pl.pallas_call(
        functools.partial(_attn_prompt_body, topk=topk, n_paths=min(4, nq)),
        grid=(nb, nq),
        in_specs=[qrow(ATT_W), qrow(IDX_W), pl.BlockSpec((tq, LANES), lambda b, i: (b * nq + i, ki_blk)),
                  full(KV_W), full(KV_W), full(LANES), full(LANES)],
        out_specs=qrow(ATT_W),
        out_shape=jax.ShapeDtypeStruct((nb * s, ATT_W), BF16),
        scratch_shapes=[pltpu.VMEM((tq, s), F32)],
        compiler_params=_params(),
        name="attn_prompt",
    )(q, qi, proj, k, v, ki_lo, ki_hi)


def _s5_discretize_body(are_ref, aim_ref, ldt_ref, bre_ref, bim_ref, abre_ref, abim_ref, bbre_ref, bbim_ref):
    dt = jnp.exp(ldt_ref[...])
    lr, li = are_ref[...], aim_ref[...]
    mag = jnp.exp(lr * dt)
    ab_re, ab_im = mag * jnp.cos(li * dt), mag * jnp.sin(li * dt)
    zr, zi = ab_re - 1.0, ab_im
    den = lr * lr + li * li
    fr = (zr * lr + zi * li) / den
    fi = (zi * lr - zr * li) / den
    abre_ref[...] = ab_re
    abim_ref[...] = ab_im
    br, bi = bre_ref[...], bim_ref[...]
    bbre_ref[...] = fr * br - fi * bi
    bbim_ref[...] = fr * bi + fi * br


def _s5_discretize(a_re, a_im, log_dt, b_re, b_im):
    depth = a_re.shape[0]
    gp = pl.BlockSpec((None, SSM_GROUPS, 1, SSM_P), lambda l: (l, 0, 0, 0))
    gwp = pl.BlockSpec((None, SSM_GROUPS, SSM_GROUP, SSM_P), lambda l: (l, 0, 0, 0))
    bt = lambda b: jnp.swapaxes(b, 2, 3)
    a4 = lambda a: a.reshape(depth, SSM_GROUPS, 1, SSM_P)
    return pl.pallas_call(
        _s5_discretize_body,
        grid=(depth,),
        in_specs=[gp, gp, pl.BlockSpec((None, SSM_GROUPS, 1, 1), lambda l: (l, 0, 0, 0)), gwp, gwp],
        out_specs=[gp, gp, gwp, gwp],
        out_shape=[jax.ShapeDtypeStruct((depth, SSM_GROUPS, 1, SSM_P), F32)] * 2
        + [jax.ShapeDtypeStruct((depth, SSM_GROUPS, SSM_GROUP, SSM_P), F32)] * 2,
        compiler_params=_params(),
        name="s5_discretize",
    )(a4(a_re), a4(a_im), log_dt.reshape(depth, SSM_GROUPS, 1, 1), bt(b_re), bt(b_im))


def _block_diag_in(bb_t):
    depth = bb_t.shape[0]
    gpb = SSM_GROUPS // SSM_BLOCKS
    x = bb_t.reshape(depth, SSM_BLOCKS, gpb, SSM_GROUP, SSM_P)
    eye = jnp.eye(gpb, dtype=bb_t.dtype)
    out = jnp.einsum('lkgwp,gh->lkgwhp', x, eye)
    return out.reshape(depth, SSM_BLOCKS, gpb * SSM_GROUP, gpb * SSM_P).astype(BF16)


def _block_diag_out(c):
    depth = c.shape[0]
    gpb = SSM_GROUPS // SSM_BLOCKS
    x = c.reshape(depth, SSM_BLOCKS, gpb, SSM_GROUP, SSM_P)
    eye = jnp.eye(gpb, dtype=c.dtype)
    out = jnp.einsum('lkgwp,gh->lkgphw', x, eye)
    return out.reshape(depth, SSM_BLOCKS, gpb * SSM_P, gpb * SSM_GROUP).astype(BF16)


def _s5_body(u_ref, bbre_ref, bbim_ref, cre_ref, cim_ref, abre_ref, abim_ref, d_ref, h0re_ref, h0im_ref,
             g_ref, htre_ref, htim_ref, bure, buim, hre, him, ust, gst, *, nb):
    c = pl.program_id(0)
    rows = bure.shape[0]
    n_t = rows // nb
    per_tile = SUBLANES // nb

    @pl.when(c == 0)
    def _():
        hre[...] = h0re_ref[...]
        him[...] = h0im_ref[...]

    for k in range(SSM_BLOCKS):
        ln = slice(k * LANES, (k + 1) * LANES)
        for b in range(nb):
            if n_t == 1:
                ust[k, b:b + 1, :] = u_ref[b, :, ln]
            else:
                ust[k, pl.ds(b, n_t, stride=nb), :] = u_ref[b, :, ln]
        uk = ust[k].astype(BF16)
        st = slice(k * SSM_BLOCK_STATE, (k + 1) * SSM_BLOCK_STATE)
        bure[:, st] = _dot(uk, bbre_ref[k])
        buim[:, st] = _dot(uk, bbim_ref[k])

    for cc in range(SSM_STATE // SCAN_LANES):
        sl = slice(cc * SCAN_LANES, (cc + 1) * SCAN_LANES)
        ar = jnp.broadcast_to(abre_ref[:, sl], (nb, SCAN_LANES))
        ai = jnp.broadcast_to(abim_ref[:, sl], (nb, SCAN_LANES))

        def step(j, carry, sl=sl, ar=ar, ai=ai):
            hr, hi = carry
            r0 = pl.multiple_of(j * SUBLANES, SUBLANES)
            xr = bure[pl.ds(r0, SUBLANES), sl]
            xi = buim[pl.ds(r0, SUBLANES), sl]
            out_r, out_i = [], []
            for t in range(per_tile):
                nr = ar * hr - ai * hi + xr[t * nb:(t + 1) * nb]
                ni = ar * hi + ai * hr + xi[t * nb:(t + 1) * nb]
                hr, hi = nr, ni
                out_r.append(hr)
                out_i.append(hi)
            bure[pl.ds(r0, SUBLANES), sl] = out_r[0] if per_tile == 1 else jnp.concatenate(out_r, axis=0)
            buim[pl.ds(r0, SUBLANES), sl] = out_i[0] if per_tile == 1 else jnp.concatenate(out_i, axis=0)
            return hr, hi

        hr, hi = lax.fori_loop(0, rows // SUBLANES, step, (hre[:, sl], him[:, sl]))
        hre[:, sl] = hr
        him[:, sl] = hi

    for k in range(SSM_BLOCKS):
        st = slice(k * SSM_BLOCK_STATE, (k + 1) * SSM_BLOCK_STATE)
        ln = slice(k * LANES, (k + 1) * LANES)
        y = _dot(bure[:, st].astype(BF16), cre_ref[k]) - _dot(buim[:, st].astype(BF16), cim_ref[k])
        y = y + d_ref[:, ln] * ust[k]
        gst[k] = _gelu_tanh(y)
        for b in range(nb):
            if n_t == 1:
                g_ref[b, :, ln] = gst[k, b:b + 1, :]
            else:
                g_ref[b, :, ln] = gst[k, pl.ds(b, n_t, stride=nb), :]

    @pl.when(c == pl.num_programs(0) - 1)
    def _():
        htre_ref[...] = hre[...]
        htim_ref[...] = him[...]


def _s5(proj, nb, s, l, bb_re, bb_im, c_re, c_im, ab_re, ab_im, d, h0_re, h0_im):
    n_t = min(s, 512 // nb)
    rows = n_t * nb
    blk4 = lambda a: pl.BlockSpec((None,) + a.shape[1:], lambda c: (l, 0, 0, 0))
    full2 = lambda a: pl.BlockSpec(a.shape, lambda c: (0, 0))
    vec = lambda a: a.reshape(1, -1)
    ab_re, ab_im, d = vec(ab_re), vec(ab_im), vec(d)
    u_spec = pl.BlockSpec((nb, n_t, SSM_W), lambda c: (0, c, 0))
    g, h_re, h_im = pl.pallas_call(
        functools.partial(_s5_body, nb=nb),
        grid=(s // n_t,),
        in_specs=[u_spec, blk4(bb_re), blk4(bb_im), blk4(c_re), blk4(c_im),
                  full2(ab_re), full2(ab_im), full2(d), full2(h0_re), full2(h0_im)],
        out_specs=[u_spec, full2(h0_re), full2(h0_im)],
        out_shape=[jax.ShapeDtypeStruct((nb, s, SSM_W), F32),
                   jax.ShapeDtypeStruct(h0_re.shape, F32), jax.ShapeDtypeStruct(h0_im.shape, F32)],
        scratch_shapes=[pltpu.VMEM((rows, SSM_STATE), F32), pltpu.VMEM((rows, SSM_STATE), F32),
                        pltpu.VMEM((nb, SSM_STATE), F32), pltpu.VMEM((nb, SSM_STATE), F32),
                        pltpu.VMEM((SSM_BLOCKS, rows, LANES), F32), pltpu.VMEM((SSM_BLOCKS, rows, LANES), F32)],
        compiler_params=_params(),
        name="s5_scan",
    )(proj.reshape(nb, s, proj.shape[1]), bb_re, bb_im, c_re, c_im, ab_re, ab_im, d, h0_re, h0_im)
    return g.reshape(nb * s, SSM_W), h_re, h_im


IDX_PAGES_PER_STEP = 16
ATT_PAGES_PER_STEP = 8


def _sample_index_body(pt_ref, qi_ref, wi_ref, kin_ref, *rest, n_pg, topk):
    del pt_ref
    pages = rest[:n_pg]
    bias_ref, bnew_ref, sc_scr = rest[n_pg:]
    c = pl.program_id(1)
    n_pages = sc_scr.shape[0]
    qi = qi_ref[...].astype(BF16)
    wi = wi_ref[...]

    def weighted(d):
        return jnp.sum(wi * jnp.maximum(d, 0.0), axis=0, keepdims=True) * IDX_SCALE

    for p in range(n_pg):
        sc_scr[pl.ds(c * n_pg + p, 1), :] = weighted(_dot(qi, pages[p][...].astype(BF16)))

    @pl.when(c == pl.num_programs(1) - 1)
    def _():
        kin = kin_ref[...].astype(BF16).astype(F32)
        d_new = jnp.sum(qi.astype(F32) * kin, axis=1, keepdims=True)
        key_new = _sort_key(jnp.broadcast_to(weighted(d_new), (1, LANES)))
        key = _sort_key(sc_scr[...])

        def count(mask, mask_new):
            per_lane = jnp.sum(jnp.where(mask, 1.0, 0.0), axis=0, keepdims=True)
            return jnp.sum(per_lane, axis=1, keepdims=True) + jnp.where(mask_new, 1.0, 0.0)

        def body(it, res):
            trial = res + jnp.left_shift(jnp.int32(1), 31 - it)
            return jnp.where(count(key >= trial, key_new >= trial) >= float(topk), trial, res)

        thr = lax.fori_loop(0, 32, body, jnp.full((1, LANES), INT_MIN, jnp.int32))
        need = float(topk) - count(key > thr, key_new > thr)
        eq = key == thr
        eqf = jnp.where(eq, 1.0, 0.0).astype(BF16)
        in_page = _dot(eqf, _strict_upper(LANES))
        r = lax.broadcasted_iota(jnp.int32, (n_pages, n_pages), 0)
        cidx = lax.broadcasted_iota(jnp.int32, (n_pages, n_pages), 1)
        lower = jnp.where(cidx < r, 1.0, 0.0).astype(BF16)
        per_page = jnp.sum(jnp.where(eq, 1.0, 0.0), axis=1, keepdims=True)
        earlier = _dot(lower, jnp.broadcast_to(per_page, (n_pages, LANES)).astype(BF16))
        keep = ((key > thr) | (eq & (in_page + earlier < need))) & (key > KEY_NEG_INF)
        bias_ref[...] = jnp.where(keep, 0.0, MASKED)
        n_eq_past = jnp.sum(jnp.sum(jnp.where(eq, 1.0, 0.0), axis=0, keepdims=True), axis=1, keepdims=True)
        keep_new = ((key_new > thr) | ((key_new == thr) & (n_eq_past < need))) & (key_new > KEY_NEG_INF)
        bnew_ref[...] = jnp.where(keep_new, 0.0, MASKED)


def _sample_index(page_table, cache_kidx_t, layer, qi, wi, ki_new):
    nb, n_pages = page_table.shape
    n_pg = _pick(n_pages, (IDX_PAGES_PER_STEP, 8, 4, 2, 1))
    topk = min(TOPK_MAX, (n_pages * PAGE_SIZE + 1) // 4)
    page_spec = lambda p: pl.BlockSpec((None, None, IDX_DIM, PAGE_SIZE),
                                       lambda b, c, pt: (layer, pt[b, c * n_pg + p], 0, 0))
    per_b = lambda shape: pl.BlockSpec((None,) + shape, lambda b, c, pt: (b, 0, 0))
    grid_spec = pltpu.PrefetchScalarGridSpec(
        num_scalar_prefetch=1,
        grid=(nb, n_pages // n_pg),
        in_specs=[per_b((N_IDX_HEADS, IDX_DIM)), per_b((N_IDX_HEADS, 1)), per_b((1, IDX_DIM))]
        + [page_spec(p) for p in range(n_pg)],
        out_specs=[per_b((n_pages, LANES)), per_b((1, LANES))],
        scratch_shapes=[pltpu.VMEM((n_pages, LANES), F32)],
    )
    return pl.pallas_call(
        functools.partial(_sample_index_body, n_pg=n_pg, topk=topk),
        grid_spec=grid_spec,
        out_shape=[jax.ShapeDtypeStruct((nb, n_pages, LANES), F32), jax.ShapeDtypeStruct((nb, 1, LANES), F32)],
        compiler_params=_params(),
        name="sample_index",
    )(page_table, qi.reshape(nb, N_IDX_HEADS, IDX_DIM), wi.reshape(nb, N_IDX_HEADS, 1),
      ki_new.reshape(nb, 1, IDX_DIM), *([cache_kidx_t] * n_pg))


def _sample_attn_body(pt_ref, bias_ref, bnew_ref, q_ref, kn_ref, vn_ref, *rest, n_pg):
    del pt_ref
    kpages = rest[:n_pg]
    vpages = rest[n_pg:2 * n_pg]
    o_ref, m_scr, l_scr, acc_scr = rest[2 * n_pg:]
    c = pl.program_id(1)

    @pl.when(c == 0)
    def _():
        m_scr[...] = jnp.full(m_scr.shape, MASKED, F32)
        l_scr[...] = jnp.zeros(l_scr.shape, F32)
        acc_scr[...] = jnp.zeros(acc_scr.shape, F32)

    q = q_ref[...].astype(BF16)
    head = lax.broadcasted_iota(jnp.int32, (N_HEADS, LANES), 0)
    in_group = [(head >= g * GQA_GROUP) & (head < (g + 1) * GQA_GROUP) for g in range(N_KV_HEADS)]
    for p in range(n_pg):
        bias = bias_ref[p:p + 1, :]
        sel = bias == 0.0
        lg = jnp.zeros((N_HEADS, PAGE_SIZE), F32)
        for g in range(N_KV_HEADS):
            kg = kpages[p][pl.ds(g, PAGE_SIZE, stride=N_KV_HEADS), :].astype(BF16)
            lg = jnp.where(in_group[g], _nt_dot(q, kg), lg)
        lg = lg * ATT_SCALE + bias
        m_old = m_scr[...]
        m_new = jnp.maximum(m_old, jnp.max(lg, axis=1, keepdims=True))
        alpha = jnp.exp(m_old - m_new)
        pr = jnp.where(sel, jnp.exp(lg - m_new), 0.0)
        l_scr[...] = alpha * l_scr[...] + jnp.sum(pr, axis=1, keepdims=True)
        prb = pr.astype(BF16)
        pv = jnp.zeros((N_HEADS, HEAD_DIM), F32)
        for g in range(N_KV_HEADS):
            vg = vpages[p][pl.ds(g, PAGE_SIZE, stride=N_KV_HEADS), :].astype(BF16)
            pv = jnp.where(in_group[g], _dot(prb, vg), pv)
        acc_scr[...] = alpha * acc_scr[...] + pv
        m_scr[...] = m_new

    @pl.when(c == pl.num_programs(1) - 1)
    def _():
        bias_new = bnew_ref[:, :1]
        sel_new = bias_new == 0.0
        qf = q.astype(F32)
        kn = kn_ref[...].astype(BF16).astype(F32)
        vn = vn_ref[...].astype(BF16).astype(F32)
        lg = jnp.sum(qf * kn, axis=1, keepdims=True) * ATT_SCALE + bias_new
        m_old = m_scr[...]
        m_new = jnp.maximum(m_old, lg)
        alpha = jnp.exp(m_old - m_new)
        pr = jnp.where(sel_new, jnp.exp(lg - m_new), 0.0)
        den = alpha * l_scr[...] + pr
        prq = pr.astype(BF16).astype(F32)
        o_ref[...] = ((alpha * acc_scr[...] + prq * vn) / den).astype(o_ref.dtype)


def _sample_attn(page_table, cache_k2, cache_v2, layer, bias, bias_new, q, k_new, v_new):
    nb, n_pages = page_table.shape
    n_pg = _pick(n_pages, (ATT_PAGES_PER_STEP, 4, 2, 1))
    page_rows = PAGE_SIZE * N_KV_HEADS
    page_spec = lambda p: pl.BlockSpec((None, None, page_rows, HEAD_DIM),
                                       lambda b, c, pt: (layer, pt[b, c * n_pg + p], 0, 0))
    per_b = lambda shape: pl.BlockSpec((None,) + shape, lambda b, c, pt: (b, 0, 0))
    per_head = lambda a: jnp.repeat(a.reshape(nb, N_KV_HEADS, HEAD_DIM), GQA_GROUP, axis=1)
    grid_spec = pltpu.PrefetchScalarGridSpec(
        num_scalar_prefetch=1,
        grid=(nb, n_pages // n_pg),
        in_specs=[pl.BlockSpec((None, n_pg, LANES), lambda b, c, pt: (b, c, 0)), per_b((1, LANES)),
                  per_b((N_HEADS, HEAD_DIM)), per_b((N_HEADS, HEAD_DIM)), per_b((N_HEADS, HEAD_DIM))]
        + [page_spec(p) for p in range(n_pg)] * 2,
        out_specs=per_b((N_HEADS, HEAD_DIM)),
        scratch_shapes=[pltpu.VMEM((N_HEADS, 1), F32), pltpu.VMEM((N_HEADS, 1), F32),
                        pltpu.VMEM((N_HEADS, HEAD_DIM), F32)],
    )
    out = pl.pallas_call(
        functools.partial(_sample_attn_body, n_pg=n_pg),
        grid_spec=grid_spec,
        out_shape=jax.ShapeDtypeStruct((nb, N_HEADS, HEAD_DIM), F32),
        compiler_params=_params(),
        name="sample_attn",
    )(page_table, bias, bias_new, q.reshape(nb, N_HEADS, HEAD_DIM), per_head(k_new), per_head(v_new),
      *([cache_k2] * n_pg), *([cache_v2] * n_pg))
    return out.reshape(nb, ATT_W)


def _layer_tail(proj, g, y_att, x, w, l, act_dtype):
    z = _glu(g, w["w_glu"], l, act_dtype)
    merged = _merge(z, y_att, w["w_branch"], l, proj, act_dtype)
    x1, hn = _out_norm(merged, w["w_out"], l, x, w["norm_ffn"][l], act_dtype)
    act = _ffn_up(hn, w["w_ffn_up"], l, act_dtype)
    return _ffn_down(act, w["w_ffn_down"], l, x1)


def _s5_layer(proj, nb, s, w, l, h0_re, h0_im):
    return _s5(proj, nb, s, l, w["bb_re"], w["bb_im"], w["c_re"], w["c_im"], w["ab_re"][l], w["ab_im"][l],
               w["ssm_d"][l], h0_re, h0_im)


def _prompt_layer(x, nb, s, tabs, w, l):
    xn = _rmsnorm(x, w["norm_mix"][l], BF16)
    proj = _in_proj(xn, w["w_in"], l, PROJ_TN, F32)
    q, k_out, k_bf, v_out, v_bf, qi, ki_out, ki_lo, ki_hi = _qk_post(
        proj, w["ki_col"], tabs, s, w["q_norm"][l], w["k_norm"][l], BF16)
    y_att = _attn_prompt(q, qi, proj, w["ki_col"], k_bf, v_bf, ki_lo, ki_hi, nb, s)
    zeros = jnp.zeros((nb, SSM_STATE), F32)
    g, h_re, h_im = _s5_layer(proj, nb, s, w, l, zeros, zeros)
    x2 = _layer_tail(proj, g, y_att, x, w, l, BF16)
    return x2, k_out, v_out, ki_out, h_re, h_im


def _sample_layer(x, nb, tabs, w, l, page_table, cache_k2, cache_v2, cache_kidx_t, h0_re, h0_im):
    xn = _rmsnorm(x, w["norm_mix"][l], F32)
    proj = _in_proj(xn, w["w_in"], l, PROJ_TN, F32)
    ki_col = w["ki_col"]
    q, k_out, _, v_out, _, qi, ki_out, _, _ = _qk_post(proj, ki_col, tabs, nb, w["q_norm"][l], w["k_norm"][l], F32)
    wi = proj[:, ki_col + IDX_DIM:ki_col + IDX_DIM + N_IDX_HEADS]
    bias, bias_new = _sample_index(page_table, cache_kidx_t, l, qi, wi, ki_out)
    y_att = _sample_attn(page_table, cache_k2, cache_v2, l, bias, bias_new, q, k_out, v_out)
    g, h_re, h_im = _s5_layer(proj, nb, 1, w, l, h0_re, h0_im)
    x2 = _layer_tail(proj, g, y_att, x, w, l, F32)
    return x2, k_out, v_out, ki_out, h_re, h_im


def kernel(x_prompt, x_sample, cache_k, cache_v, cache_kidx, state_ssm_re, state_ssm_im, page_table, norm_mix, w_in, q_norm, k_norm, ssm_a_re, ssm_a_im, ssm_log_dt, ssm_b_re, ssm_b_im, ssm_c_re, ssm_c_im, ssm_d, w_glu, w_branch, w_out, norm_ffn, w_ffn_up, w_ffn_down):
    nb_p, s, d_model = x_prompt.shape
    nb_s, dec_seq, _ = x_sample.shape
    assert dec_seq == 1, "the sample group decodes one token per sequence"
    depth = w_in.shape[0]
    past = page_table.shape[1] * PAGE_SIZE

    ki_orig = COL_GATE
    gate_orig = ki_orig + IDX_DIM + N_IDX_HEADS
    n_gate = w_in.shape[2] - gate_orig
    ki_col = COL_GATE + n_gate
    assert ki_col % LANES == 0
    cols = ki_col + LANES
    cols_pad = -(-cols // PROJ_TN) * PROJ_TN
    w_in_r = jnp.concatenate(
        [w_in[:, :, :COL_GATE], w_in[:, :, gate_orig:], w_in[:, :, ki_orig:gate_orig],
         jnp.zeros((depth, d_model, cols_pad - ki_col - IDX_DIM - N_IDX_HEADS), w_in.dtype)], axis=2).astype(BF16)

    ab_re, ab_im, bbt_re, bbt_im = _s5_discretize(ssm_a_re, ssm_a_im, ssm_log_dt, ssm_b_re, ssm_b_im)
    bb_re, bb_im = _block_diag_in(bbt_re), _block_diag_in(bbt_im)
    c_re, c_im = _block_diag_out(ssm_c_re), _block_diag_out(ssm_c_im)
    w_glu_b, w_branch_b, w_out_b = w_glu.astype(BF16), w_branch.astype(BF16), w_out.astype(BF16)
    w_up_b, w_down_b = w_ffn_up.astype(BF16), w_ffn_down.astype(BF16)

    tabs_p = _rope_tables(jnp.arange(s, dtype=jnp.int32))
    tabs_s = _rope_tables(jnp.full((nb_s,), past, dtype=jnp.int32))

    xp = x_prompt.reshape(nb_p * s, d_model)
    xs = x_sample.reshape(nb_s, d_model)
    cache_k2 = cache_k.reshape(cache_k.shape[:2] + (PAGE_SIZE * N_KV_HEADS, HEAD_DIM))
    cache_v2 = cache_v.reshape(cache_v.shape[:2] + (PAGE_SIZE * N_KV_HEADS, HEAD_DIM))
    cache_kidx_t = jnp.swapaxes(cache_kidx, 2, 3)
    w = dict(ki_col=ki_col, norm_mix=norm_mix, w_in=w_in_r, q_norm=q_norm, k_norm=k_norm,
             bb_re=bb_re, bb_im=bb_im, c_re=c_re, c_im=c_im, ab_re=ab_re, ab_im=ab_im, ssm_d=ssm_d,
             w_glu=w_glu_b, w_branch=w_branch_b, w_out=w_out_b, norm_ffn=norm_ffn,
             w_ffn_up=w_up_b, w_ffn_down=w_down_b)
    outs_p, outs_s = [], []
    for l in range(depth):
        xp, *rest_p = _prompt_layer(xp, nb_p, s, tabs_p, w, l)
        xs, *rest_s = _sample_layer(xs, nb_s, tabs_s, w, l, page_table, cache_k2, cache_v2, cache_kidx_t,
                                    state_ssm_re[l].reshape(nb_s, SSM_STATE), state_ssm_im[l].reshape(nb_s, SSM_STATE))
        outs_p.append(rest_p)
        outs_s.append(rest_s)

    def stack(outs, idx, shape):
        return jnp.stack([o[idx].reshape(shape) for o in outs])

    kv_p, kv_s = (nb_p, s, N_KV_HEADS, HEAD_DIM), (nb_s, dec_seq, N_KV_HEADS, HEAD_DIM)
    st_p, st_s = (nb_p, SSM_GROUPS, SSM_P), (nb_s, SSM_GROUPS, SSM_P)
    return (xp.reshape(nb_p, s, d_model), xs.reshape(nb_s, dec_seq, d_model),
            stack(outs_p, 0, kv_p), stack(outs_p, 1, kv_p), stack(outs_p, 2, (nb_p, s, IDX_DIM)),
            stack(outs_p, 3, st_p), stack(outs_p, 4, st_p),
            stack(outs_s, 0, kv_s), stack(outs_s, 1, kv_s), stack(outs_s, 2, (nb_s, dec_seq, IDX_DIM)),
            stack(outs_s, 3, st_s), stack(outs_s, 4, st_s))
```

```python
import functools
import math

import jax
import jax.numpy as jnp
from jax import lax
from jax.experimental import pallas as pl
from jax.experimental.pallas import tpu as pltpu

F32 = jnp.float32
BF16 = jnp.bfloat16

SSM_W = 1024
SSM_GROUP = 16
SSM_GROUPS = SSM_W // SSM_GROUP
SSM_P = 64
SSM_STATE = SSM_GROUPS * SSM_P
N_HEADS = 8
N_KV_HEADS = 4
HEAD_DIM = 128
GQA_GROUP = N_HEADS // N_KV_HEADS
ATT_W = N_HEADS * HEAD_DIM
KV_W = N_KV_HEADS * HEAD_DIM
N_IDX_HEADS = 16
IDX_DIM = 64
IDX_W = N_IDX_HEADS * IDX_DIM
IDX_SCALE = (IDX_DIM * N_IDX_HEADS) ** -0.5
ATT_SCALE = HEAD_DIM ** -0.5
TOPK_MAX = 256
PAGE_SIZE = 128
ROPE_THETA = 10000.0
EPS = 1e-6

LANES = 128
SUBLANES = 8
VMEM_LIMIT = 56 * 1024 * 1024

COL_U = 0
COL_Q = SSM_W
COL_K = COL_Q + ATT_W
COL_V = COL_K + KV_W
COL_QI = COL_V + KV_W
COL_GATE = COL_QI + IDX_W
PROJ_TN = 768

MASKED = -1e30
INT_MIN = -(2 ** 31)
KEY_NEG_INF = -2139095041

SCAN_LANES = 512
SSM_BLOCKS = SSM_W // LANES
SSM_BLOCK_STATE = SSM_STATE // SSM_BLOCKS


def _params(**kw):
    return pltpu.CompilerParams(vmem_limit_bytes=VMEM_LIMIT, **kw)


def _pick(n, cands):
    for c in cands:
        if n % c == 0:
            return c
    return n


def _nt_dot(a, b):
    return lax.dot_general(a, b, (((1,), (1,)), ((), ())), preferred_element_type=F32)


def _dot(a, b):
    return jnp.dot(a, b, preferred_element_type=F32)


def _sigmoid(x):
    return 1.0 / (1.0 + jnp.exp(-x))


def _gelu_tanh(x):
    c = math.sqrt(2.0 / math.pi)
    return 0.5 * x * (1.0 + jnp.tanh(c * (x + 0.044715 * (x * x * x))))


def _sort_key(s):
    bits = pltpu.bitcast(s, jnp.int32)
    return bits ^ ((bits >> 31) & jnp.int32(0x7FFFFFFF))


def _rmsnorm_body(x_ref, g_ref, o_ref):
    x = x_ref[...].astype(F32)
    ms = jnp.mean(x * x, axis=-1, keepdims=True)
    o_ref[...] = ((x * lax.rsqrt(ms + EPS)) * g_ref[...]).astype(o_ref.dtype)


def _rmsnorm(x, g, out_dtype):
    m, d = x.shape
    tm = _pick(m, (512, 256, 128, 8))
    return pl.pallas_call(
        _rmsnorm_body,
        grid=(m // tm,),
        in_specs=[pl.BlockSpec((tm, d), lambda i: (i, 0)), pl.BlockSpec((1, d), lambda i: (0, 0))],
        out_specs=pl.BlockSpec((tm, d), lambda i: (i, 0)),
        out_shape=jax.ShapeDtypeStruct((m, d), out_dtype),
        compiler_params=_params(),
        name="rmsnorm",
    )(x, g.reshape(1, d))


def _mm_body(a_ref, w_ref, o_ref):
    o_ref[...] = _dot(a_ref[...].astype(BF16), w_ref[...]).astype(o_ref.dtype)


def _in_proj(a, w, l, tn, out_dtype):
    m, k = a.shape
    n = w.shape[2]
    tm = _pick(m, (1024, 512, 256, 128, 8))
    return pl.pallas_call(
        _mm_body,
        grid=(n // tn, m // tm),
        in_specs=[pl.BlockSpec((tm, k), lambda j, i: (i, 0)), pl.BlockSpec((None, k, tn), lambda j, i: (l, 0, j))],
        out_specs=pl.BlockSpec((tm, tn), lambda j, i: (i, j)),
        out_shape=jax.ShapeDtypeStruct((m, n), out_dtype),
        compiler_params=_params(),
        name="in_proj",
    )(a, w)


def _cast_weight_once(step, w_ref, wb_scr):
    @pl.when(step == 0)
    def _():
        wb_scr[...] = w_ref[...].astype(BF16)


def _glu_body(g_ref, w_ref, o_ref, wb):
    _cast_weight_once(pl.program_id(0), w_ref, wb)
    g = g_ref[...]
    z = _dot(g.astype(BF16), wb[...])
    o_ref[...] = (g * _sigmoid(z)).astype(o_ref.dtype)


def _glu(g, w, l, out_dtype):
    m, n = g.shape
    tm = _pick(m, (512, 256, 128, 8))
    return pl.pallas_call(
        _glu_body,
        grid=(m // tm,),
        in_specs=[pl.BlockSpec((tm, n), lambda i: (i, 0)),
                  pl.BlockSpec((None, n, n), lambda i: (l, 0, 0), pipeline_mode=pl.Buffered(1))],
        out_specs=pl.BlockSpec((tm, n), lambda i: (i, 0)),
        out_shape=jax.ShapeDtypeStruct((m, n), out_dtype),
        scratch_shapes=[pltpu.VMEM((n, n), BF16)],
        compiler_params=_params(),
        name="glu",
    )(g, w)


def _merge_body(a0_ref, a1_ref, w0_ref, w1_ref, g0_ref, g1_ref, o_ref, wb0, wb1):
    _cast_weight_once(pl.program_id(1), w0_ref, wb0)
    _cast_weight_once(pl.program_id(1), w1_ref, wb1)
    y0 = _dot(a0_ref[...].astype(BF16), wb0[...])
    y1 = _dot(a1_ref[...].astype(BF16), wb1[...])
    o_ref[...] = (_sigmoid(g0_ref[...]) * y0 + _sigmoid(g1_ref[...]) * y1).astype(o_ref.dtype)


def _merge(a0, a1, w_branch, l, proj, out_dtype):
    m, k = a0.shape
    n = w_branch.shape[3]
    tm = _pick(m, (512, 256, 128, 8))
    tn = _pick(n, (1024, 512, 256, 128))
    g0_blk = COL_GATE // tn
    g1_blk = (COL_GATE + n) // tn
    wspec = lambda br: pl.BlockSpec((None, None, k, tn), lambda j, i: (l, br, 0, j))
    return pl.pallas_call(
        _merge_body,
        grid=(n // tn, m // tm),
        in_specs=[
            pl.BlockSpec((tm, k), lambda j, i: (i, 0)),
            pl.BlockSpec((tm, k), lambda j, i: (i, 0)),
            wspec(0), wspec(1),
            pl.BlockSpec((tm, tn), lambda j, i: (i, g0_blk + j)),
            pl.BlockSpec((tm, tn), lambda j, i: (i, g1_blk + j)),
        ],
        out_specs=pl.BlockSpec((tm, tn), lambda j, i: (i, j)),
        out_shape=jax.ShapeDtypeStruct((m, n), out_dtype),
        scratch_shapes=[pltpu.VMEM((k, tn), BF16), pltpu.VMEM((k, tn), BF16)],
        compiler_params=_params(),
        name="merge",
    )(a0, a1, w_branch, w_branch, proj, proj)


def _out_norm_body(a_ref, w_ref, x_ref, g_ref, x1_ref, hn_ref, wb):
    _cast_weight_once(pl.program_id(0), w_ref, wb)
    x1 = x_ref[...] + _dot(a_ref[...].astype(BF16), wb[...])
    x1_ref[...] = x1
    ms = jnp.mean(x1 * x1, axis=-1, keepdims=True)
    hn_ref[...] = ((x1 * lax.rsqrt(ms + EPS)) * g_ref[...]).astype(hn_ref.dtype)


def _out_norm(a, w, l, x, g, hn_dtype):
    m, k = a.shape
    n = w.shape[2]
    tm = _pick(m, (256, 128, 8))
    return pl.pallas_call(
        _out_norm_body,
        grid=(m // tm,),
        in_specs=[
            pl.BlockSpec((tm, k), lambda i: (i, 0)),
            pl.BlockSpec((None, k, n), lambda i: (l, 0, 0), pipeline_mode=pl.Buffered(1)),
            pl.BlockSpec((tm, n), lambda i: (i, 0)),
            pl.BlockSpec((1, n), lambda i: (0, 0)),
        ],
        out_specs=[pl.BlockSpec((tm, n), lambda i: (i, 0)), pl.BlockSpec((tm, n), lambda i: (i, 0))],
        out_shape=[jax.ShapeDtypeStruct((m, n), F32), jax.ShapeDtypeStruct((m, n), hn_dtype)],
        scratch_shapes=[pltpu.VMEM((k, n), BF16)],
        compiler_params=_params(),
        name="out_proj_norm",
    )(a, w, x, g.reshape(1, n))


def _ffn_up_body(a_ref, wg_ref, wu_ref, o_ref, wgb, wub):
    _cast_weight_once(pl.program_id(1), wg_ref, wgb)
    _cast_weight_once(pl.program_id(1), wu_ref, wub)
    a = a_ref[...].astype(BF16)
    gate = _dot(a, wgb[...])
    up = _dot(a, wub[...])
    o_ref[...] = ((gate * _sigmoid(gate)) * up).astype(o_ref.dtype)


def _ffn_up(a, w_up, l, out_dtype):
    m, k = a.shape
    d_ff = w_up.shape[2] // 2
    tm = _pick(m, (1024, 512, 256, 128, 8))
    tn = _pick(d_ff, (512, 256, 128))
    up_blk = d_ff // tn
    return pl.pallas_call(
        _ffn_up_body,
        grid=(d_ff // tn, m // tm),
        in_specs=[
            pl.BlockSpec((tm, k), lambda j, i: (i, 0)),
            pl.BlockSpec((None, k, tn), lambda j, i: (l, 0, j)),
            pl.BlockSpec((None, k, tn), lambda j, i: (l, 0, up_blk + j)),
        ],
        out_specs=pl.BlockSpec((tm, tn), lambda j, i: (i, j)),
        out_shape=jax.ShapeDtypeStruct((m, d_ff), out_dtype),
        scratch_shapes=[pltpu.VMEM((k, tn), BF16), pltpu.VMEM((k, tn), BF16)],
        compiler_params=_params(),
        name="ffn_up",
    )(a, w_up, w_up)


def _ffn_down_body(a_ref, w_ref, x_ref, o_ref, wb):
    _cast_weight_once(pl.program_id(1), w_ref, wb)
    o_ref[...] = x_ref[...] + _dot(a_ref[...].astype(BF16), wb[...])


def _ffn_down(a, w, l, x):
    m, k = a.shape
    n = w.shape[2]
    tm = _pick(m, (256, 128, 8))
    tn = _pick(n, (512, 256, 128))
    return pl.pallas_call(
        _ffn_down_body,
        grid=(n // tn, m // tm),
        in_specs=[
            pl.BlockSpec((tm, k), lambda j, i: (i, 0)),
            pl.BlockSpec((None, k, tn), lambda j, i: (l, 0, j)),
            pl.BlockSpec((tm, tn), lambda j, i: (i, j)),
        ],
        out_specs=pl.BlockSpec((tm, tn), lambda j, i: (i, j)),
        out_shape=jax.ShapeDtypeStruct((m, n), F32),
        scratch_shapes=[pltpu.VMEM((k, tn), BF16)],
        compiler_params=_params(),
        name="ffn_down",
    )(a, w, x)


def _rope_tables(pos):
    def tables(dim):
        half = dim // 2
        freqs = ROPE_THETA ** (-jnp.arange(half, dtype=F32) / half)
        ang = pos.astype(F32)[:, None] * freqs[None, :]
        cos, sin = jnp.cos(ang), jnp.sin(ang)
        reps = LANES // dim
        cos_t = jnp.tile(jnp.concatenate([cos, cos], axis=1), (1, reps))
        sin_t = jnp.tile(jnp.concatenate([-sin, sin], axis=1), (1, reps))
        return cos_t, sin_t
    return tables(HEAD_DIM) + tables(IDX_DIM)


def _rope_head(x, cos_t, sin_t):
    return x * cos_t + pltpu.roll(x, HEAD_DIM // 2, axis=1) * sin_t


def _rope_idx(x, cos_t, sin_t, first_half):
    half = IDX_DIM // 2
    partner = jnp.where(first_half, pltpu.roll(x, LANES - half, axis=1), pltpu.roll(x, half, axis=1))
    return x * cos_t + partner * sin_t


def _head_norm(x, g):
    ms = jnp.mean(x * x, axis=-1, keepdims=True)
    return (x * lax.rsqrt(ms + EPS)) * g


def _qk_post_body(q_ref, k_ref, v_ref, qi_ref, kiw_ref, ch_ref, sh_ref, ci_ref, si_ref, qn_ref, kn_ref,
                  qo_ref, ko_ref, kb_ref, vo_ref, vb_ref, qio_ref, kio_ref, kilo_ref, kihi_ref):
    ch, sh, ci, si = ch_ref[...], sh_ref[...], ci_ref[...], si_ref[...]
    tm = ch.shape[0]
    lane = lax.broadcasted_iota(jnp.int32, ch.shape, 1)
    first_half = (lane & (IDX_DIM - 1)) < (IDX_DIM // 2)
    for h in range(N_HEADS):
        sl = slice(h * HEAD_DIM, (h + 1) * HEAD_DIM)
        qo_ref[:, sl] = _rope_head(_head_norm(q_ref[:, sl], qn_ref[...]), ch, sh).astype(qo_ref.dtype)
    for h in range(N_KV_HEADS):
        sl = slice(h * HEAD_DIM, (h + 1) * HEAD_DIM)
        kr = _rope_head(_head_norm(k_ref[:, sl], kn_ref[...]), ch, sh)
        ko_ref[pl.ds(h, tm, stride=N_KV_HEADS), :] = kr
        vo_ref[pl.ds(h, tm, stride=N_KV_HEADS), :] = v_ref[:, sl]
        kb_ref[:, sl] = kr.astype(kb_ref.dtype)
    vb_ref[...] = v_ref[...].astype(vb_ref.dtype)
    for h in range(IDX_W // LANES):
        sl = slice(h * LANES, (h + 1) * LANES)
        qio_ref[:, sl] = _rope_idx(qi_ref[:, sl], ci, si, first_half).astype(qio_ref.dtype)
    kir = _rope_idx(kiw_ref[...], ci, si, first_half)
    kio_ref[...] = kir[:, :IDX_DIM]
    zero = jnp.zeros_like(kir)
    kilo_ref[...] = jnp.where(lane < IDX_DIM, kir, zero).astype(kilo_ref.dtype)
    kihi_ref[...] = jnp.where(lane < IDX_DIM, zero, pltpu.roll(kir, IDX_DIM, axis=1)).astype(kihi_ref.dtype)


def _qk_post(proj, ki_col, tabs, tab_rows, q_norm, k_norm, act_dtype):
    m = proj.shape[0]
    tm = _pick(min(m, tab_rows), (512, 256, 128, 8))
    n_tab = tab_rows // tm
    col = lambda off, w: (lambda i: (i, off // w))
    tab_spec = pl.BlockSpec((tm, LANES), lambda i: (i % n_tab, 0))
    row = lambda w: pl.BlockSpec((tm, w), lambda i: (i, 0))
    head_rows = pl.BlockSpec((tm * N_KV_HEADS, HEAD_DIM), lambda i: (i, 0))
    ki_blk = ki_col // LANES
    outs = pl.pallas_call(
        _qk_post_body,
        grid=(m // tm,),
        in_specs=[
            pl.BlockSpec((tm, ATT_W), col(COL_Q, ATT_W)),
            pl.BlockSpec((tm, KV_W), col(COL_K, KV_W)),
            pl.BlockSpec((tm, KV_W), col(COL_V, KV_W)),
            pl.BlockSpec((tm, IDX_W), col(COL_QI, IDX_W)),
            pl.BlockSpec((tm, LANES), lambda i: (i, ki_blk)),
            tab_spec, tab_spec, tab_spec, tab_spec,
            pl.BlockSpec((1, HEAD_DIM), lambda i: (0, 0)),
            pl.BlockSpec((1, HEAD_DIM), lambda i: (0, 0)),
        ],
        out_specs=[row(ATT_W), head_rows, row(KV_W), head_rows, row(KV_W), row(IDX_W), row(IDX_DIM),
                   row(LANES), row(LANES)],
        out_shape=[
            jax.ShapeDtypeStruct((m, ATT_W), act_dtype),
            jax.ShapeDtypeStruct((m * N_KV_HEADS, HEAD_DIM), F32),
            jax.ShapeDtypeStruct((m, KV_W), act_dtype),
            jax.ShapeDtypeStruct((m * N_KV_HEADS, HEAD_DIM), F32),
            jax.ShapeDtypeStruct((m, KV_W), act_dtype),
            jax.ShapeDtypeStruct((m, IDX_W), act_dtype),
            jax.ShapeDtypeStruct((m, IDX_DIM), F32),
            jax.ShapeDtypeStruct((m, LANES), act_dtype),
            jax.ShapeDtypeStruct((m, LANES), act_dtype),
        ],
        compiler_params=_params(),
        name="qk_post",
    )(proj, proj, proj, proj, proj, *tabs, q_norm.reshape(1, HEAD_DIM), k_norm.reshape(1, HEAD_DIM))
    return outs


def _strict_upper(n):
    r = lax.broadcasted_iota(jnp.int32, (n, n), 0)
    c = lax.broadcasted_iota(jnp.int32, (n, n), 1)
    return jnp.where(r < c, 1.0, 0.0).astype(BF16)


def _row_count(mask):
    return jnp.sum(jnp.where(mask, 1.0, 0.0), axis=1, keepdims=True)


def _attn_prompt_body(q_ref, qi_ref, kiw_ref, k_ref, v_ref, kilo_ref, kihi_ref, o_ref,
                      key_scr, bias_scr, lg_scr, m_scr, l_scr, acc_scr, *, topk):
    i = pl.program_id(1)
    tq = q_ref.shape[0]
    wc = key_scr.shape[2]
    n_ch = i + 1
    wi = kiw_ref[:, IDX_DIM:IDX_DIM + N_IDX_HEADS]
    row = i * tq + lax.broadcasted_iota(jnp.int32, (tq, wc), 0)
    col = lax.broadcasted_iota(jnp.int32, (tq, wc), 1)

    def score_chunk(kc, carry):
        r0 = pl.multiple_of(kc * wc, wc)
        ki_lo = kilo_ref[pl.ds(r0, wc), :]
        ki_hi = kihi_ref[pl.ds(r0, wc), :]
        score = jnp.zeros((tq, wc), F32)
        for pair in range(N_IDX_HEADS // 2):
            x = qi_ref[:, pair * LANES:(pair + 1) * LANES]
            score = score + wi[:, 2 * pair:2 * pair + 1] * jnp.maximum(_nt_dot(x, ki_lo), 0.0)
            score = score + wi[:, 2 * pair + 1:2 * pair + 2] * jnp.maximum(_nt_dot(x, ki_hi), 0.0)
        key_scr[kc] = jnp.where(kc * wc + col <= row, _sort_key(score * IDX_SCALE), INT_MIN)
        return carry

    lax.fori_loop(0, n_ch, score_chunk, 0)

    def search(it, res):
        trial = res + jnp.left_shift(jnp.int32(1), 31 - it)

        def count_chunk(kc, acc):
            hit = jnp.where(key_scr[kc] >= trial, 1.0, 0.0)
            for j in range(wc // LANES):
                acc = acc + hit[:, j * LANES:(j + 1) * LANES]
            return acc

        acc = lax.fori_loop(0, n_ch, count_chunk, jnp.zeros((tq, LANES), F32))
        return jnp.where(jnp.sum(acc, axis=1, keepdims=True) >= float(topk), trial, res)

    thr = lax.fori_loop(0, 32, search, jnp.full((tq, 1), INT_MIN, jnp.int32))

    def bias_chunk(kc, n_keep):
        key = key_scr[kc]
        keep = (key >= thr) & (key > KEY_NEG_INF)
        bias_scr[kc] = jnp.where(keep, 0.0, MASKED)
        return n_keep + _row_count(keep)

    n_keep = lax.fori_loop(0, n_ch, bias_chunk, jnp.zeros((tq, 1), F32))

    @pl.when(jnp.max(n_keep) > float(topk))
    def _():
        n_gt = lax.fori_loop(0, n_ch, lambda kc, n: n + _row_count(key_scr[kc] > thr), jnp.zeros((tq, 1), F32))
        need = float(topk) - n_gt
        upper = _strict_upper(wc)

        def tie_chunk(kc, before):
            key = key_scr[kc]
            eq = key == thr
            eqf = jnp.where(eq, 1.0, 0.0)
            rank = before + _dot(eqf.astype(BF16), upper)
            keep = ((key > thr) | (eq & (rank < need))) & (key > KEY_NEG_INF)
            bias_scr[kc] = jnp.where(keep, 0.0, MASKED)
            return before + jnp.sum(eqf, axis=1, keepdims=True)

        lax.fori_loop(0, n_ch, tie_chunk, jnp.zeros((tq, 1), F32))

    m_scr[...] = jnp.full(m_scr.shape, MASKED, F32)
    l_scr[...] = jnp.zeros(l_scr.shape, F32)
    acc_scr[...] = jnp.zeros(acc_scr.shape, F32)
    lane_blocks = [slice(j * LANES, (j + 1) * LANES) for j in range(wc // LANES)]

    def logit_chunk(kc, carry):
        r0 = pl.multiple_of(kc * wc, wc)
        bias = bias_scr[kc]
        for h in range(N_HEADS):
            g = h // GQA_GROUP
            kg = k_ref[pl.ds(r0, wc), g * HEAD_DIM:(g + 1) * HEAD_DIM]
            lg = _nt_dot(q_ref[:, h * HEAD_DIM:(h + 1) * HEAD_DIM], kg) * ATT_SCALE + bias
            lg_scr[h, kc] = lg
            mx = m_scr[h]
            for lb in lane_blocks:
                mx = jnp.maximum(mx, lg[:, lb])
            m_scr[h] = mx
        return carry

    lax.fori_loop(0, n_ch, logit_chunk, 0)
    for h in range(N_HEADS):
        m_scr[h] = jnp.broadcast_to(jnp.max(m_scr[h], axis=1, keepdims=True), (tq, LANES))

    def prob_chunk(kc, carry):
        r0 = pl.multiple_of(kc * wc, wc)
        for h in range(N_HEADS):
            g = h // GQA_GROUP
            sl = slice(h * HEAD_DIM, (h + 1) * HEAD_DIM)
            vg = v_ref[pl.ds(r0, wc), g * HEAD_DIM:(g + 1) * HEAD_DIM]
            lg = lg_scr[h, kc]
            mx = m_scr[h]
            ps = [jnp.exp(lg[:, lb] - mx) for lb in lane_blocks]
            den = l_scr[h]
            for p in ps:
                den = den + p
            l_scr[h] = den
            acc_scr[:, sl] = acc_scr[:, sl] + _dot(jnp.concatenate(ps, axis=1).astype(BF16), vg)
        return carry

    lax.fori_loop(0, n_ch, prob_chunk, 0)
    for h in range(N_HEADS):
        sl = slice(h * HEAD_DIM, (h + 1) * HEAD_DIM)
        den = jnp.sum(l_scr[h], axis=1, keepdims=True)
        o_ref[:, sl] = (acc_scr[:, sl] / den).astype(o_ref.dtype)


def _attn_prompt(q, qi, proj, ki_col, k, v, ki_lo, ki_hi, nb, s):
    tq = _pick(s, (256, 128))
    nq = s // tq
    topk = min(TOPK_MAX, s // 4)
    ki_blk = ki_col // LANES
    qrow = lambda w: pl.BlockSpec((tq, w), lambda b, i: (b * nq + i, 0))
    full = lambda w: pl.BlockSpec((s, w), lambda b, i: (b, 0))
    return pl.pallas_call(
        functools.partial(_attn_prompt_body, topk=topk),
        grid=(nb, nq),
        in_specs=[qrow(ATT_W), qrow(IDX_W), pl.BlockSpec((tq, LANES), lambda b, i: (b * nq + i, ki_blk)),
                  full(KV_W), full(KV_W), full(LANES), full(LANES)],
        out_specs=qrow(ATT_W),
        out_shape=jax.ShapeDtypeStruct((nb * s, ATT_W), BF16),
        scratch_shapes=[pltpu.VMEM((nq, tq, tq), jnp.int32), pltpu.VMEM((nq, tq, tq), F32),
                        pltpu.VMEM((N_HEADS, nq, tq, tq), F32),
                        pltpu.VMEM((N_HEADS, tq, LANES), F32), pltpu.VMEM((N_HEADS, tq, LANES), F32),
                        pltpu.VMEM((tq, ATT_W), F32)],
        compiler_params=_params(),
        name="attn_prompt",
    )(q, qi, proj, k, v, ki_lo, ki_hi)


def _s5_discretize_body(are_ref, aim_ref, ldt_ref, bre_ref, bim_ref, abre_ref, abim_ref, bbre_ref, bbim_ref):
    dt = jnp.exp(ldt_ref[...])
    lr, li = are_ref[...], aim_ref[...]
    mag = jnp.exp(lr * dt)
    ab_re, ab_im = mag * jnp.cos(li * dt), mag * jnp.sin(li * dt)
    zr, zi = ab_re - 1.0, ab_im
    den = lr * lr + li * li
    fr = (zr * lr + zi * li) / den
    fi = (zi * lr - zr * li) / den
    abre_ref[...] = ab_re
    abim_ref[...] = ab_im
    br, bi = bre_ref[...], bim_ref[...]
    bbre_ref[...] = fr * br - fi * bi
    bbim_ref[...] = fr * bi + fi * br


def _s5_discretize(a_re, a_im, log_dt, b_re, b_im):
    depth = a_re.shape[0]
    gp = pl.BlockSpec((None, SSM_GROUPS, 1, SSM_P), lambda l: (l, 0, 0, 0))
    gwp = pl.BlockSpec((None, SSM_GROUPS, SSM_GROUP, SSM_P), lambda l: (l, 0, 0, 0))
    bt = lambda b: jnp.swapaxes(b, 2, 3)
    a4 = lambda a: a.reshape(depth, SSM_GROUPS, 1, SSM_P)
    return pl.pallas_call(
        _s5_discretize_body,
        grid=(depth,),
        in_specs=[gp, gp, pl.BlockSpec((None, SSM_GROUPS, 1, 1), lambda l: (l, 0, 0, 0)), gwp, gwp],
        out_specs=[gp, gp, gwp, gwp],
        out_shape=[jax.ShapeDtypeStruct((depth, SSM_GROUPS, 1, SSM_P), F32)] * 2
        + [jax.ShapeDtypeStruct((depth, SSM_GROUPS, SSM_GROUP, SSM_P), F32)] * 2,
        compiler_params=_params(),
        name="s5_discretize",
    )(a4(a_re), a4(a_im), log_dt.reshape(depth, SSM_GROUPS, 1, 1), bt(b_re), bt(b_im))


def _block_diag_in(bb_t):
    depth = bb_t.shape[0]
    gpb = SSM_GROUPS // SSM_BLOCKS
    x = bb_t.reshape(depth, SSM_BLOCKS, gpb, SSM_GROUP, SSM_P)
    eye = jnp.eye(gpb, dtype=bb_t.dtype)
    out = jnp.einsum('lkgwp,gh->lkgwhp', x, eye)
    return out.reshape(depth, SSM_BLOCKS, gpb * SSM_GROUP, gpb * SSM_P).astype(BF16)


def _block_diag_out(c):
    depth = c.shape[0]
    gpb = SSM_GROUPS // SSM_BLOCKS
    x = c.reshape(depth, SSM_BLOCKS, gpb, SSM_GROUP, SSM_P)
    eye = jnp.eye(gpb, dtype=c.dtype)
    out = jnp.einsum('lkgwp,gh->lkgphw', x, eye)
    return out.reshape(depth, SSM_BLOCKS, gpb * SSM_P, gpb * SSM_GROUP).astype(BF16)


def _s5_body(u_ref, bbre_ref, bbim_ref, cre_ref, cim_ref, abre_ref, abim_ref, d_ref, h0re_ref, h0im_ref,
             g_ref, htre_ref, htim_ref, bure, buim, hre, him, ust, gst, *, nb):
    c = pl.program_id(0)
    rows = bure.shape[0]
    n_t = rows // nb
    per_tile = SUBLANES // nb

    @pl.when(c == 0)
    def _():
        hre[...] = h0re_ref[...]
        him[...] = h0im_ref[...]

    for k in range(SSM_BLOCKS):
        ln = slice(k * LANES, (k + 1) * LANES)
        for b in range(nb):
            if n_t == 1:
                ust[k, b:b + 1, :] = u_ref[b, :, ln]
            else:
                ust[k, pl.ds(b, n_t, stride=nb), :] = u_ref[b, :, ln]
        uk = ust[k].astype(BF16)
        st = slice(k * SSM_BLOCK_STATE, (k + 1) * SSM_BLOCK_STATE)
        bure[:, st] = _dot(uk, bbre_ref[k])
        buim[:, st] = _dot(uk, bbim_ref[k])

    for cc in range(SSM_STATE // SCAN_LANES):
        sl = slice(cc * SCAN_LANES, (cc + 1) * SCAN_LANES)
        ar = jnp.broadcast_to(abre_ref[:, sl], (nb, SCAN_LANES))
        ai = jnp.broadcast_to(abim_ref[:, sl], (nb, SCAN_LANES))

        def step(j, carry, sl=sl, ar=ar, ai=ai):
            hr, hi = carry
            r0 = pl.multiple_of(j * SUBLANES, SUBLANES)
            xr = bure[pl.ds(r0, SUBLANES), sl]
            xi = buim[pl.ds(r0, SUBLANES), sl]
            out_r, out_i = [], []
            for t in range(per_tile):
                nr = ar * hr - ai * hi + xr[t * nb:(t + 1) * nb]
                ni = ar * hi + ai * hr + xi[t * nb:(t + 1) * nb]
                hr, hi = nr, ni
                out_r.append(hr)
                out_i.append(hi)
            bure[pl.ds(r0, SUBLANES), sl] = out_r[0] if per_tile == 1 else jnp.concatenate(out_r, axis=0)
            buim[pl.ds(r0, SUBLANES), sl] = out_i[0] if per_tile == 1 else jnp.concatenate(out_i, axis=0)
            return hr, hi

        hr, hi = lax.fori_loop(0, rows // SUBLANES, step, (hre[:, sl], him[:, sl]))
        hre[:, sl] = hr
        him[:, sl] = hi

    for k in range(SSM_BLOCKS):
        st = slice(k * SSM_BLOCK_STATE, (k + 1) * SSM_BLOCK_STATE)
        ln = slice(k * LANES, (k + 1) * LANES)
        y = _dot(bure[:, st].astype(BF16), cre_ref[k]) - _dot(buim[:, st].astype(BF16), cim_ref[k])
        y = y + d_ref[:, ln] * ust[k]
        gst[k] = _gelu_tanh(y)
        for b in range(nb):
            if n_t == 1:
                g_ref[b, :, ln] = gst[k, b:b + 1, :]
            else:
                g_ref[b, :, ln] = gst[k, pl.ds(b, n_t, stride=nb), :]

    @pl.when(c == pl.num_programs(0) - 1)
    def _():
        htre_ref[...] = hre[...]
        htim_ref[...] = him[...]


def _s5(proj, nb, s, l, bb_re, bb_im, c_re, c_im, ab_re, ab_im, d, h0_re, h0_im):
    n_t = min(s, 512 // nb)
    rows = n_t * nb
    blk4 = lambda a: pl.BlockSpec((None,) + a.shape[1:], lambda c: (l, 0, 0, 0))
    full2 = lambda a: pl.BlockSpec(a.shape, lambda c: (0, 0))
    vec = lambda a: a.reshape(1, -1)
    ab_re, ab_im, d = vec(ab_re), vec(ab_im), vec(d)
    u_spec = pl.BlockSpec((nb, n_t, SSM_W), lambda c: (0, c, 0))
    g, h_re, h_im = pl.pallas_call(
        functools.partial(_s5_body, nb=nb),
        grid=(s // n_t,),
        in_specs=[u_spec, blk4(bb_re), blk4(bb_im), blk4(c_re), blk4(c_im),
                  full2(ab_re), full2(ab_im), full2(d), full2(h0_re), full2(h0_im)],
        out_specs=[u_spec, full2(h0_re), full2(h0_im)],
        out_shape=[jax.ShapeDtypeStruct((nb, s, SSM_W), F32),
                   jax.ShapeDtypeStruct(h0_re.shape, F32), jax.ShapeDtypeStruct(h0_im.shape, F32)],
        scratch_shapes=[pltpu.VMEM((rows, SSM_STATE), F32), pltpu.VMEM((rows, SSM_STATE), F32),
                        pltpu.VMEM((nb, SSM_STATE), F32), pltpu.VMEM((nb, SSM_STATE), F32),
                        pltpu.VMEM((SSM_BLOCKS, rows, LANES), F32), pltpu.VMEM((SSM_BLOCKS, rows, LANES), F32)],
        compiler_params=_params(),
        name="s5_scan",
    )(proj.reshape(nb, s, proj.shape[1]), bb_re, bb_im, c_re, c_im, ab_re, ab_im, d, h0_re, h0_im)
    return g.reshape(nb * s, SSM_W), h_re, h_im


IDX_PAGES_PER_STEP = 16
ATT_PAGES_PER_STEP = 8


def _sample_index_body(pt_ref, qi_ref, wi_ref, kin_ref, *rest, n_pg, topk):
    del pt_ref
    pages = rest[:n_pg]
    bias_ref, bnew_ref, sc_scr = rest[n_pg:]
    c = pl.program_id(1)
    n_pages = sc_scr.shape[0]
    qi = qi_ref[...].astype(BF16)
    wi = wi_ref[...]

    def weighted(d):
        return jnp.sum(wi * jnp.maximum(d, 0.0), axis=0, keepdims=True) * IDX_SCALE

    for p in range(n_pg):
        sc_scr[pl.ds(c * n_pg + p, 1), :] = weighted(_dot(qi, pages[p][...].astype(BF16)))

    @pl.when(c == pl.num_programs(1) - 1)
    def _():
        kin = kin_ref[...].astype(BF16).astype(F32)
        d_new = jnp.sum(qi.astype(F32) * kin, axis=1, keepdims=True)
        key_new = _sort_key(jnp.broadcast_to(weighted(d_new), (1, LANES)))
        key = _sort_key(sc_scr[...])

        def count(mask, mask_new):
            per_lane = jnp.sum(jnp.where(mask, 1.0, 0.0), axis=0, keepdims=True)
            return jnp.sum(per_lane, axis=1, keepdims=True) + jnp.where(mask_new, 1.0, 0.0)

        def body(it, res):
            trial = res + jnp.left_shift(jnp.int32(1), 31 - it)
            return jnp.where(count(key >= trial, key_new >= trial) >= float(topk), trial, res)

        thr = lax.fori_loop(0, 32, body, jnp.full((1, LANES), INT_MIN, jnp.int32))
        need = float(topk) - count(key > thr, key_new > thr)
        eq = key == thr
        eqf = jnp.where(eq, 1.0, 0.0).astype(BF16)
        in_page = _dot(eqf, _strict_upper(LANES))
        r = lax.broadcasted_iota(jnp.int32, (n_pages, n_pages), 0)
        cidx = lax.broadcasted_iota(jnp.int32, (n_pages, n_pages), 1)
        lower = jnp.where(cidx < r, 1.0, 0.0).astype(BF16)
        per_page = jnp.sum(jnp.where(eq, 1.0, 0.0), axis=1, keepdims=True)
        earlier = _dot(lower, jnp.broadcast_to(per_page, (n_pages, LANES)).astype(BF16))
        keep = ((key > thr) | (eq & (in_page + earlier < need))) & (key > KEY_NEG_INF)
        bias_ref[...] = jnp.where(keep, 0.0, MASKED)
        n_eq_past = jnp.sum(jnp.sum(jnp.where(eq, 1.0, 0.0), axis=0, keepdims=True), axis=1, keepdims=True)
        keep_new = ((key_new > thr) | ((key_new == thr) & (n_eq_past < need))) & (key_new > KEY_NEG_INF)
        bnew_ref[...] = jnp.where(keep_new, 0.0, MASKED)


def _sample_index(page_table, cache_kidx_t, layer, qi, wi, ki_new):
    nb, n_pages = page_table.shape
    n_pg = _pick(n_pages, (IDX_PAGES_PER_STEP, 8, 4, 2, 1))
    topk = min(TOPK_MAX, (n_pages * PAGE_SIZE + 1) // 4)
    page_spec = lambda p: pl.BlockSpec((None, None, IDX_DIM, PAGE_SIZE),
                                       lambda b, c, pt: (layer, pt[b, c * n_pg + p], 0, 0))
    per_b = lambda shape: pl.BlockSpec((None,) + shape, lambda b, c, pt: (b, 0, 0))
    grid_spec = pltpu.PrefetchScalarGridSpec(
        num_scalar_prefetch=1,
        grid=(nb, n_pages // n_pg),
        in_specs=[per_b((N_IDX_HEADS, IDX_DIM)), per_b((N_IDX_HEADS, 1)), per_b((1, IDX_DIM))]
        + [page_spec(p) for p in range(n_pg)],
        out_specs=[per_b((n_pages, LANES)), per_b((1, LANES))],
        scratch_shapes=[pltpu.VMEM((n_pages, LANES), F32)],
    )
    return pl.pallas_call(
        functools.partial(_sample_index_body, n_pg=n_pg, topk=topk),
        grid_spec=grid_spec,
        out_shape=[jax.ShapeDtypeStruct((nb, n_pages, LANES), F32), jax.ShapeDtypeStruct((nb, 1, LANES), F32)],
        compiler_params=_params(),
        name="sample_index",
    )(page_table, qi.reshape(nb, N_IDX_HEADS, IDX_DIM), wi.reshape(nb, N_IDX_HEADS, 1),
      ki_new.reshape(nb, 1, IDX_DIM), *([cache_kidx_t] * n_pg))


def _sample_attn_body(pt_ref, bias_ref, bnew_ref, q_ref, kn_ref, vn_ref, *rest, n_pg):
    del pt_ref
    kpages = rest[:n_pg]
    vpages = rest[n_pg:2 * n_pg]
    o_ref, m_scr, l_scr, acc_scr = rest[2 * n_pg:]
    c = pl.program_id(1)

    @pl.when(c == 0)
    def _():
        m_scr[...] = jnp.full(m_scr.shape, MASKED, F32)
        l_scr[...] = jnp.zeros(l_scr.shape, F32)
        acc_scr[...] = jnp.zeros(acc_scr.shape, F32)

    q = q_ref[...].astype(BF16)
    head = lax.broadcasted_iota(jnp.int32, (N_HEADS, LANES), 0)
    in_group = [(head >= g * GQA_GROUP) & (head < (g + 1) * GQA_GROUP) for g in range(N_KV_HEADS)]
    for p in range(n_pg):
        bias = bias_ref[p:p + 1, :]
        sel = bias == 0.0
        lg = jnp.zeros((N_HEADS, PAGE_SIZE), F32)
        for g in range(N_KV_HEADS):
            kg = kpages[p][pl.ds(g, PAGE_SIZE, stride=N_KV_HEADS), :].astype(BF16)
            lg = jnp.where(in_group[g], _nt_dot(q, kg), lg)
        lg = lg * ATT_SCALE + bias
        m_old = m_scr[...]
        m_new = jnp.maximum(m_old, jnp.max(lg, axis=1, keepdims=True))
        alpha = jnp.exp(m_old - m_new)
        pr = jnp.where(sel, jnp.exp(lg - m_new), 0.0)
        l_scr[...] = alpha * l_scr[...] + jnp.sum(pr, axis=1, keepdims=True)
        prb = pr.astype(BF16)
        pv = jnp.zeros((N_HEADS, HEAD_DIM), F32)
        for g in range(N_KV_HEADS):
            vg = vpages[p][pl.ds(g, PAGE_SIZE, stride=N_KV_HEADS), :].astype(BF16)
            pv = jnp.where(in_group[g], _dot(prb, vg), pv)
        acc_scr[...] = alpha * acc_scr[...] + pv
        m_scr[...] = m_new

    @pl.when(c == pl.num_programs(1) - 1)
    def _():
        bias_new = bnew_ref[:, :1]
        sel_new = bias_new == 0.0
        qf = q.astype(F32)
        kn = kn_ref[...].astype(BF16).astype(F32)
        vn = vn_ref[...].astype(BF16).astype(F32)
        lg = jnp.sum(qf * kn, axis=1, keepdims=True) * ATT_SCALE + bias_new
        m_old = m_scr[...]
        m_new = jnp.maximum(m_old, lg)
        alpha = jnp.exp(m_old - m_new)
        pr = jnp.where(sel_new, jnp.exp(lg - m_new), 0.0)
        den = alpha * l_scr[...] + pr
        prq = pr.astype(BF16).astype(F32)
        o_ref[...] = ((alpha * acc_scr[...] + prq * vn) / den).astype(o_ref.dtype)


def _sample_attn(page_table, cache_k2, cache_v2, layer, bias, bias_new, q, k_new, v_new):
    nb, n_pages = page_table.shape
    n_pg = _pick(n_pages, (ATT_PAGES_PER_STEP, 4, 2, 1))
    page_rows = PAGE_SIZE * N_KV_HEADS
    page_spec = lambda p: pl.BlockSpec((None, None, page_rows, HEAD_DIM),
                                       lambda b, c, pt: (layer, pt[b, c * n_pg + p], 0, 0))
    per_b = lambda shape: pl.BlockSpec((None,) + shape, lambda b, c, pt: (b, 0, 0))
    per_head = lambda a: jnp.repeat(a.reshape(nb, N_KV_HEADS, HEAD_DIM), GQA_GROUP, axis=1)
    grid_spec = pltpu.PrefetchScalarGridSpec(
        num_scalar_prefetch=1,
        grid=(nb, n_pages // n_pg),
        in_specs=[pl.BlockSpec((None, n_pg, LANES), lambda b, c, pt: (b, c, 0)), per_b((1, LANES)),
                  per_b((N_HEADS, HEAD_DIM)), per_b((N_HEADS, HEAD_DIM)), per_b((N_HEADS, HEAD_DIM))]
        + [page_spec(p) for p in range(n_pg)] * 2,
        out_specs=per_b((N_HEADS, HEAD_DIM)),
        scratch_shapes=[pltpu.VMEM((N_HEADS, 1), F32), pltpu.VMEM((N_HEADS, 1), F32),
                        pltpu.VMEM((N_HEADS, HEAD_DIM), F32)],
    )
    out = pl.pallas_call(
        functools.partial(_sample_attn_body, n_pg=n_pg),
        grid_spec=grid_spec,
        out_shape=jax.ShapeDtypeStruct((nb, N_HEADS, HEAD_DIM), F32),
        compiler_params=_params(),
        name="sample_attn",
    )(page_table, bias, bias_new, q.reshape(nb, N_HEADS, HEAD_DIM), per_head(k_new), per_head(v_new),
      *([cache_k2] * n_pg), *([cache_v2] * n_pg))
    return out.reshape(nb, ATT_W)


def _layer_tail(proj, g, y_att, x, w, l, act_dtype):
    z = _glu(g, w["w_glu"], l, act_dtype)
    merged = _merge(z, y_att, w["w_branch"], l, proj, act_dtype)
    x1, hn = _out_norm(merged, w["w_out"], l, x, w["norm_ffn"][l], act_dtype)
    act = _ffn_up(hn, w["w_ffn_up"], l, act_dtype)
    return _ffn_down(act, w["w_ffn_down"], l, x1)


def _s5_layer(proj, nb, s, w, l, h0_re, h0_im):
    return _s5(proj, nb, s, l, w["bb_re"], w["bb_im"], w["c_re"], w["c_im"], w["ab_re"][l], w["ab_im"][l],
               w["ssm_d"][l], h0_re, h0_im)


def _prompt_layer(x, nb, s, tabs, w, l):
    xn = _rmsnorm(x, w["norm_mix"][l], BF16)
    proj = _in_proj(xn, w["w_in"], l, PROJ_TN, F32)
    q, k_out, k_bf, v_out, v_bf, qi, ki_out, ki_lo, ki_hi = _qk_post(
        proj, w["ki_col"], tabs, s, w["q_norm"][l], w["k_norm"][l], BF16)
    y_att = _attn_prompt(q, qi, proj, w["ki_col"], k_bf, v_bf, ki_lo, ki_hi, nb, s)
    zeros = jnp.zeros((nb, SSM_STATE), F32)
    g, h_re, h_im = _s5_layer(proj, nb, s, w, l, zeros, zeros)
    x2 = _layer_tail(proj, g, y_att, x, w, l, BF16)
    return x2, k_out, v_out, ki_out, h_re, h_im


def _sample_layer(x, nb, tabs, w, l, page_table, cache_k2, cache_v2, cache_kidx_t, h0_re, h0_im):
    xn = _rmsnorm(x, w["norm_mix"][l], F32)
    proj = _in_proj(xn, w["w_in"], l, PROJ_TN, F32)
    ki_col = w["ki_col"]
    q, k_out, _, v_out, _, qi, ki_out, _, _ = _qk_post(proj, ki_col, tabs, nb, w["q_norm"][l], w["k_norm"][l], F32)
    wi = proj[:, ki_col + IDX_DIM:ki_col + IDX_DIM + N_IDX_HEADS]
    bias, bias_new = _sample_index(page_table, cache_kidx_t, l, qi, wi, ki_out)
    y_att = _sample_attn(page_table, cache_k2, cache_v2, l, bias, bias_new, q, k_out, v_out)
    g, h_re, h_im = _s5_layer(proj, nb, 1, w, l, h0_re, h0_im)
    x2 = _layer_tail(proj, g, y_att, x, w, l, F32)
    return x2, k_out, v_out, ki_out, h_re, h_im


def kernel(x_prompt, x_sample, cache_k, cache_v, cache_kidx, state_ssm_re, state_ssm_im, page_table, norm_mix, w_in, q_norm, k_norm, ssm_a_re, ssm_a_im, ssm_log_dt, ssm_b_re, ssm_b_im, ssm_c_re, ssm_c_im, ssm_d, w_glu, w_branch, w_out, norm_ffn, w_ffn_up, w_ffn_down):
    nb_p, s, d_model = x_prompt.shape
    nb_s, dec_seq, _ = x_sample.shape
    assert dec_seq == 1, "the sample group decodes one token per sequence"
    depth = w_in.shape[0]
    past = page_table.shape[1] * PAGE_SIZE

    ki_orig = COL_GATE
    gate_orig = ki_orig + IDX_DIM + N_IDX_HEADS
    n_gate = w_in.shape[2] - gate_orig
    ki_col = COL_GATE + n_gate
    assert ki_col % LANES == 0
    cols = ki_col + LANES
    cols_pad = -(-cols // PROJ_TN) * PROJ_TN
    w_in_r = jnp.concatenate(
        [w_in[:, :, :COL_GATE], w_in[:, :, gate_orig:], w_in[:, :, ki_orig:gate_orig],
         jnp.zeros((depth, d_model, cols_pad - ki_col - IDX_DIM - N_IDX_HEADS), w_in.dtype)], axis=2).astype(BF16)

    ab_re, ab_im, bbt_re, bbt_im = _s5_discretize(ssm_a_re, ssm_a_im, ssm_log_dt, ssm_b_re, ssm_b_im)
    bb_re, bb_im = _block_diag_in(bbt_re), _block_diag_in(bbt_im)
    c_re, c_im = _block_diag_out(ssm_c_re), _block_diag_out(ssm_c_im)

    tabs_p = _rope_tables(jnp.arange(s, dtype=jnp.int32))
    tabs_s = _rope_tables(jnp.full((nb_s,), past, dtype=jnp.int32))

    xp = x_prompt.reshape(nb_p * s, d_model)
    xs = x_sample.reshape(nb_s, d_model)
    cache_k2 = cache_k.reshape(cache_k.shape[:2] + (PAGE_SIZE * N_KV_HEADS, HEAD_DIM))
    cache_v2 = cache_v.reshape(cache_v.shape[:2] + (PAGE_SIZE * N_KV_HEADS, HEAD_DIM))
    cache_kidx_t = jnp.swapaxes(cache_kidx, 2, 3)
    w = dict(ki_col=ki_col, norm_mix=norm_mix, w_in=w_in_r, q_norm=q_norm, k_norm=k_norm,
             bb_re=bb_re, bb_im=bb_im, c_re=c_re, c_im=c_im, ab_re=ab_re, ab_im=ab_im, ssm_d=ssm_d,
             w_glu=w_glu, w_branch=w_branch, w_out=w_out, norm_ffn=norm_ffn,
             w_ffn_up=w_ffn_up, w_ffn_down=w_ffn_down)
    outs_p, outs_s = [], []
    for l in range(depth):
        xp, *rest_p = _prompt_layer(xp, nb_p, s, tabs_p, w, l)
        xs, *rest_s = _sample_layer(xs, nb_s, tabs_s, w, l, page_table, cache_k2, cache_v2, cache_kidx_t,
                                    state_ssm_re[l].reshape(nb_s, SSM_STATE), state_ssm_im[l].reshape(nb_s, SSM_STATE))
        outs_p.append(rest_p)
        outs_s.append(rest_s)

    def stack(outs, idx, shape):
        return jnp.stack([o[idx].reshape(shape) for o in outs])

    kv_p, kv_s = (nb_p, s, N_KV_HEADS, HEAD_DIM), (nb_s, dec_seq, N_KV_HEADS, HEAD_DIM)
    st_p, st_s = (nb_p, SSM_GROUPS, SSM_P), (nb_s, SSM_GROUPS, SSM_P)
    return (xp.reshape(nb_p, s, d_model), xs.reshape(nb_s, dec_seq, d_model),
            stack(outs_p, 0, kv_p), stack(outs_p, 1, kv_p), stack(outs_p, 2, (nb_p, s, IDX_DIM)),
            stack(outs_p, 3, st_p), stack(outs_p, 4, st_p),
            stack(outs_s, 0, kv_s), stack(outs_s, 1, kv_s), stack(outs_s, 2, (nb_s, dec_seq, IDX_DIM)),
            stack(outs_s, 3, st_s), stack(outs_s, 4, st_s))
```

```python
import functools
import math

import jax
import jax.numpy as jnp
from jax import lax
from jax.experimental import pallas as pl
from jax.experimental.pallas import tpu as pltpu

F32 = jnp.float32
BF16 = jnp.bfloat16

SSM_W = 1024
SSM_GROUP = 16
SSM_GROUPS = SSM_W // SSM_GROUP
SSM_P = 64
SSM_STATE = SSM_GROUPS * SSM_P
N_HEADS = 8
N_KV_HEADS = 4
HEAD_DIM = 128
GQA_GROUP = N_HEADS // N_KV_HEADS
ATT_W = N_HEADS * HEAD_DIM
KV_W = N_KV_HEADS * HEAD_DIM
N_IDX_HEADS = 16
IDX_DIM = 64
IDX_W = N_IDX_HEADS * IDX_DIM
IDX_SCALE = (IDX_DIM * N_IDX_HEADS) ** -0.5
ATT_SCALE = HEAD_DIM ** -0.5
TOPK_MAX = 256
PAGE_SIZE = 128
ROPE_THETA = 10000.0
EPS = 1e-6

LANES = 128
SUBLANES = 8
VMEM_LIMIT = 56 * 1024 * 1024

COL_U = 0
COL_Q = SSM_W
COL_K = COL_Q + ATT_W
COL_V = COL_K + KV_W
COL_QI = COL_V + KV_W
COL_GATE = COL_QI + IDX_W
PROJ_TN = 768

MASKED = -1e30
INT_MIN = -(2 ** 31)
KEY_NEG_INF = -2139095041

SCAN_LANES = 512
SSM_BLOCKS = SSM_W // LANES
SSM_BLOCK_STATE = SSM_STATE // SSM_BLOCKS


def _params(**kw):
    return pltpu.CompilerParams(vmem_limit_bytes=VMEM_LIMIT, **kw)


def _pick(n, cands):
    for c in cands:
        if n % c == 0:
            return c
    return n


def _nt_dot(a, b):
    return lax.dot_general(a, b, (((1,), (1,)), ((), ())), preferred_element_type=F32)


def _dot(a, b):
    return jnp.dot(a, b, preferred_element_type=F32)


def _sigmoid(x):
    return 1.0 / (1.0 + jnp.exp(-x))


def _gelu_tanh(x):
    c = math.sqrt(2.0 / math.pi)
    return 0.5 * x * (1.0 + jnp.tanh(c * (x + 0.044715 * (x * x * x))))


def _sort_key(s):
    bits = pltpu.bitcast(s, jnp.int32)
    return bits ^ ((bits >> 31) & jnp.int32(0x7FFFFFFF))


def _rmsnorm_body(x_ref, g_ref, o_ref):
    x = x_ref[...].astype(F32)
    ms = jnp.mean(x * x, axis=-1, keepdims=True)
    o_ref[...] = ((x * lax.rsqrt(ms + EPS)) * g_ref[...]).astype(o_ref.dtype)


def _rmsnorm(x, g, out_dtype):
    m, d = x.shape
    tm = _pick(m, (512, 256, 128, 8))
    return pl.pallas_call(
        _rmsnorm_body,
        grid=(m // tm,),
        in_specs=[pl.BlockSpec((tm, d), lambda i: (i, 0)), pl.BlockSpec((1, d), lambda i: (0, 0))],
        out_specs=pl.BlockSpec((tm, d), lambda i: (i, 0)),
        out_shape=jax.ShapeDtypeStruct((m, d), out_dtype),
        compiler_params=_params(),
        name="rmsnorm",
    )(x, g.reshape(1, d))


def _mm_body(a_ref, w_ref, o_ref):
    o_ref[...] = _dot(a_ref[...].astype(BF16), w_ref[...]).astype(o_ref.dtype)


def _in_proj(a, w, l, tn, out_dtype):
    m, k = a.shape
    n = w.shape[2]
    tm = _pick(m, (1024, 512, 256, 128, 8))
    return pl.pallas_call(
        _mm_body,
        grid=(n // tn, m // tm),
        in_specs=[pl.BlockSpec((tm, k), lambda j, i: (i, 0)), pl.BlockSpec((None, k, tn), lambda j, i: (l, 0, j))],
        out_specs=pl.BlockSpec((tm, tn), lambda j, i: (i, j)),
        out_shape=jax.ShapeDtypeStruct((m, n), out_dtype),
        compiler_params=_params(),
        name="in_proj",
    )(a, w)


def _cast_weight_once(step, w_ref, wb_scr):
    @pl.when(step == 0)
    def _():
        wb_scr[...] = w_ref[...].astype(BF16)


def _glu_body(g_ref, w_ref, o_ref, wb):
    _cast_weight_once(pl.program_id(0), w_ref, wb)
    g = g_ref[...]
    z = _dot(g.astype(BF16), wb[...])
    o_ref[...] = (g * _sigmoid(z)).astype(o_ref.dtype)


def _glu(g, w, l, out_dtype):
    m, n = g.shape
    tm = _pick(m, (512, 256, 128, 8))
    return pl.pallas_call(
        _glu_body,
        grid=(m // tm,),
        in_specs=[pl.BlockSpec((tm, n), lambda i: (i, 0)),
                  pl.BlockSpec((None, n, n), lambda i: (l, 0, 0), pipeline_mode=pl.Buffered(1))],
        out_specs=pl.BlockSpec((tm, n), lambda i: (i, 0)),
        out_shape=jax.ShapeDtypeStruct((m, n), out_dtype),
        scratch_shapes=[pltpu.VMEM((n, n), BF16)],
        compiler_params=_params(),
        name="glu",
    )(g, w)


def _merge_body(a0_ref, a1_ref, w0_ref, w1_ref, g0_ref, g1_ref, o_ref, wb0, wb1):
    _cast_weight_once(pl.program_id(1), w0_ref, wb0)
    _cast_weight_once(pl.program_id(1), w1_ref, wb1)
    y0 = _dot(a0_ref[...].astype(BF16), wb0[...])
    y1 = _dot(a1_ref[...].astype(BF16), wb1[...])
    o_ref[...] = (_sigmoid(g0_ref[...]) * y0 + _sigmoid(g1_ref[...]) * y1).astype(o_ref.dtype)


def _merge(a0, a1, w_branch, l, proj, out_dtype):
    m, k = a0.shape
    n = w_branch.shape[3]
    tm = _pick(m, (512, 256, 128, 8))
    tn = _pick(n, (1024, 512, 256, 128))
    g0_blk = COL_GATE // tn
    g1_blk = (COL_GATE + n) // tn
    wspec = lambda br: pl.BlockSpec((None, None, k, tn), lambda j, i: (l, br, 0, j))
    return pl.pallas_call(
        _merge_body,
        grid=(n // tn, m // tm),
        in_specs=[
            pl.BlockSpec((tm, k), lambda j, i: (i, 0)),
            pl.BlockSpec((tm, k), lambda j, i: (i, 0)),
            wspec(0), wspec(1),
            pl.BlockSpec((tm, tn), lambda j, i: (i, g0_blk + j)),
            pl.BlockSpec((tm, tn), lambda j, i: (i, g1_blk + j)),
        ],
        out_specs=pl.BlockSpec((tm, tn), lambda j, i: (i, j)),
        out_shape=jax.ShapeDtypeStruct((m, n), out_dtype),
        scratch_shapes=[pltpu.VMEM((k, tn), BF16), pltpu.VMEM((k, tn), BF16)],
        compiler_params=_params(),
        name="merge",
    )(a0, a1, w_branch, w_branch, proj, proj)


def _out_norm_body(a_ref, w_ref, x_ref, g_ref, x1_ref, hn_ref, wb):
    _cast_weight_once(pl.program_id(0), w_ref, wb)
    x1 = x_ref[...] + _dot(a_ref[...].astype(BF16), wb[...])
    x1_ref[...] = x1
    ms = jnp.mean(x1 * x1, axis=-1, keepdims=True)
    hn_ref[...] = ((x1 * lax.rsqrt(ms + EPS)) * g_ref[...]).astype(hn_ref.dtype)


def _out_norm(a, w, l, x, g, hn_dtype):
    m, k = a.shape
    n = w.shape[2]
    tm = _pick(m, (256, 128, 8))
    return pl.pallas_call(
        _out_norm_body,
        grid=(m // tm,),
        in_specs=[
            pl.BlockSpec((tm, k), lambda i: (i, 0)),
            pl.BlockSpec((None, k, n), lambda i: (l, 0, 0), pipeline_mode=pl.Buffered(1)),
            pl.BlockSpec((tm, n), lambda i: (i, 0)),
            pl.BlockSpec((1, n), lambda i: (0, 0)),
        ],
        out_specs=[pl.BlockSpec((tm, n), lambda i: (i, 0)), pl.BlockSpec((tm, n), lambda i: (i, 0))],
        out_shape=[jax.ShapeDtypeStruct((m, n), F32), jax.ShapeDtypeStruct((m, n), hn_dtype)],
        scratch_shapes=[pltpu.VMEM((k, n), BF16)],
        compiler_params=_params(),
        name="out_proj_norm",
    )(a, w, x, g.reshape(1, n))


def _ffn_up_body(a_ref, wg_ref, wu_ref, o_ref, wgb, wub):
    _cast_weight_once(pl.program_id(1), wg_ref, wgb)
    _cast_weight_once(pl.program_id(1), wu_ref, wub)
    a = a_ref[...].astype(BF16)
    gate = _dot(a, wgb[...])
    up = _dot(a, wub[...])
    o_ref[...] = ((gate * _sigmoid(gate)) * up).astype(o_ref.dtype)


def _ffn_up(a, w_up, l, out_dtype):
    m, k = a.shape
    d_ff = w_up.shape[2] // 2
    tm = _pick(m, (1024, 512, 256, 128, 8))
    tn = _pick(d_ff, (512, 256, 128))
    up_blk = d_ff // tn
    return pl.pallas_call(
        _ffn_up_body,
        grid=(d_ff // tn, m // tm),
        in_specs=[
            pl.BlockSpec((tm, k), lambda j, i: (i, 0)),
            pl.BlockSpec((None, k, tn), lambda j, i: (l, 0, j)),
            pl.BlockSpec((None, k, tn), lambda j, i: (l, 0, up_blk + j)),
        ],
        out_specs=pl.BlockSpec((tm, tn), lambda j, i: (i, j)),
        out_shape=jax.ShapeDtypeStruct((m, d_ff), out_dtype),
        scratch_shapes=[pltpu.VMEM((k, tn), BF16), pltpu.VMEM((k, tn), BF16)],
        compiler_params=_params(),
        name="ffn_up",
    )(a, w_up, w_up)


def _ffn_down_body(a_ref, w_ref, x_ref, o_ref, wb):
    _cast_weight_once(pl.program_id(1), w_ref, wb)
    o_ref[...] = x_ref[...] + _dot(a_ref[...].astype(BF16), wb[...])


def _ffn_down(a, w, l, x):
    m, k = a.shape
    n = w.shape[2]
    tm = _pick(m, (256, 128, 8))
    tn = _pick(n, (1024, 512, 256, 128))
    return pl.pallas_call(
        _ffn_down_body,
        grid=(n // tn, m // tm),
        in_specs=[
            pl.BlockSpec((tm, k), lambda j, i: (i, 0)),
            pl.BlockSpec((None, k, tn), lambda j, i: (l, 0, j), pipeline_mode=pl.Buffered(1)),
            pl.BlockSpec((tm, tn), lambda j, i: (i, j)),
        ],
        out_specs=pl.BlockSpec((tm, tn), lambda j, i: (i, j)),
        out_shape=jax.ShapeDtypeStruct((m, n), F32),
        scratch_shapes=[pltpu.VMEM((k, tn), BF16)],
        compiler_params=_params(),
        name="ffn_down",
    )(a, w, x)


def _rope_tables(pos):
    def tables(dim):
        half = dim // 2
        freqs = ROPE_THETA ** (-jnp.arange(half, dtype=F32) / half)
        ang = pos.astype(F32)[:, None] * freqs[None, :]
        cos, sin = jnp.cos(ang), jnp.sin(ang)
        reps = LANES // dim
        cos_t = jnp.tile(jnp.concatenate([cos, cos], axis=1), (1, reps))
        sin_t = jnp.tile(jnp.concatenate([-sin, sin], axis=1), (1, reps))
        return cos_t, sin_t
    return tables(HEAD_DIM) + tables(IDX_DIM)


def _rope_head(x, cos_t, sin_t):
    return x * cos_t + pltpu.roll(x, HEAD_DIM // 2, axis=1) * sin_t


def _rope_idx(x, cos_t, sin_t, first_half):
    half = IDX_DIM // 2
    partner = jnp.where(first_half, pltpu.roll(x, LANES - half, axis=1), pltpu.roll(x, half, axis=1))
    return x * cos_t + partner * sin_t


def _head_norm(x, g):
    ms = jnp.mean(x * x, axis=-1, keepdims=True)
    return (x * lax.rsqrt(ms + EPS)) * g


def _qk_post_body(q_ref, k_ref, v_ref, qi_ref, kiw_ref, ch_ref, sh_ref, ci_ref, si_ref, qn_ref, kn_ref,
                  qo_ref, ko_ref, kb_ref, vo_ref, vb_ref, qio_ref, kio_ref, kilo_ref, kihi_ref):
    ch, sh, ci, si = ch_ref[...], sh_ref[...], ci_ref[...], si_ref[...]
    tm = ch.shape[0]
    lane = lax.broadcasted_iota(jnp.int32, ch.shape, 1)
    first_half = (lane & (IDX_DIM - 1)) < (IDX_DIM // 2)
    for h in range(N_HEADS):
        sl = slice(h * HEAD_DIM, (h + 1) * HEAD_DIM)
        qo_ref[:, sl] = _rope_head(_head_norm(q_ref[:, sl], qn_ref[...]), ch, sh).astype(qo_ref.dtype)
    for h in range(N_KV_HEADS):
        sl = slice(h * HEAD_DIM, (h + 1) * HEAD_DIM)
        kr = _rope_head(_head_norm(k_ref[:, sl], kn_ref[...]), ch, sh)
        ko_ref[pl.ds(h, tm, stride=N_KV_HEADS), :] = kr
        vo_ref[pl.ds(h, tm, stride=N_KV_HEADS), :] = v_ref[:, sl]
        kb_ref[:, sl] = kr.astype(kb_ref.dtype)
    vb_ref[...] = v_ref[...].astype(vb_ref.dtype)
    for h in range(IDX_W // LANES):
        sl = slice(h * LANES, (h + 1) * LANES)
        qio_ref[:, sl] = _rope_idx(qi_ref[:, sl], ci, si, first_half).astype(qio_ref.dtype)
    kir = _rope_idx(kiw_ref[...], ci, si, first_half)
    kio_ref[...] = kir[:, :IDX_DIM]
    zero = jnp.zeros_like(kir)
    kilo_ref[...] = jnp.where(lane < IDX_DIM, kir, zero).astype(kilo_ref.dtype)
    kihi_ref[...] = jnp.where(lane < IDX_DIM, zero, pltpu.roll(kir, IDX_DIM, axis=1)).astype(kihi_ref.dtype)


def _qk_post(proj, ki_col, tabs, tab_rows, q_norm, k_norm, act_dtype):
    m = proj.shape[0]
    tm = _pick(min(m, tab_rows), (512, 256, 128, 8))
    n_tab = tab_rows // tm
    col = lambda off, w: (lambda i: (i, off // w))
    tab_spec = pl.BlockSpec((tm, LANES), lambda i: (i % n_tab, 0))
    row = lambda w: pl.BlockSpec((tm, w), lambda i: (i, 0))
    head_rows = pl.BlockSpec((tm * N_KV_HEADS, HEAD_DIM), lambda i: (i, 0))
    ki_blk = ki_col // LANES
    outs = pl.pallas_call(
        _qk_post_body,
        grid=(m // tm,),
        in_specs=[
            pl.BlockSpec((tm, ATT_W), col(COL_Q, ATT_W)),
            pl.BlockSpec((tm, KV_W), col(COL_K, KV_W)),
            pl.BlockSpec((tm, KV_W), col(COL_V, KV_W)),
            pl.BlockSpec((tm, IDX_W), col(COL_QI, IDX_W)),
            pl.BlockSpec((tm, LANES), lambda i: (i, ki_blk)),
            tab_spec, tab_spec, tab_spec, tab_spec,
            pl.BlockSpec((1, HEAD_DIM), lambda i: (0, 0)),
            pl.BlockSpec((1, HEAD_DIM), lambda i: (0, 0)),
        ],
        out_specs=[row(ATT_W), head_rows, row(KV_W), head_rows, row(KV_W), row(IDX_W), row(IDX_DIM),
                   row(LANES), row(LANES)],
        out_shape=[
            jax.ShapeDtypeStruct((m, ATT_W), act_dtype),
            jax.ShapeDtypeStruct((m * N_KV_HEADS, HEAD_DIM), F32),
            jax.ShapeDtypeStruct((m, KV_W), act_dtype),
            jax.ShapeDtypeStruct((m * N_KV_HEADS, HEAD_DIM), F32),
            jax.ShapeDtypeStruct((m, KV_W), act_dtype),
            jax.ShapeDtypeStruct((m, IDX_W), act_dtype),
            jax.ShapeDtypeStruct((m, IDX_DIM), F32),
            jax.ShapeDtypeStruct((m, LANES), act_dtype),
            jax.ShapeDtypeStruct((m, LANES), act_dtype),
        ],
        compiler_params=_params(),
        name="qk_post",
    )(proj, proj, proj, proj, proj, *tabs, q_norm.reshape(1, HEAD_DIM), k_norm.reshape(1, HEAD_DIM))
    return outs


def _strict_upper(n):
    r = lax.broadcasted_iota(jnp.int32, (n, n), 0)
    c = lax.broadcasted_iota(jnp.int32, (n, n), 1)
    return jnp.where(r < c, 1.0, 0.0).astype(BF16)


def _row_count(mask):
    return jnp.sum(jnp.where(mask, 1.0, 0.0), axis=1, keepdims=True)


def _attn_prompt_body(q_ref, qi_ref, kiw_ref, k_ref, v_ref, kilo_ref, kihi_ref, o_ref,
                      key_scr, bias_scr, lg_scr, m_scr, l_scr, acc_scr, *, topk):
    i = pl.program_id(1)
    tq = q_ref.shape[0]
    wc = key_scr.shape[2]
    n_ch = i + 1
    wi = kiw_ref[:, IDX_DIM:IDX_DIM + N_IDX_HEADS]
    row = i * tq + lax.broadcasted_iota(jnp.int32, (tq, wc), 0)
    col = lax.broadcasted_iota(jnp.int32, (tq, wc), 1)

    def score_chunk(kc, carry):
        r0 = pl.multiple_of(kc * wc, wc)
        ki_lo = kilo_ref[pl.ds(r0, wc), :]
        ki_hi = kihi_ref[pl.ds(r0, wc), :]
        score = jnp.zeros((tq, wc), F32)
        for pair in range(N_IDX_HEADS // 2):
            x = qi_ref[:, pair * LANES:(pair + 1) * LANES]
            score = score + wi[:, 2 * pair:2 * pair + 1] * jnp.maximum(_nt_dot(x, ki_lo), 0.0)
            score = score + wi[:, 2 * pair + 1:2 * pair + 2] * jnp.maximum(_nt_dot(x, ki_hi), 0.0)
        key_scr[kc] = jnp.where(kc * wc + col <= row, _sort_key(score * IDX_SCALE), INT_MIN)
        return carry

    lax.fori_loop(0, n_ch, score_chunk, 0)

    def search(it, res):
        trial = res + jnp.left_shift(jnp.int32(1), 31 - it)

        def count_chunk(kc, acc):
            hit = jnp.where(key_scr[kc] >= trial, 1.0, 0.0)
            for j in range(wc // LANES):
                acc = acc + hit[:, j * LANES:(j + 1) * LANES]
            return acc

        acc = lax.fori_loop(0, n_ch, count_chunk, jnp.zeros((tq, LANES), F32))
        return jnp.where(jnp.sum(acc, axis=1, keepdims=True) >= float(topk), trial, res)

    thr = lax.fori_loop(0, 32, search, jnp.full((tq, 1), INT_MIN, jnp.int32))

    def bias_chunk(kc, n_keep):
        key = key_scr[kc]
        keep = (key >= thr) & (key > KEY_NEG_INF)
        bias_scr[kc] = jnp.where(keep, 0.0, MASKED)
        return n_keep + _row_count(keep)

    n_keep = lax.fori_loop(0, n_ch, bias_chunk, jnp.zeros((tq, 1), F32))

    @pl.when(jnp.max(n_keep) > float(topk))
    def _():
        n_gt = lax.fori_loop(0, n_ch, lambda kc, n: n + _row_count(key_scr[kc] > thr), jnp.zeros((tq, 1), F32))
        need = float(topk) - n_gt
        upper = _strict_upper(wc)

        def tie_chunk(kc, before):
            key = key_scr[kc]
            eq = key == thr
            eqf = jnp.where(eq, 1.0, 0.0)
            rank = before + _dot(eqf.astype(BF16), upper)
            keep = ((key > thr) | (eq & (rank < need))) & (key > KEY_NEG_INF)
            bias_scr[kc] = jnp.where(keep, 0.0, MASKED)
            return before + jnp.sum(eqf, axis=1, keepdims=True)

        lax.fori_loop(0, n_ch, tie_chunk, jnp.zeros((tq, 1), F32))

    m_scr[...] = jnp.full(m_scr.shape, MASKED, F32)
    l_scr[...] = jnp.zeros(l_scr.shape, F32)
    acc_scr[...] = jnp.zeros(acc_scr.shape, F32)
    lane_blocks = [slice(j * LANES, (j + 1) * LANES) for j in range(wc // LANES)]

    def logit_chunk(kc, carry):
        r0 = pl.multiple_of(kc * wc, wc)
        bias = bias_scr[kc]
        for h in range(N_HEADS):
            g = h // GQA_GROUP
            kg = k_ref[pl.ds(r0, wc), g * HEAD_DIM:(g + 1) * HEAD_DIM]
            lg = _nt_dot(q_ref[:, h * HEAD_DIM:(h + 1) * HEAD_DIM], kg) * ATT_SCALE + bias
            lg_scr[h, kc] = lg
            mx = m_scr[h]
            for lb in lane_blocks:
                mx = jnp.maximum(mx, lg[:, lb])
            m_scr[h] = mx
        return carry

    lax.fori_loop(0, n_ch, logit_chunk, 0)
    for h in range(N_HEADS):
        m_scr[h] = jnp.broadcast_to(jnp.max(m_scr[h], axis=1, keepdims=True), (tq, LANES))

    def prob_chunk(kc, carry):
        r0 = pl.multiple_of(kc * wc, wc)
        for h in range(N_HEADS):
            g = h // GQA_GROUP
            sl = slice(h * HEAD_DIM, (h + 1) * HEAD_DIM)
            vg = v_ref[pl.ds(r0, wc), g * HEAD_DIM:(g + 1) * HEAD_DIM]
            lg = lg_scr[h, kc]
            mx = m_scr[h]
            ps = [jnp.exp(lg[:, lb] - mx) for lb in lane_blocks]
            den = l_scr[h]
            for p in ps:
                den = den + p
            l_scr[h] = den
            acc_scr[:, sl] = acc_scr[:, sl] + _dot(jnp.concatenate(ps, axis=1).astype(BF16), vg)
        return carry

    lax.fori_loop(0, n_ch, prob_chunk, 0)
    for h in range(N_HEADS):
        sl = slice(h * HEAD_DIM, (h + 1) * HEAD_DIM)
        den = jnp.sum(l_scr[h], axis=1, keepdims=True)
        o_ref[:, sl] = (acc_scr[:, sl] / den).astype(o_ref.dtype)


def _attn_prompt(q, qi, proj, ki_col, k, v, ki_lo, ki_hi, nb, s):
    tq = _pick(s, (256, 128))
    nq = s // tq
    topk = min(TOPK_MAX, s // 4)
    ki_blk = ki_col // LANES
    qrow = lambda w: pl.BlockSpec((tq, w), lambda b, i: (b * nq + i, 0))
    full = lambda w: pl.BlockSpec((s, w), lambda b, i: (b, 0))
    return pl.pallas_call(
        functools.partial(_attn_prompt_body, topk=topk),
        grid=(nb, nq),
        in_specs=[qrow(ATT_W), qrow(IDX_W), pl.BlockSpec((tq, LANES), lambda b, i: (b * nq + i, ki_blk)),
                  full(KV_W), full(KV_W), full(LANES), full(LANES)],
        out_specs=qrow(ATT_W),
        out_shape=jax.ShapeDtypeStruct((nb * s, ATT_W), BF16),
        scratch_shapes=[pltpu.VMEM((nq, tq, tq), jnp.int32), pltpu.VMEM((nq, tq, tq), F32),
                        pltpu.VMEM((N_HEADS, nq, tq, tq), F32),
                        pltpu.VMEM((N_HEADS, tq, LANES), F32), pltpu.VMEM((N_HEADS, tq, LANES), F32),
                        pltpu.VMEM((tq, ATT_W), F32)],
        compiler_params=_params(),
        name="attn_prompt",
    )(q, qi, proj, k, v, ki_lo, ki_hi)


def _s5_discretize_body(are_ref, aim_ref, ldt_ref, bre_ref, bim_ref, abre_ref, abim_ref, bbre_ref, bbim_ref):
    dt = jnp.exp(ldt_ref[...])
    lr, li = are_ref[...], aim_ref[...]
    mag = jnp.exp(lr * dt)
    ab_re, ab_im = mag * jnp.cos(li * dt), mag * jnp.sin(li * dt)
    zr, zi = ab_re - 1.0, ab_im
    den = lr * lr + li * li
    fr = (zr * lr + zi * li) / den
    fi = (zi * lr - zr * li) / den
    abre_ref[...] = ab_re
    abim_ref[...] = ab_im
    br, bi = bre_ref[...], bim_ref[...]
    bbre_ref[...] = fr * br - fi * bi
    bbim_ref[...] = fr * bi + fi * br


def _s5_discretize(a_re, a_im, log_dt, b_re, b_im):
    depth = a_re.shape[0]
    gp = pl.BlockSpec((None, SSM_GROUPS, 1, SSM_P), lambda l: (l, 0, 0, 0))
    gwp = pl.BlockSpec((None, SSM_GROUPS, SSM_GROUP, SSM_P), lambda l: (l, 0, 0, 0))
    bt = lambda b: jnp.swapaxes(b, 2, 3)
    a4 = lambda a: a.reshape(depth, SSM_GROUPS, 1, SSM_P)
    return pl.pallas_call(
        _s5_discretize_body,
        grid=(depth,),
        in_specs=[gp, gp, pl.BlockSpec((None, SSM_GROUPS, 1, 1), lambda l: (l, 0, 0, 0)), gwp, gwp],
        out_specs=[gp, gp, gwp, gwp],
        out_shape=[jax.ShapeDtypeStruct((depth, SSM_GROUPS, 1, SSM_P), F32)] * 2
        + [jax.ShapeDtypeStruct((depth, SSM_GROUPS, SSM_GROUP, SSM_P), F32)] * 2,
        compiler_params=_params(),
        name="s5_discretize",
    )(a4(a_re), a4(a_im), log_dt.reshape(depth, SSM_GROUPS, 1, 1), bt(b_re), bt(b_im))


def _block_diag_in(bb_t):
    depth = bb_t.shape[0]
    gpb = SSM_GROUPS // SSM_BLOCKS
    x = bb_t.reshape(depth, SSM_BLOCKS, gpb, SSM_GROUP, SSM_P)
    eye = jnp.eye(gpb, dtype=bb_t.dtype)
    out = jnp.einsum('lkgwp,gh->lkgwhp', x, eye)
    return out.reshape(depth, SSM_BLOCKS, gpb * SSM_GROUP, gpb * SSM_P).astype(BF16)


def _block_diag_out(c):
    depth = c.shape[0]
    gpb = SSM_GROUPS // SSM_BLOCKS
    x = c.reshape(depth, SSM_BLOCKS, gpb, SSM_GROUP, SSM_P)
    eye = jnp.eye(gpb, dtype=c.dtype)
    out = jnp.einsum('lkgwp,gh->lkgphw', x, eye)
    return out.reshape(depth, SSM_BLOCKS, gpb * SSM_P, gpb * SSM_GROUP).astype(BF16)


def _s5_body(u_ref, bbre_ref, bbim_ref, cre_ref, cim_ref, abre_ref, abim_ref, d_ref, h0re_ref, h0im_ref,
             g_ref, htre_ref, htim_ref, bure, buim, hre, him, ust, gst, *, nb):
    c = pl.program_id(0)
    rows = bure.shape[0]
    n_t = rows // nb
    per_tile = SUBLANES // nb

    @pl.when(c == 0)
    def _():
        hre[...] = h0re_ref[...]
        him[...] = h0im_ref[...]

    for k in range(SSM_BLOCKS):
        ln = slice(k * LANES, (k + 1) * LANES)
        for b in range(nb):
            if n_t == 1:
                ust[k, b:b + 1, :] = u_ref[b, :, ln]
            else:
                ust[k, pl.ds(b, n_t, stride=nb), :] = u_ref[b, :, ln]
        uk = ust[k].astype(BF16)
        st = slice(k * SSM_BLOCK_STATE, (k + 1) * SSM_BLOCK_STATE)
        bure[:, st] = _dot(uk, bbre_ref[k])
        buim[:, st] = _dot(uk, bbim_ref[k])

    for cc in range(SSM_STATE // SCAN_LANES):
        sl = slice(cc * SCAN_LANES, (cc + 1) * SCAN_LANES)
        ar = jnp.broadcast_to(abre_ref[:, sl], (nb, SCAN_LANES))
        ai = jnp.broadcast_to(abim_ref[:, sl], (nb, SCAN_LANES))

        def step(j, carry, sl=sl, ar=ar, ai=ai):
            hr, hi = carry
            r0 = pl.multiple_of(j * SUBLANES, SUBLANES)
            xr = bure[pl.ds(r0, SUBLANES), sl]
            xi = buim[pl.ds(r0, SUBLANES), sl]
            out_r, out_i = [], []
            for t in range(per_tile):
                nr = ar * hr - ai * hi + xr[t * nb:(t + 1) * nb]
                ni = ar * hi + ai * hr + xi[t * nb:(t + 1) * nb]
                hr, hi = nr, ni
                out_r.append(hr)
                out_i.append(hi)
            bure[pl.ds(r0, SUBLANES), sl] = out_r[0] if per_tile == 1 else jnp.concatenate(out_r, axis=0)
            buim[pl.ds(r0, SUBLANES), sl] = out_i[0] if per_tile == 1 else jnp.concatenate(out_i, axis=0)
            return hr, hi

        hr, hi = lax.fori_loop(0, rows // SUBLANES, step, (hre[:, sl], him[:, sl]))
        hre[:, sl] = hr
        him[:, sl] = hi

    for k in range(SSM_BLOCKS):
        st = slice(k * SSM_BLOCK_STATE, (k + 1) * SSM_BLOCK_STATE)
        ln = slice(k * LANES, (k + 1) * LANES)
        y = _dot(bure[:, st].astype(BF16), cre_ref[k]) - _dot(buim[:, st].astype(BF16), cim_ref[k])
        y = y + d_ref[:, ln] * ust[k]
        gst[k] = _gelu_tanh(y)
        for b in range(nb):
            if n_t == 1:
                g_ref[b, :, ln] = gst[k, b:b + 1, :]
            else:
                g_ref[b, :, ln] = gst[k, pl.ds(b, n_t, stride=nb), :]

    @pl.when(c == pl.num_programs(0) - 1)
    def _():
        htre_ref[...] = hre[...]
        htim_ref[...] = him[...]


def _s5(proj, nb, s, l, bb_re, bb_im, c_re, c_im, ab_re, ab_im, d, h0_re, h0_im):
    n_t = min(s, 512 // nb)
    rows = n_t * nb
    blk4 = lambda a: pl.BlockSpec((None,) + a.shape[1:], lambda c: (l, 0, 0, 0))
    full2 = lambda a: pl.BlockSpec(a.shape, lambda c: (0, 0))
    vec = lambda a: a.reshape(1, -1)
    ab_re, ab_im, d = vec(ab_re), vec(ab_im), vec(d)
    u_spec = pl.BlockSpec((nb, n_t, SSM_W), lambda c: (0, c, 0))
    g, h_re, h_im = pl.pallas_call(
        functools.partial(_s5_body, nb=nb),
        grid=(s // n_t,),
        in_specs=[u_spec, blk4(bb_re), blk4(bb_im), blk4(c_re), blk4(c_im),
                  full2(ab_re), full2(ab_im), full2(d), full2(h0_re), full2(h0_im)],
        out_specs=[u_spec, full2(h0_re), full2(h0_im)],
        out_shape=[jax.ShapeDtypeStruct((nb, s, SSM_W), F32),
                   jax.ShapeDtypeStruct(h0_re.shape, F32), jax.ShapeDtypeStruct(h0_im.shape, F32)],
        scratch_shapes=[pltpu.VMEM((rows, SSM_STATE), F32), pltpu.VMEM((rows, SSM_STATE), F32),
                        pltpu.VMEM((nb, SSM_STATE), F32), pltpu.VMEM((nb, SSM_STATE), F32),
                        pltpu.VMEM((SSM_BLOCKS, rows, LANES), F32), pltpu.VMEM((SSM_BLOCKS, rows, LANES), F32)],
        compiler_params=_params(),
        name="s5_scan",
    )(proj.reshape(nb, s, proj.shape[1]), bb_re, bb_im, c_re, c_im, ab_re, ab_im, d, h0_re, h0_im)
    return g.reshape(nb * s, SSM_W), h_re, h_im


IDX_PAGES_PER_STEP = 16


def _sample_index_body(pt_ref, qi_ref, wi_ref, kin_ref, *rest, n_pg, topk):
    del pt_ref
    pages = rest[:n_pg]
    idx_ref, nsel_ref, bnew_ref, sc_scr, rank_scr = rest[n_pg:]
    c = pl.program_id(1)
    n_pages = sc_scr.shape[0]
    qi = qi_ref[...].astype(BF16)
    wi = wi_ref[...]

    def weighted(d):
        return jnp.sum(wi * jnp.maximum(d, 0.0), axis=0, keepdims=True) * IDX_SCALE

    for p in range(n_pg):
        sc_scr[pl.ds(c * n_pg + p, 1), :] = weighted(_dot(qi, pages[p][...].astype(BF16)))

    @pl.when(c == pl.num_programs(1) - 1)
    def _():
        kin = kin_ref[...].astype(BF16).astype(F32)
        d_new = jnp.sum(qi.astype(F32) * kin, axis=1, keepdims=True)
        key_new = _sort_key(jnp.broadcast_to(weighted(d_new), (1, LANES)))
        key = _sort_key(sc_scr[...])

        def count(mask, mask_new):
            per_lane = jnp.sum(jnp.where(mask, 1.0, 0.0), axis=0, keepdims=True)
            return jnp.sum(per_lane, axis=1, keepdims=True) + jnp.where(mask_new, 1.0, 0.0)

        def body(it, res):
            trial = res + jnp.left_shift(jnp.int32(1), 31 - it)
            return jnp.where(count(key >= trial, key_new >= trial) >= float(topk), trial, res)

        thr = lax.fori_loop(0, 32, body, jnp.full((1, LANES), INT_MIN, jnp.int32))
        need = float(topk) - count(key > thr, key_new > thr)
        eq = key == thr
        eqf = jnp.where(eq, 1.0, 0.0).astype(BF16)
        in_page = _dot(eqf, _strict_upper(LANES))
        r = lax.broadcasted_iota(jnp.int32, (n_pages, n_pages), 0)
        cidx = lax.broadcasted_iota(jnp.int32, (n_pages, n_pages), 1)
        lower = jnp.where(cidx < r, 1.0, 0.0).astype(BF16)
        per_page = jnp.sum(jnp.where(eq, 1.0, 0.0), axis=1, keepdims=True)
        earlier = _dot(lower, jnp.broadcast_to(per_page, (n_pages, LANES)).astype(BF16))
        keep = ((key > thr) | (eq & (in_page + earlier < need))) & (key > KEY_NEG_INF)
        n_eq_past = jnp.sum(jnp.sum(jnp.where(eq, 1.0, 0.0), axis=0, keepdims=True), axis=1, keepdims=True)
        keep_new = ((key_new > thr) | ((key_new == thr) & (n_eq_past < need))) & (key_new > KEY_NEG_INF)
        bnew_ref[...] = jnp.where(keep_new, 0.0, MASKED)

        keepf = jnp.where(keep, 1.0, 0.0)
        kept_per_page = jnp.sum(keepf, axis=1, keepdims=True)
        slot_of = (_dot(keepf.astype(BF16), _strict_upper(LANES))
                   + _dot(lower, jnp.broadcast_to(kept_per_page, (n_pages, LANES)).astype(BF16)))
        rank_scr[...] = jnp.where(keep, slot_of, -1.0)
        slot = lax.broadcasted_iota(jnp.int32, (topk, LANES), 0).astype(F32)
        lane = lax.broadcasted_iota(jnp.int32, (topk, LANES), 1)

        def place_page(p, acc):
            return acc + jnp.where(rank_scr[pl.ds(p, 1), :] == slot, p * PAGE_SIZE + lane, 0)

        acc = lax.fori_loop(0, n_pages, place_page, jnp.zeros((topk, LANES), jnp.int32))
        idx_ref[...] = jnp.sum(acc.astype(F32), axis=1, keepdims=True).astype(jnp.int32)
        n_kept = jnp.sum(jnp.sum(keepf, axis=0, keepdims=True), axis=1, keepdims=True)
        nsel_ref[...] = jnp.broadcast_to(n_kept, (1, LANES)).astype(jnp.int32)


def _sample_index(page_table, cache_kidx_t, layer, qi, wi, ki_new):
    nb, n_pages = page_table.shape
    n_pg = _pick(n_pages, (IDX_PAGES_PER_STEP, 8, 4, 2, 1))
    topk = min(TOPK_MAX, (n_pages * PAGE_SIZE + 1) // 4)
    page_spec = lambda p: pl.BlockSpec((None, None, IDX_DIM, PAGE_SIZE),
                                       lambda b, c, pt: (layer, pt[b, c * n_pg + p], 0, 0))
    per_b = lambda shape: pl.BlockSpec((None,) + shape, lambda b, c, pt: (b, 0, 0))
    grid_spec = pltpu.PrefetchScalarGridSpec(
        num_scalar_prefetch=1,
        grid=(nb, n_pages // n_pg),
        in_specs=[per_b((N_IDX_HEADS, IDX_DIM)), per_b((N_IDX_HEADS, 1)), per_b((1, IDX_DIM))]
        + [page_spec(p) for p in range(n_pg)],
        out_specs=[per_b((topk, 1)), per_b((1, LANES)), per_b((1, LANES))],
        scratch_shapes=[pltpu.VMEM((n_pages, LANES), F32), pltpu.VMEM((n_pages, LANES), F32)],
    )
    idx, n_sel, bias_new = pl.pallas_call(
        functools.partial(_sample_index_body, n_pg=n_pg, topk=topk),
        grid_spec=grid_spec,
        out_shape=[jax.ShapeDtypeStruct((nb, topk, 1), jnp.int32), jax.ShapeDtypeStruct((nb, 1, LANES), jnp.int32),
                   jax.ShapeDtypeStruct((nb, 1, LANES), F32)],
        compiler_params=_params(),
        name="sample_index",
    )(page_table, qi.reshape(nb, N_IDX_HEADS, IDX_DIM), wi.reshape(nb, N_IDX_HEADS, 1),
      ki_new.reshape(nb, 1, IDX_DIM), *([cache_kidx_t] * n_pg))
    return idx.reshape(nb, topk), n_sel[:, 0, 0], bias_new


SLOTS_PER_TILE = SUBLANES // N_KV_HEADS


def _group_rows(n_cols):
    head = lax.broadcasted_iota(jnp.int32, (N_HEADS, n_cols), 0)
    return [(head >= g * GQA_GROUP) & (head < (g + 1) * GQA_GROUP) for g in range(N_KV_HEADS)]


def _sample_attn_body(pt_ref, idx_ref, nsel_ref, q_ref, kn_ref, vn_ref, bnew_ref, odd_ref, ck_hbm, cv_hbm,
                      o_ref, kbuf, vbuf, sem, *, layer, topk):
    b = pl.program_id(0)

    def tile_copy(src_hbm, buf, which, phys, src_row, dst_row):
        return pltpu.make_async_copy(src_hbm.at[layer, phys, pl.ds(src_row, SUBLANES), :],
                                     buf.at[pl.ds(dst_row, SUBLANES), :], sem.at[which])

    def issue(r, carry):
        idx = idx_ref[b, r]
        phys = pt_ref[b, idx // PAGE_SIZE]
        src_row = pl.multiple_of(((idx % PAGE_SIZE) // SLOTS_PER_TILE) * SUBLANES, SUBLANES)
        dst_row = pl.multiple_of(r * SUBLANES, SUBLANES)
        tile_copy(ck_hbm, kbuf, 0, phys, src_row, dst_row).start()
        tile_copy(cv_hbm, vbuf, 1, phys, src_row, dst_row).start()
        return carry

    lax.fori_loop(0, topk, issue, 0)

    def drain(r, carry):
        tile_copy(ck_hbm, kbuf, 0, 0, 0, 0).wait()
        tile_copy(cv_hbm, vbuf, 1, 0, 0, 0).wait()
        return carry

    lax.fori_loop(0, topk, drain, 0)

    q = q_ref[...].astype(BF16)
    odd = odd_ref[...] == 1
    live = lax.broadcasted_iota(jnp.int32, (1, topk), 1) < nsel_ref[b]

    def gathered(buf, g):
        first = buf[pl.ds(g, topk, stride=SUBLANES), :]
        second = buf[pl.ds(N_KV_HEADS + g, topk, stride=SUBLANES), :]
        return jnp.where(odd, second, first).astype(BF16)

    in_group = _group_rows(topk)
    lg = jnp.zeros((N_HEADS, topk), F32)
    for g in range(N_KV_HEADS):
        lg = jnp.where(in_group[g], _nt_dot(q, gathered(kbuf, g)), lg)
    lg = jnp.where(live, lg * ATT_SCALE, MASKED)
    qf = q.astype(F32)
    kn = kn_ref[...].astype(BF16).astype(F32)
    vn = vn_ref[...].astype(BF16).astype(F32)
    lg_new = jnp.sum(qf * kn, axis=1, keepdims=True) * ATT_SCALE + bnew_ref[:, :1]
    mx = jnp.maximum(jnp.max(lg, axis=1, keepdims=True), lg_new)
    pr = jnp.where(live, jnp.exp(lg - mx), 0.0)
    pr_new = jnp.where(bnew_ref[:, :1] == 0.0, jnp.exp(lg_new - mx), 0.0)
    den = jnp.sum(pr, axis=1, keepdims=True) + pr_new
    prb = pr.astype(BF16)
    in_group_d = _group_rows(HEAD_DIM)
    pv = jnp.zeros((N_HEADS, HEAD_DIM), F32)
    for g in range(N_KV_HEADS):
        pv = jnp.where(in_group_d[g], _dot(prb, gathered(vbuf, g)), pv)
    o_ref[...] = ((pv + pr_new.astype(BF16).astype(F32) * vn) / den).astype(o_ref.dtype)


def _sample_attn(page_table, cache_k2, cache_v2, layer, idx, n_sel, bias_new, q, k_new, v_new):
    nb = page_table.shape[0]
    topk = idx.shape[1]
    per_b = lambda shape: pl.BlockSpec((None,) + shape, lambda b, pt, ix, ns: (b, 0, 0))
    per_head = lambda a: jnp.repeat(a.reshape(nb, N_KV_HEADS, HEAD_DIM), GQA_GROUP, axis=1)
    odd = (idx % SLOTS_PER_TILE).reshape(nb, topk, 1)
    grid_spec = pltpu.PrefetchScalarGridSpec(
        num_scalar_prefetch=3,
        grid=(nb,),
        in_specs=[per_b((N_HEADS, HEAD_DIM)), per_b((N_HEADS, HEAD_DIM)), per_b((N_HEADS, HEAD_DIM)),
                  per_b((1, LANES)), per_b((topk, 1)),
                  pl.BlockSpec(memory_space=pl.ANY), pl.BlockSpec(memory_space=pl.ANY)],
        out_specs=per_b((N_HEADS, HEAD_DIM)),
        scratch_shapes=[pltpu.VMEM((topk * SUBLANES, HEAD_DIM), F32), pltpu.VMEM((topk * SUBLANES, HEAD_DIM), F32),
                        pltpu.SemaphoreType.DMA((2,))],
    )
    out = pl.pallas_call(
        functools.partial(_sample_attn_body, layer=layer, topk=topk),
        grid_spec=grid_spec,
        out_shape=jax.ShapeDtypeStruct((nb, N_HEADS, HEAD_DIM), F32),
        compiler_params=_params(),
        name="sample_attn",
    )(page_table, idx, n_sel, q.reshape(nb, N_HEADS, HEAD_DIM), per_head(k_new), per_head(v_new), bias_new, odd,
      cache_k2, cache_v2)
    return out.reshape(nb, ATT_W)


def _layer_tail(proj, g, y_att, x, w, l, act_dtype):
    z = _glu(g, w["w_glu"], l, act_dtype)
    merged = _merge(z, y_att, w["w_branch"], l, proj, act_dtype)
    x1, hn = _out_norm(merged, w["w_out"], l, x, w["norm_ffn"][l], act_dtype)
    act = _ffn_up(hn, w["w_ffn_up"], l, act_dtype)
    return _ffn_down(act, w["w_ffn_down"], l, x1)


def _s5_layer(proj, nb, s, w, l, h0_re, h0_im):
    return _s5(proj, nb, s, l, w["bb_re"], w["bb_im"], w["c_re"], w["c_im"], w["ab_re"][l], w["ab_im"][l],
               w["ssm_d"][l], h0_re, h0_im)


def _prompt_layer(x, nb, s, tabs, w, l):
    xn = _rmsnorm(x, w["norm_mix"][l], BF16)
    proj = _in_proj(xn, w["w_in"], l, PROJ_TN, F32)
    q, k_out, k_bf, v_out, v_bf, qi, ki_out, ki_lo, ki_hi = _qk_post(
        proj, w["ki_col"], tabs, s, w["q_norm"][l], w["k_norm"][l], BF16)
    y_att = _attn_prompt(q, qi, proj, w["ki_col"], k_bf, v_bf, ki_lo, ki_hi, nb, s)
    zeros = jnp.zeros((nb, SSM_STATE), F32)
    g, h_re, h_im = _s5_layer(proj, nb, s, w, l, zeros, zeros)
    x2 = _layer_tail(proj, g, y_att, x, w, l, BF16)
    return x2, k_out, v_out, ki_out, h_re, h_im


def _sample_layer(x, nb, tabs, w, l, page_table, cache_k2, cache_v2, cache_kidx_t, h0_re, h0_im):
    xn = _rmsnorm(x, w["norm_mix"][l], F32)
    proj = _in_proj(xn, w["w_in"], l, PROJ_TN, F32)
    ki_col = w["ki_col"]
    q, k_out, _, v_out, _, qi, ki_out, _, _ = _qk_post(proj, ki_col, tabs, nb, w["q_norm"][l], w["k_norm"][l], F32)
    wi = proj[:, ki_col + IDX_DIM:ki_col + IDX_DIM + N_IDX_HEADS]
    idx, n_sel, bias_new = _sample_index(page_table, cache_kidx_t, l, qi, wi, ki_out)
    y_att = _sample_attn(page_table, cache_k2, cache_v2, l, idx, n_sel, bias_new, q, k_out, v_out)
    g, h_re, h_im = _s5_layer(proj, nb, 1, w, l, h0_re, h0_im)
    x2 = _layer_tail(proj, g, y_att, x, w, l, F32)
    return x2, k_out, v_out, ki_out, h_re, h_im


def kernel(x_prompt, x_sample, cache_k, cache_v, cache_kidx, state_ssm_re, state_ssm_im, page_table, norm_mix, w_in, q_norm, k_norm, ssm_a_re, ssm_a_im, ssm_log_dt, ssm_b_re, ssm_b_im, ssm_c_re, ssm_c_im, ssm_d, w_glu, w_branch, w_out, norm_ffn, w_ffn_up, w_ffn_down):
    nb_p, s, d_model = x_prompt.shape
    nb_s, dec_seq, _ = x_sample.shape
    assert dec_seq == 1, "the sample group decodes one token per sequence"
    depth = w_in.shape[0]
    past = page_table.shape[1] * PAGE_SIZE

    ki_orig = COL_GATE
    gate_orig = ki_orig + IDX_DIM + N_IDX_HEADS
    n_gate = w_in.shape[2] - gate_orig
    ki_col = COL_GATE + n_gate
    assert ki_col % LANES == 0
    cols = ki_col + LANES
    cols_pad = -(-cols // PROJ_TN) * PROJ_TN
    w_in_r = jnp.concatenate(
        [w_in[:, :, :COL_GATE], w_in[:, :, gate_orig:], w_in[:, :, ki_orig:gate_orig],
         jnp.zeros((depth, d_model, cols_pad - ki_col - IDX_DIM - N_IDX_HEADS), w_in.dtype)], axis=2).astype(BF16)

    ab_re, ab_im, bbt_re, bbt_im = _s5_discretize(ssm_a_re, ssm_a_im, ssm_log_dt, ssm_b_re, ssm_b_im)
    bb_re, bb_im = _block_diag_in(bbt_re), _block_diag_in(bbt_im)
    c_re, c_im = _block_diag_out(ssm_c_re), _block_diag_out(ssm_c_im)

    tabs_p = _rope_tables(jnp.arange(s, dtype=jnp.int32))
    tabs_s = _rope_tables(jnp.full((nb_s,), past, dtype=jnp.int32))

    xp = x_prompt.reshape(nb_p * s, d_model)
    xs = x_sample.reshape(nb_s, d_model)
    cache_k2 = cache_k.reshape(cache_k.shape[:2] + (PAGE_SIZE * N_KV_HEADS, HEAD_DIM))
    cache_v2 = cache_v.reshape(cache_v.shape[:2] + (PAGE_SIZE * N_KV_HEADS, HEAD_DIM))
    cache_kidx_t = jnp.swapaxes(cache_kidx, 2, 3)
    w = dict(ki_col=ki_col, norm_mix=norm_mix, w_in=w_in_r, q_norm=q_norm, k_norm=k_norm,
             bb_re=bb_re, bb_im=bb_im, c_re=c_re, c_im=c_im, ab_re=ab_re, ab_im=ab_im, ssm_d=ssm_d,
             w_glu=w_glu, w_branch=w_branch, w_out=w_out, norm_ffn=norm_ffn,
             w_ffn_up=w_ffn_up, w_ffn_down=w_ffn_down)
    outs_p, outs_s = [], []
    for l in range(depth):
        xp, *rest_p = _prompt_layer(xp, nb_p, s, tabs_p, w, l)
        xs, *rest_s = _sample_layer(xs, nb_s, tabs_s, w, l, page_table, cache_k2, cache_v2, cache_kidx_t,
                                    state_ssm_re[l].reshape(nb_s, SSM_STATE), state_ssm_im[l].reshape(nb_s, SSM_STATE))
        outs_p.append(rest_p)
        outs_s.append(rest_s)

    def stack(outs, idx, shape):
        return jnp.stack([o[idx].reshape(shape) for o in outs])

    kv_p, kv_s = (nb_p, s, N_KV_HEADS, HEAD_DIM), (nb_s, dec_seq, N_KV_HEADS, HEAD_DIM)
    st_p, st_s = (nb_p, SSM_GROUPS, SSM_P), (nb_s, SSM_GROUPS, SSM_P)
    return (xp.reshape(nb_p, s, d_model), xs.reshape(nb_s, dec_seq, d_model),
            stack(outs_p, 0, kv_p), stack(outs_p, 1, kv_p), stack(outs_p, 2, (nb_p, s, IDX_DIM)),
            stack(outs_p, 3, st_p), stack(outs_p, 4, st_p),
            stack(outs_s, 0, kv_s), stack(outs_s, 1, kv_s), stack(outs_s, 2, (nb_s, dec_seq, IDX_DIM)),
            stack(outs_s, 3, st_s), stack(outs_s, 4, st_s))
```

```python
import functools
import math

import jax
import jax.numpy as jnp
from jax import lax
from jax.experimental import pallas as pl
from jax.experimental.pallas import tpu as pltpu

F32 = jnp.float32
BF16 = jnp.bfloat16

SSM_W = 1024
SSM_GROUP = 16
SSM_GROUPS = SSM_W // SSM_GROUP
SSM_P = 64
SSM_STATE = SSM_GROUPS * SSM_P
N_HEADS = 8
N_KV_HEADS = 4
HEAD_DIM = 128
GQA_GROUP = N_HEADS // N_KV_HEADS
ATT_W = N_HEADS * HEAD_DIM
KV_W = N_KV_HEADS * HEAD_DIM
N_IDX_HEADS = 16
IDX_DIM = 64
IDX_W = N_IDX_HEADS * IDX_DIM
IDX_SCALE = (IDX_DIM * N_IDX_HEADS) ** -0.5
ATT_SCALE = HEAD_DIM ** -0.5
TOPK_MAX = 256
PAGE_SIZE = 128
ROPE_THETA = 10000.0
EPS = 1e-6

LANES = 128
SUBLANES = 8
VMEM_LIMIT = 56 * 1024 * 1024

COL_U = 0
COL_Q = SSM_W
COL_K = COL_Q + ATT_W
COL_V = COL_K + KV_W
COL_QI = COL_V + KV_W
COL_GATE = COL_QI + IDX_W
PROJ_TN = 768

MASKED = -1e30
INT_MIN = -(2 ** 31)
KEY_NEG_INF = -2139095041

SCAN_LANES = 512
SSM_BLOCKS = SSM_W // LANES
SSM_BLOCK_STATE = SSM_STATE // SSM_BLOCKS


def _params(**kw):
    return pltpu.CompilerParams(vmem_limit_bytes=VMEM_LIMIT, **kw)


def _pick(n, cands):
    for c in cands:
        if n % c == 0:
            return c
    return n


def _nt_dot(a, b):
    return lax.dot_general(a, b, (((1,), (1,)), ((), ())), preferred_element_type=F32)


def _dot(a, b):
    return jnp.dot(a, b, preferred_element_type=F32)


def _sigmoid(x):
    return 1.0 / (1.0 + jnp.exp(-x))


def _gelu_tanh(x):
    c = math.sqrt(2.0 / math.pi)
    return 0.5 * x * (1.0 + jnp.tanh(c * (x + 0.044715 * (x * x * x))))


def _sort_key(s):
    bits = pltpu.bitcast(s, jnp.int32)
    return bits ^ ((bits >> 31) & jnp.int32(0x7FFFFFFF))


def _rmsnorm_body(x_ref, g_ref, o_ref):
    x = x_ref[...].astype(F32)
    ms = jnp.mean(x * x, axis=-1, keepdims=True)
    o_ref[...] = ((x * lax.rsqrt(ms + EPS)) * g_ref[...]).astype(o_ref.dtype)


def _rmsnorm(x, g, out_dtype):
    m, d = x.shape
    tm = _pick(m, (512, 256, 128, 8))
    return pl.pallas_call(
        _rmsnorm_body,
        grid=(m // tm,),
        in_specs=[pl.BlockSpec((tm, d), lambda i: (i, 0)), pl.BlockSpec((1, d), lambda i: (0, 0))],
        out_specs=pl.BlockSpec((tm, d), lambda i: (i, 0)),
        out_shape=jax.ShapeDtypeStruct((m, d), out_dtype),
        compiler_params=_params(),
        name="rmsnorm",
    )(x, g.reshape(1, d))


def _mm_body(a_ref, w_ref, o_ref):
    o_ref[...] = _dot(a_ref[...].astype(BF16), w_ref[...]).astype(o_ref.dtype)


def _in_proj(a, w, l, tn, out_dtype):
    m, k = a.shape
    n = w.shape[2]
    tm = _pick(m, (1024, 512, 256, 128, 8))
    return pl.pallas_call(
        _mm_body,
        grid=(n // tn, m // tm),
        in_specs=[pl.BlockSpec((tm, k), lambda j, i: (i, 0)), pl.BlockSpec((None, k, tn), lambda j, i: (l, 0, j))],
        out_specs=pl.BlockSpec((tm, tn), lambda j, i: (i, j)),
        out_shape=jax.ShapeDtypeStruct((m, n), out_dtype),
        compiler_params=_params(),
        name="in_proj",
    )(a, w)


def _cast_weight_once(step, w_ref, wb_scr):
    @pl.when(step == 0)
    def _():
        wb_scr[...] = w_ref[...].astype(BF16)


def _glu_body(g_ref, w_ref, o_ref, wb):
    _cast_weight_once(pl.program_id(0), w_ref, wb)
    g = g_ref[...]
    z = _dot(g.astype(BF16), wb[...])
    o_ref[...] = (g * _sigmoid(z)).astype(o_ref.dtype)


def _glu(g, w, l, out_dtype):
    m, n = g.shape
    tm = _pick(m, (512, 256, 128, 8))
    return pl.pallas_call(
        _glu_body,
        grid=(m // tm,),
        in_specs=[pl.BlockSpec((tm, n), lambda i: (i, 0)),
                  pl.BlockSpec((None, n, n), lambda i: (l, 0, 0), pipeline_mode=pl.Buffered(1))],
        out_specs=pl.BlockSpec((tm, n), lambda i: (i, 0)),
        out_shape=jax.ShapeDtypeStruct((m, n), out_dtype),
        scratch_shapes=[pltpu.VMEM((n, n), BF16)],
        compiler_params=_params(),
        name="glu",
    )(g, w)


def _merge_body(a0_ref, a1_ref, w0_ref, w1_ref, g0_ref, g1_ref, o_ref, wb0, wb1):
    _cast_weight_once(pl.program_id(1), w0_ref, wb0)
    _cast_weight_once(pl.program_id(1), w1_ref, wb1)
    y0 = _dot(a0_ref[...].astype(BF16), wb0[...])
    y1 = _dot(a1_ref[...].astype(BF16), wb1[...])
    o_ref[...] = (_sigmoid(g0_ref[...]) * y0 + _sigmoid(g1_ref[...]) * y1).astype(o_ref.dtype)


def _merge(a0, a1, w_branch, l, proj, out_dtype):
    m, k = a0.shape
    n = w_branch.shape[3]
    tm = _pick(m, (512, 256, 128, 8))
    tn = _pick(n, (1024, 512, 256, 128))
    g0_blk = COL_GATE // tn
    g1_blk = (COL_GATE + n) // tn
    wspec = lambda br: pl.BlockSpec((None, None, k, tn), lambda j, i: (l, br, 0, j))
    return pl.pallas_call(
        _merge_body,
        grid=(n // tn, m // tm),
        in_specs=[
            pl.BlockSpec((tm, k), lambda j, i: (i, 0)),
            pl.BlockSpec((tm, k), lambda j, i: (i, 0)),
            wspec(0), wspec(1),
            pl.BlockSpec((tm, tn), lambda j, i: (i, g0_blk + j)),
            pl.BlockSpec((tm, tn), lambda j, i: (i, g1_blk + j)),
        ],
        out_specs=pl.BlockSpec((tm, tn), lambda j, i: (i, j)),
        out_shape=jax.ShapeDtypeStruct((m, n), out_dtype),
        scratch_shapes=[pltpu.VMEM((k, tn), BF16), pltpu.VMEM((k, tn), BF16)],
        compiler_params=_params(),
        name="merge",
    )(a0, a1, w_branch, w_branch, proj, proj)


def _out_norm_body(a_ref, w_ref, x_ref, g_ref, x1_ref, hn_ref, wb):
    _cast_weight_once(pl.program_id(0), w_ref, wb)
    x1 = x_ref[...] + _dot(a_ref[...].astype(BF16), wb[...])
    x1_ref[...] = x1
    ms = jnp.mean(x1 * x1, axis=-1, keepdims=True)
    hn_ref[...] = ((x1 * lax.rsqrt(ms + EPS)) * g_ref[...]).astype(hn_ref.dtype)


def _out_norm(a, w, l, x, g, hn_dtype):
    m, k = a.shape
    n = w.shape[2]
    tm = _pick(m, (256, 128, 8))
    return pl.pallas_call(
        _out_norm_body,
        grid=(m // tm,),
        in_specs=[
            pl.BlockSpec((tm, k), lambda i: (i, 0)),
            pl.BlockSpec((None, k, n), lambda i: (l, 0, 0), pipeline_mode=pl.Buffered(1)),
            pl.BlockSpec((tm, n), lambda i: (i, 0)),
            pl.BlockSpec((1, n), lambda i: (0, 0)),
        ],
        out_specs=[pl.BlockSpec((tm, n), lambda i: (i, 0)), pl.BlockSpec((tm, n), lambda i: (i, 0))],
        out_shape=[jax.ShapeDtypeStruct((m, n), F32), jax.ShapeDtypeStruct((m, n), hn_dtype)],
        scratch_shapes=[pltpu.VMEM((k, n), BF16)],
        compiler_params=_params(),
        name="out_proj_norm",
    )(a, w, x, g.reshape(1, n))


def _ffn_up_body(a_ref, wg_ref, wu_ref, o_ref, wgb, wub):
    _cast_weight_once(pl.program_id(1), wg_ref, wgb)
    _cast_weight_once(pl.program_id(1), wu_ref, wub)
    a = a_ref[...].astype(BF16)
    gate = _dot(a, wgb[...])
    up = _dot(a, wub[...])
    o_ref[...] = ((gate * _sigmoid(gate)) * up).astype(o_ref.dtype)


def _ffn_up(a, w_up, l, out_dtype):
    m, k = a.shape
    d_ff = w_up.shape[2] // 2
    tm = _pick(m, (1024, 512, 256, 128, 8))
    tn = _pick(d_ff, (512, 256, 128))
    up_blk = d_ff // tn
    return pl.pallas_call(
        _ffn_up_body,
        grid=(d_ff // tn, m // tm),
        in_specs=[
            pl.BlockSpec((tm, k), lambda j, i: (i, 0)),
            pl.BlockSpec((None, k, tn), lambda j, i: (l, 0, j)),
            pl.BlockSpec((None, k, tn), lambda j, i: (l, 0, up_blk + j)),
        ],
        out_specs=pl.BlockSpec((tm, tn), lambda j, i: (i, j)),
        out_shape=jax.ShapeDtypeStruct((m, d_ff), out_dtype),
        scratch_shapes=[pltpu.VMEM((k, tn), BF16), pltpu.VMEM((k, tn), BF16)],
        compiler_params=_params(),
        name="ffn_up",
    )(a, w_up, w_up)


def _ffn_down_body(a_ref, w_ref, x_ref, o_ref, wb):
    _cast_weight_once(pl.program_id(1), w_ref, wb)
    o_ref[...] = x_ref[...] + _dot(a_ref[...].astype(BF16), wb[...])


def _ffn_down(a, w, l, x):
    m, k = a.shape
    n = w.shape[2]
    tm = _pick(m, (256, 128, 8))
    tn = _pick(n, (1024, 512, 256, 128))
    return pl.pallas_call(
        _ffn_down_body,
        grid=(n // tn, m // tm),
        in_specs=[
            pl.BlockSpec((tm, k), lambda j, i: (i, 0)),
            pl.BlockSpec((None, k, tn), lambda j, i: (l, 0, j), pipeline_mode=pl.Buffered(1)),
            pl.BlockSpec((tm, tn), lambda j, i: (i, j)),
        ],
        out_specs=pl.BlockSpec((tm, tn), lambda j, i: (i, j)),
        out_shape=jax.ShapeDtypeStruct((m, n), F32),
        scratch_shapes=[pltpu.VMEM((k, tn), BF16)],
        compiler_params=_params(),
        name="ffn_down",
    )(a, w, x)


def _rope_tables(pos):
    def tables(dim):
        half = dim // 2
        freqs = ROPE_THETA ** (-jnp.arange(half, dtype=F32) / half)
        ang = pos.astype(F32)[:, None] * freqs[None, :]
        cos, sin = jnp.cos(ang), jnp.sin(ang)
        reps = LANES // dim
        cos_t = jnp.tile(jnp.concatenate([cos, cos], axis=1), (1, reps))
        sin_t = jnp.tile(jnp.concatenate([-sin, sin], axis=1), (1, reps))
        return cos_t, sin_t
    return tables(HEAD_DIM) + tables(IDX_DIM)


def _rope_head(x, cos_t, sin_t):
    return x * cos_t + pltpu.roll(x, HEAD_DIM // 2, axis=1) * sin_t


def _rope_idx(x, cos_t, sin_t, first_half):
    half = IDX_DIM // 2
    partner = jnp.where(first_half, pltpu.roll(x, LANES - half, axis=1), pltpu.roll(x, half, axis=1))
    return x * cos_t + partner * sin_t


def _head_norm(x, g):
    ms = jnp.mean(x * x, axis=-1, keepdims=True)
    return (x * lax.rsqrt(ms + EPS)) * g


def _qk_post_body(q_ref, k_ref, v_ref, qi_ref, kiw_ref, ch_ref, sh_ref, ci_ref, si_ref, qn_ref, kn_ref,
                  qo_ref, ko_ref, kb_ref, vo_ref, vb_ref, qio_ref, kio_ref, kilo_ref, kihi_ref):
    ch, sh, ci, si = ch_ref[...], sh_ref[...], ci_ref[...], si_ref[...]
    tm = ch.shape[0]
    lane = lax.broadcasted_iota(jnp.int32, ch.shape, 1)
    first_half = (lane & (IDX_DIM - 1)) < (IDX_DIM // 2)
    for h in range(N_HEADS):
        sl = slice(h * HEAD_DIM, (h + 1) * HEAD_DIM)
        qo_ref[:, sl] = _rope_head(_head_norm(q_ref[:, sl], qn_ref[...]), ch, sh).astype(qo_ref.dtype)
    for h in range(N_KV_HEADS):
        sl = slice(h * HEAD_DIM, (h + 1) * HEAD_DIM)
        kr = _rope_head(_head_norm(k_ref[:, sl], kn_ref[...]), ch, sh)
        ko_ref[pl.ds(h, tm, stride=N_KV_HEADS), :] = kr
        vo_ref[pl.ds(h, tm, stride=N_KV_HEADS), :] = v_ref[:, sl]
        kb_ref[:, sl] = kr.astype(kb_ref.dtype)
    vb_ref[...] = v_ref[...].astype(vb_ref.dtype)
    for h in range(IDX_W // LANES):
        sl = slice(h * LANES, (h + 1) * LANES)
        qio_ref[:, sl] = _rope_idx(qi_ref[:, sl], ci, si, first_half).astype(qio_ref.dtype)
    kir = _rope_idx(kiw_ref[...], ci, si, first_half)
    kio_ref[...] = kir[:, :IDX_DIM]
    zero = jnp.zeros_like(kir)
    kilo_ref[...] = jnp.where(lane < IDX_DIM, kir, zero).astype(kilo_ref.dtype)
    kihi_ref[...] = jnp.where(lane < IDX_DIM, zero, pltpu.roll(kir, IDX_DIM, axis=1)).astype(kihi_ref.dtype)


def _qk_post(proj, ki_col, tabs, tab_rows, q_norm, k_norm, act_dtype):
    m = proj.shape[0]
    tm = _pick(min(m, tab_rows), (512, 256, 128, 8))
    n_tab = tab_rows // tm
    col = lambda off, w: (lambda i: (i, off // w))
    tab_spec = pl.BlockSpec((tm, LANES), lambda i: (i % n_tab, 0))
    row = lambda w: pl.BlockSpec((tm, w), lambda i: (i, 0))
    head_rows = pl.BlockSpec((tm * N_KV_HEADS, HEAD_DIM), lambda i: (i, 0))
    ki_blk = ki_col // LANES
    outs = pl.pallas_call(
        _qk_post_body,
        grid=(m // tm,),
        in_specs=[
            pl.BlockSpec((tm, ATT_W), col(COL_Q, ATT_W)),
            pl.BlockSpec((tm, KV_W), col(COL_K, KV_W)),
            pl.BlockSpec((tm, KV_W), col(COL_V, KV_W)),
            pl.BlockSpec((tm, IDX_W), col(COL_QI, IDX_W)),
            pl.BlockSpec((tm, LANES), lambda i: (i, ki_blk)),
            tab_spec, tab_spec, tab_spec, tab_spec,
            pl.BlockSpec((1, HEAD_DIM), lambda i: (0, 0)),
            pl.BlockSpec((1, HEAD_DIM), lambda i: (0, 0)),
        ],
        out_specs=[row(ATT_W), head_rows, row(KV_W), head_rows, row(KV_W), row(IDX_W), row(IDX_DIM),
                   row(LANES), row(LANES)],
        out_shape=[
            jax.ShapeDtypeStruct((m, ATT_W), act_dtype),
            jax.ShapeDtypeStruct((m * N_KV_HEADS, HEAD_DIM), F32),
            jax.ShapeDtypeStruct((m, KV_W), act_dtype),
            jax.ShapeDtypeStruct((m * N_KV_HEADS, HEAD_DIM), F32),
            jax.ShapeDtypeStruct((m, KV_W), act_dtype),
            jax.ShapeDtypeStruct((m, IDX_W), act_dtype),
            jax.ShapeDtypeStruct((m, IDX_DIM), F32),
            jax.ShapeDtypeStruct((m, LANES), act_dtype),
            jax.ShapeDtypeStruct((m, LANES), act_dtype),
        ],
        compiler_params=_params(),
        name="qk_post",
    )(proj, proj, proj, proj, proj, *tabs, q_norm.reshape(1, HEAD_DIM), k_norm.reshape(1, HEAD_DIM))
    return outs


def _strict_upper(n):
    r = lax.broadcasted_iota(jnp.int32, (n, n), 0)
    c = lax.broadcasted_iota(jnp.int32, (n, n), 1)
    return jnp.where(r < c, 1.0, 0.0).astype(BF16)


def _strict_lower(n):
    r = lax.broadcasted_iota(jnp.int32, (n, n), 0)
    c = lax.broadcasted_iota(jnp.int32, (n, n), 1)
    return jnp.where(c < r, 1.0, 0.0).astype(BF16)


def _attn_prompt_body(q_ref, qi_ref, kiw_ref, k_ref, v_ref, kilo_ref, kihi_ref, o_ref,
                      key_scr, bias_scr, lg_scr, m_scr, l_scr, acc_scr, *, topk):
    i = pl.program_id(1)
    tq = q_ref.shape[0]
    wc = key_scr.shape[1]
    n_ch = i + 1
    wi_t = jnp.transpose(kiw_ref[...])
    q_pos = i * tq + lax.broadcasted_iota(jnp.int32, (wc, tq), 1)
    k_off = lax.broadcasted_iota(jnp.int32, (wc, tq), 0)

    def key_count(mask):
        return jnp.sum(jnp.where(mask, 1.0, 0.0), axis=0, keepdims=True)

    def score_chunk(kc, carry):
        r0 = pl.multiple_of(kc * wc, wc)
        ki_lo = kilo_ref[pl.ds(r0, wc), :]
        ki_hi = kihi_ref[pl.ds(r0, wc), :]
        score = jnp.zeros((wc, tq), F32)
        for pair in range(N_IDX_HEADS // 2):
            x = qi_ref[:, pair * LANES:(pair + 1) * LANES]
            w_lo = wi_t[IDX_DIM + 2 * pair:IDX_DIM + 2 * pair + 1, :]
            w_hi = wi_t[IDX_DIM + 2 * pair + 1:IDX_DIM + 2 * pair + 2, :]
            score = score + w_lo * jnp.maximum(_nt_dot(ki_lo, x), 0.0)
            score = score + w_hi * jnp.maximum(_nt_dot(ki_hi, x), 0.0)
        key_scr[kc] = jnp.where(kc * wc + k_off <= q_pos, _sort_key(score * IDX_SCALE), INT_MIN)
        return carry

    lax.fori_loop(0, n_ch, score_chunk, 0)

    def search(it, res):
        trial = res + jnp.left_shift(jnp.int32(1), 31 - it)

        def count_chunk(kc, acc):
            hit = jnp.where(key_scr[kc] >= trial, 1.0, 0.0)
            for j in range(wc // SUBLANES):
                acc = acc + hit[j * SUBLANES:(j + 1) * SUBLANES, :]
            return acc

        acc = lax.fori_loop(0, n_ch, count_chunk, jnp.zeros((SUBLANES, tq), F32))
        return jnp.where(jnp.sum(acc, axis=0, keepdims=True) >= float(topk), trial, res)

    thr = lax.fori_loop(0, 32, search, jnp.full((1, tq), INT_MIN, jnp.int32))

    def bias_chunk(kc, n_keep):
        key = key_scr[kc]
        keep = (key >= thr) & (key > KEY_NEG_INF)
        bias_scr[kc] = jnp.transpose(jnp.where(keep, 0.0, MASKED))
        return n_keep + key_count(keep)

    n_keep = lax.fori_loop(0, n_ch, bias_chunk, jnp.zeros((1, tq), F32))

    @pl.when(jnp.max(n_keep) > float(topk))
    def _():
        n_gt = lax.fori_loop(0, n_ch, lambda kc, n: n + key_count(key_scr[kc] > thr), jnp.zeros((1, tq), F32))
        need = float(topk) - n_gt
        lower = _strict_lower(wc)

        def tie_chunk(kc, before):
            key = key_scr[kc]
            eq = key == thr
            eqf = jnp.where(eq, 1.0, 0.0)
            rank = before + _dot(lower, eqf.astype(BF16))
            keep = ((key > thr) | (eq & (rank < need))) & (key > KEY_NEG_INF)
            bias_scr[kc] = jnp.transpose(jnp.where(keep, 0.0, MASKED))
            return before + jnp.sum(eqf, axis=0, keepdims=True)

        lax.fori_loop(0, n_ch, tie_chunk, jnp.zeros((1, tq), F32))

    m_scr[...] = jnp.full(m_scr.shape, MASKED, F32)
    l_scr[...] = jnp.zeros(l_scr.shape, F32)
    acc_scr[...] = jnp.zeros(acc_scr.shape, F32)
    lane_blocks = [slice(j * LANES, (j + 1) * LANES) for j in range(wc // LANES)]

    def logit_chunk(kc, carry):
        r0 = pl.multiple_of(kc * wc, wc)
        bias = bias_scr[kc]
        for h in range(N_HEADS):
            g = h // GQA_GROUP
            kg = k_ref[pl.ds(r0, wc), g * HEAD_DIM:(g + 1) * HEAD_DIM]
            lg = _nt_dot(q_ref[:, h * HEAD_DIM:(h + 1) * HEAD_DIM], kg) * ATT_SCALE + bias
            lg_scr[h, kc] = lg
            mx = m_scr[h]
            for lb in lane_blocks:
                mx = jnp.maximum(mx, lg[:, lb])
            m_scr[h] = mx
        return carry

    lax.fori_loop(0, n_ch, logit_chunk, 0)
    for h in range(N_HEADS):
        m_scr[h] = jnp.broadcast_to(jnp.max(m_scr[h], axis=1, keepdims=True), (tq, LANES))

    def prob_chunk(kc, carry):
        r0 = pl.multiple_of(kc * wc, wc)
        for h in range(N_HEADS):
            g = h // GQA_GROUP
            sl = slice(h * HEAD_DIM, (h + 1) * HEAD_DIM)
            vg = v_ref[pl.ds(r0, wc), g * HEAD_DIM:(g + 1) * HEAD_DIM]
            lg = lg_scr[h, kc]
            mx = m_scr[h]
            ps = [jnp.exp(lg[:, lb] - mx) for lb in lane_blocks]
            den = l_scr[h]
            for p in ps:
                den = den + p
            l_scr[h] = den
            acc_scr[:, sl] = acc_scr[:, sl] + _dot(jnp.concatenate(ps, axis=1).astype(BF16), vg)
        return carry

    lax.fori_loop(0, n_ch, prob_chunk, 0)
    for h in range(N_HEADS):
        sl = slice(h * HEAD_DIM, (h + 1) * HEAD_DIM)
        den = jnp.sum(l_scr[h], axis=1, keepdims=True)
        o_ref[:, sl] = (acc_scr[:, sl] / den).astype(o_ref.dtype)


def _attn_prompt(q, qi, proj, ki_col, k, v, ki_lo, ki_hi, nb, s):
    tq = _pick(s, (256, 128))
    nq = s // tq
    topk = min(TOPK_MAX, s // 4)
    ki_blk = ki_col // LANES
    qrow = lambda w: pl.BlockSpec((tq, w), lambda b, i: (b * nq + i, 0))
    full = lambda w: pl.BlockSpec((s, w), lambda b, i: (b, 0))
    return pl.pallas_call(
        functools.partial(_attn_prompt_body, topk=topk),
        grid=(nb, nq),
        in_specs=[qrow(ATT_W), qrow(IDX_W), pl.BlockSpec((tq, LANES), lambda b, i: (b * nq + i, ki_blk)),
                  full(KV_W), full(KV_W), full(LANES), full(LANES)],
        out_specs=qrow(ATT_W),
        out_shape=jax.ShapeDtypeStruct((nb * s, ATT_W), BF16),
        scratch_shapes=[pltpu.VMEM((nq, tq, tq), jnp.int32), pltpu.VMEM((nq, tq, tq), F32),
                        pltpu.VMEM((N_HEADS, nq, tq, tq), F32),
                        pltpu.VMEM((N_HEADS, tq, LANES), F32), pltpu.VMEM((N_HEADS, tq, LANES), F32),
                        pltpu.VMEM((tq, ATT_W), F32)],
        compiler_params=_params(),
        name="attn_prompt",
    )(q, qi, proj, k, v, ki_lo, ki_hi)


def _s5_discretize_body(are_ref, aim_ref, ldt_ref, bre_ref, bim_ref, abre_ref, abim_ref, bbre_ref, bbim_ref):
    dt = jnp.exp(ldt_ref[...])
    lr, li = are_ref[...], aim_ref[...]
    mag = jnp.exp(lr * dt)
    ab_re, ab_im = mag * jnp.cos(li * dt), mag * jnp.sin(li * dt)
    zr, zi = ab_re - 1.0, ab_im
    den = lr * lr + li * li
    fr = (zr * lr + zi * li) / den
    fi = (zi * lr - zr * li) / den
    abre_ref[...] = ab_re
    abim_ref[...] = ab_im
    br, bi = bre_ref[...], bim_ref[...]
    bbre_ref[...] = fr * br - fi * bi
    bbim_ref[...] = fr * bi + fi * br


def _s5_discretize(a_re, a_im, log_dt, b_re, b_im):
    depth = a_re.shape[0]
    gp = pl.BlockSpec((None, SSM_GROUPS, 1, SSM_P), lambda l: (l, 0, 0, 0))
    gwp = pl.BlockSpec((None, SSM_GROUPS, SSM_GROUP, SSM_P), lambda l: (l, 0, 0, 0))
    bt = lambda b: jnp.swapaxes(b, 2, 3)
    a4 = lambda a: a.reshape(depth, SSM_GROUPS, 1, SSM_P)
    return pl.pallas_call(
        _s5_discretize_body,
        grid=(depth,),
        in_specs=[gp, gp, pl.BlockSpec((None, SSM_GROUPS, 1, 1), lambda l: (l, 0, 0, 0)), gwp, gwp],
        out_specs=[gp, gp, gwp, gwp],
        out_shape=[jax.ShapeDtypeStruct((depth, SSM_GROUPS, 1, SSM_P), F32)] * 2
        + [jax.ShapeDtypeStruct((depth, SSM_GROUPS, SSM_GROUP, SSM_P), F32)] * 2,
        compiler_params=_params(),
        name="s5_discretize",
    )(a4(a_re), a4(a_im), log_dt.reshape(depth, SSM_GROUPS, 1, 1), bt(b_re), bt(b_im))


def _block_diag_in(bb_t):
    depth = bb_t.shape[0]
    gpb = SSM_GROUPS // SSM_BLOCKS
    x = bb_t.reshape(depth, SSM_BLOCKS, gpb, SSM_GROUP, SSM_P)
    eye = jnp.eye(gpb, dtype=bb_t.dtype)
    out = jnp.einsum('lkgwp,gh->lkgwhp', x, eye)
    return out.reshape(depth, SSM_BLOCKS, gpb * SSM_GROUP, gpb * SSM_P).astype(BF16)


def _block_diag_out(c):
    depth = c.shape[0]
    gpb = SSM_GROUPS // SSM_BLOCKS
    x = c.reshape(depth, SSM_BLOCKS, gpb, SSM_GROUP, SSM_P)
    eye = jnp.eye(gpb, dtype=c.dtype)
    out = jnp.einsum('lkgwp,gh->lkgphw', x, eye)
    return out.reshape(depth, SSM_BLOCKS, gpb * SSM_P, gpb * SSM_GROUP).astype(BF16)


def _s5_body(u_ref, bbre_ref, bbim_ref, cre_ref, cim_ref, abre_ref, abim_ref, d_ref, h0re_ref, h0im_ref,
             g_ref, htre_ref, htim_ref, bure, buim, hre, him, ust, gst, *, nb):
    c = pl.program_id(0)
    rows = bure.shape[0]
    n_t = rows // nb
    per_tile = SUBLANES // nb

    @pl.when(c == 0)
    def _():
        hre[...] = h0re_ref[...]
        him[...] = h0im_ref[...]

    for k in range(SSM_BLOCKS):
        ln = slice(k * LANES, (k + 1) * LANES)
        for b in range(nb):
            if n_t == 1:
                ust[k, b:b + 1, :] = u_ref[b, :, ln]
            else:
                ust[k, pl.ds(b, n_t, stride=nb), :] = u_ref[b, :, ln]
        uk = ust[k].astype(BF16)
        st = slice(k * SSM_BLOCK_STATE, (k + 1) * SSM_BLOCK_STATE)
        bure[:, st] = _dot(uk, bbre_ref[k])
        buim[:, st] = _dot(uk, bbim_ref[k])

    for cc in range(SSM_STATE // SCAN_LANES):
        sl = slice(cc * SCAN_LANES, (cc + 1) * SCAN_LANES)
        ar = jnp.broadcast_to(abre_ref[:, sl], (nb, SCAN_LANES))
        ai = jnp.broadcast_to(abim_ref[:, sl], (nb, SCAN_LANES))

        def step(j, carry, sl=sl, ar=ar, ai=ai):
            hr, hi = carry
            r0 = pl.multiple_of(j * SUBLANES, SUBLANES)
            xr = bure[pl.ds(r0, SUBLANES), sl]
            xi = buim[pl.ds(r0, SUBLANES), sl]
            out_r, out_i = [], []
            for t in range(per_tile):
                nr = ar * hr - ai * hi + xr[t * nb:(t + 1) * nb]
                ni = ar * hi + ai * hr + xi[t * nb:(t + 1) * nb]
                hr, hi = nr, ni
                out_r.append(hr)
                out_i.append(hi)
            bure[pl.ds(r0, SUBLANES), sl] = out_r[0] if per_tile == 1 else jnp.concatenate(out_r, axis=0)
            buim[pl.ds(r0, SUBLANES), sl] = out_i[0] if per_tile == 1 else jnp.concatenate(out_i, axis=0)
            return hr, hi

        hr, hi = lax.fori_loop(0, rows // SUBLANES, step, (hre[:, sl], him[:, sl]))
        hre[:, sl] = hr
        him[:, sl] = hi

    for k in range(SSM_BLOCKS):
        st = slice(k * SSM_BLOCK_STATE, (k + 1) * SSM_BLOCK_STATE)
        ln = slice(k * LANES, (k + 1) * LANES)
        y = _dot(bure[:, st].astype(BF16), cre_ref[k]) - _dot(buim[:, st].astype(BF16), cim_ref[k])
        y = y + d_ref[:, ln] * ust[k]
        gst[k] = _gelu_tanh(y)
        for b in range(nb):
            if n_t == 1:
                g_ref[b, :, ln] = gst[k, b:b + 1, :]
            else:
                g_ref[b, :, ln] = gst[k, pl.ds(b, n_t, stride=nb), :]

    @pl.when(c == pl.num_programs(0) - 1)
    def _():
        htre_ref[...] = hre[...]
        htim_ref[...] = him[...]


def _s5(proj, nb, s, l, bb_re, bb_im, c_re, c_im, ab_re, ab_im, d, h0_re, h0_im):
    n_t = min(s, 512 // nb)
    rows = n_t * nb
    blk4 = lambda a: pl.BlockSpec((None,) + a.shape[1:], lambda c: (l, 0, 0, 0))
    full2 = lambda a: pl.BlockSpec(a.shape, lambda c: (0, 0))
    vec = lambda a: a.reshape(1, -1)
    ab_re, ab_im, d = vec(ab_re), vec(ab_im), vec(d)
    u_spec = pl.BlockSpec((nb, n_t, SSM_W), lambda c: (0, c, 0))
    g, h_re, h_im = pl.pallas_call(
        functools.partial(_s5_body, nb=nb),
        grid=(s // n_t,),
        in_specs=[u_spec, blk4(bb_re), blk4(bb_im), blk4(c_re), blk4(c_im),
                  full2(ab_re), full2(ab_im), full2(d), full2(h0_re), full2(h0_im)],
        out_specs=[u_spec, full2(h0_re), full2(h0_im)],
        out_shape=[jax.ShapeDtypeStruct((nb, s, SSM_W), F32),
                   jax.ShapeDtypeStruct(h0_re.shape, F32), jax.ShapeDtypeStruct(h0_im.shape, F32)],
        scratch_shapes=[pltpu.VMEM((rows, SSM_STATE), F32), pltpu.VMEM((rows, SSM_STATE), F32),
                        pltpu.VMEM((nb, SSM_STATE), F32), pltpu.VMEM((nb, SSM_STATE), F32),
                        pltpu.VMEM((SSM_BLOCKS, rows, LANES), F32), pltpu.VMEM((SSM_BLOCKS, rows, LANES), F32)],
        compiler_params=_params(),
        name="s5_scan",
    )(proj.reshape(nb, s, proj.shape[1]), bb_re, bb_im, c_re, c_im, ab_re, ab_im, d, h0_re, h0_im)
    return g.reshape(nb * s, SSM_W), h_re, h_im


IDX_PAGES_PER_STEP = 16


def _sample_index_body(pt_ref, qi_ref, wi_ref, kin_ref, *rest, n_pg, topk):
    del pt_ref
    pages = rest[:n_pg]
    idx_ref, nsel_ref, bnew_ref, sc_scr, rank_scr = rest[n_pg:]
    c = pl.program_id(1)
    n_pages = sc_scr.shape[0]
    qi = qi_ref[...].astype(BF16)
    wi = wi_ref[...]

    def weighted(d):
        return jnp.sum(wi * jnp.maximum(d, 0.0), axis=0, keepdims=True) * IDX_SCALE

    for p in range(n_pg):
        sc_scr[pl.ds(c * n_pg + p, 1), :] = weighted(_dot(qi, pages[p][...].astype(BF16)))

    @pl.when(c == pl.num_programs(1) - 1)
    def _():
        kin = kin_ref[...].astype(BF16).astype(F32)
        d_new = jnp.sum(qi.astype(F32) * kin, axis=1, keepdims=True)
        key_new = _sort_key(jnp.broadcast_to(weighted(d_new), (1, LANES)))
        key = _sort_key(sc_scr[...])

        def count(mask, mask_new):
            per_lane = jnp.sum(jnp.where(mask, 1.0, 0.0), axis=0, keepdims=True)
            return jnp.sum(per_lane, axis=1, keepdims=True) + jnp.where(mask_new, 1.0, 0.0)

        def body(it, res):
            trial = res + jnp.left_shift(jnp.int32(1), 31 - it)
            return jnp.where(count(key >= trial, key_new >= trial) >= float(topk), trial, res)

        thr = lax.fori_loop(0, 32, body, jnp.full((1, LANES), INT_MIN, jnp.int32))
        need = float(topk) - count(key > thr, key_new > thr)
        eq = key == thr
        eqf = jnp.where(eq, 1.0, 0.0).astype(BF16)
        in_page = _dot(eqf, _strict_upper(LANES))
        lower = _strict_lower(n_pages)
        per_page = jnp.sum(jnp.where(eq, 1.0, 0.0), axis=1, keepdims=True)
        earlier = _dot(lower, jnp.broadcast_to(per_page, (n_pages, LANES)).astype(BF16))
        keep = ((key > thr) | (eq & (in_page + earlier < need))) & (key > KEY_NEG_INF)
        n_eq_past = jnp.sum(jnp.sum(jnp.where(eq, 1.0, 0.0), axis=0, keepdims=True), axis=1, keepdims=True)
        keep_new = ((key_new > thr) | ((key_new == thr) & (n_eq_past < need))) & (key_new > KEY_NEG_INF)
        bnew_ref[...] = jnp.where(keep_new, 0.0, MASKED)

        keepf = jnp.where(keep, 1.0, 0.0)
        kept_per_page = jnp.sum(keepf, axis=1, keepdims=True)
        slot_of = (_dot(keepf.astype(BF16), _strict_upper(LANES))
                   + _dot(lower, jnp.broadcast_to(kept_per_page, (n_pages, LANES)).astype(BF16)))
        rank_scr[...] = jnp.where(keep, slot_of, -1.0)
        slot = lax.broadcasted_iota(jnp.int32, (topk, LANES), 0).astype(F32)
        lane = lax.broadcasted_iota(jnp.int32, (topk, LANES), 1)

        def place_page(p, acc):
            return acc + jnp.where(rank_scr[pl.ds(p, 1), :] == slot, p * PAGE_SIZE + lane, 0)

        acc = lax.fori_loop(0, n_pages, place_page, jnp.zeros((topk, LANES), jnp.int32))
        idx_ref[...] = jnp.sum(acc.astype(F32), axis=1, keepdims=True).astype(jnp.int32)
        n_kept = jnp.sum(jnp.sum(keepf, axis=0, keepdims=True), axis=1, keepdims=True)
        nsel_ref[...] = jnp.broadcast_to(n_kept, (1, LANES)).astype(jnp.int32)


def _sample_index(page_table, cache_kidx_t, layer, qi, wi, ki_new):
    nb, n_pages = page_table.shape
    n_pg = _pick(n_pages, (IDX_PAGES_PER_STEP, 8, 4, 2, 1))
    topk = min(TOPK_MAX, (n_pages * PAGE_SIZE + 1) // 4)
    page_spec = lambda p: pl.BlockSpec((None, None, IDX_DIM, PAGE_SIZE),
                                       lambda b, c, pt: (layer, pt[b, c * n_pg + p], 0, 0))
    per_b = lambda shape: pl.BlockSpec((None,) + shape, lambda b, c, pt: (b, 0, 0))
    grid_spec = pltpu.PrefetchScalarGridSpec(
        num_scalar_prefetch=1,
        grid=(nb, n_pages // n_pg),
        in_specs=[per_b((N_IDX_HEADS, IDX_DIM)), per_b((N_IDX_HEADS, 1)), per_b((1, IDX_DIM))]
        + [page_spec(p) for p in range(n_pg)],
        out_specs=[per_b((topk, 1)), per_b((1, LANES)), per_b((1, LANES))],
        scratch_shapes=[pltpu.VMEM((n_pages, LANES), F32), pltpu.VMEM((n_pages, LANES), F32)],
    )
    idx, n_sel, bias_new = pl.pallas_call(
        functools.partial(_sample_index_body, n_pg=n_pg, topk=topk),
        grid_spec=grid_spec,
        out_shape=[jax.ShapeDtypeStruct((nb, topk, 1), jnp.int32), jax.ShapeDtypeStruct((nb, 1, LANES), jnp.int32),
                   jax.ShapeDtypeStruct((nb, 1, LANES), F32)],
        compiler_params=_params(),
        name="sample_index",
    )(page_table, qi.reshape(nb, N_IDX_HEADS, IDX_DIM), wi.reshape(nb, N_IDX_HEADS, 1),
      ki_new.reshape(nb, 1, IDX_DIM), *([cache_kidx_t] * n_pg))
    return idx.reshape(nb, topk), n_sel[:, 0, 0], bias_new


SLOTS_PER_TILE = SUBLANES // N_KV_HEADS


def _group_rows(n_cols):
    head = lax.broadcasted_iota(jnp.int32, (N_HEADS, n_cols), 0)
    return [(head >= g * GQA_GROUP) & (head < (g + 1) * GQA_GROUP) for g in range(N_KV_HEADS)]


def _sample_attn_body(pt_ref, idx_ref, nsel_ref, q_ref, kn_ref, vn_ref, bnew_ref, odd_ref, ck_hbm, cv_hbm,
                      o_ref, kbuf, vbuf, sem, *, layer, topk):
    b = pl.program_id(0)

    def tile_copy(src_hbm, buf, which, phys, src_row, dst_row):
        return pltpu.make_async_copy(src_hbm.at[layer, phys, pl.ds(src_row, SUBLANES), :],
                                     buf.at[pl.ds(dst_row, SUBLANES), :], sem.at[which])

    def issue(r, carry):
        idx = idx_ref[b, r]
        phys = pt_ref[b, idx // PAGE_SIZE]
        src_row = pl.multiple_of(((idx % PAGE_SIZE) // SLOTS_PER_TILE) * SUBLANES, SUBLANES)
        dst_row = pl.multiple_of(r * SUBLANES, SUBLANES)
        tile_copy(ck_hbm, kbuf, 0, phys, src_row, dst_row).start()
        tile_copy(cv_hbm, vbuf, 1, phys, src_row, dst_row).start()
        return carry

    lax.fori_loop(0, topk, issue, 0)

    def drain(r, carry):
        tile_copy(ck_hbm, kbuf, 0, 0, 0, 0).wait()
        tile_copy(cv_hbm, vbuf, 1, 0, 0, 0).wait()
        return carry

    lax.fori_loop(0, topk, drain, 0)

    q = q_ref[...].astype(BF16)
    odd = odd_ref[...] == 1
    live = lax.broadcasted_iota(jnp.int32, (1, topk), 1) < nsel_ref[b]

    def gathered(buf, g):
        first = buf[pl.ds(g, topk, stride=SUBLANES), :]
        second = buf[pl.ds(N_KV_HEADS + g, topk, stride=SUBLANES), :]
        return jnp.where(odd, second, first).astype(BF16)

    in_group = _group_rows(topk)
    lg = jnp.zeros((N_HEADS, topk), F32)
    for g in range(N_KV_HEADS):
        lg = jnp.where(in_group[g], _nt_dot(q, gathered(kbuf, g)), lg)
    lg = jnp.where(live, lg * ATT_SCALE, MASKED)
    qf = q.astype(F32)
    kn = kn_ref[...].astype(BF16).astype(F32)
    vn = vn_ref[...].astype(BF16).astype(F32)
    lg_new = jnp.sum(qf * kn, axis=1, keepdims=True) * ATT_SCALE + bnew_ref[:, :1]
    mx = jnp.maximum(jnp.max(lg, axis=1, keepdims=True), lg_new)
    pr = jnp.where(live, jnp.exp(lg - mx), 0.0)
    pr_new = jnp.where(bnew_ref[:, :1] == 0.0, jnp.exp(lg_new - mx), 0.0)
    den = jnp.sum(pr, axis=1, keepdims=True) + pr_new
    prb = pr.astype(BF16)
    in_group_d = _group_rows(HEAD_DIM)
    pv = jnp.zeros((N_HEADS, HEAD_DIM), F32)
    for g in range(N_KV_HEADS):
        pv = jnp.where(in_group_d[g], _dot(prb, gathered(vbuf, g)), pv)
    o_ref[...] = ((pv + pr_new.astype(BF16).astype(F32) * vn) / den).astype(o_ref.dtype)


def _sample_attn(page_table, cache_k2, cache_v2, layer, idx, n_sel, bias_new, q, k_new, v_new):
    nb = page_table.shape[0]
    topk = idx.shape[1]
    per_b = lambda shape: pl.BlockSpec((None,) + shape, lambda b, pt, ix, ns: (b, 0, 0))
    per_head = lambda a: jnp.repeat(a.reshape(nb, N_KV_HEADS, HEAD_DIM), GQA_GROUP, axis=1)
    odd = (idx % SLOTS_PER_TILE).reshape(nb, topk, 1)
    grid_spec = pltpu.PrefetchScalarGridSpec(
        num_scalar_prefetch=3,
        grid=(nb,),
        in_specs=[per_b((N_HEADS, HEAD_DIM)), per_b((N_HEADS, HEAD_DIM)), per_b((N_HEADS, HEAD_DIM)),
                  per_b((1, LANES)), per_b((topk, 1)),
                  pl.BlockSpec(memory_space=pl.ANY), pl.BlockSpec(memory_space=pl.ANY)],
        out_specs=per_b((N_HEADS, HEAD_DIM)),
        scratch_shapes=[pltpu.VMEM((topk * SUBLANES, HEAD_DIM), F32), pltpu.VMEM((topk * SUBLANES, HEAD_DIM), F32),
                        pltpu.SemaphoreType.DMA((2,))],
    )
    out = pl.pallas_call(
        functools.partial(_sample_attn_body, layer=layer, topk=topk),
        grid_spec=grid_spec,
        out_shape=jax.ShapeDtypeStruct((nb, N_HEADS, HEAD_DIM), F32),
        compiler_params=_params(),
        name="sample_attn",
    )(page_table, idx, n_sel, q.reshape(nb, N_HEADS, HEAD_DIM), per_head(k_new), per_head(v_new), bias_new, odd,
      cache_k2, cache_v2)
    return out.reshape(nb, ATT_W)


def _layer_tail(proj, g, y_att, x, w, l, act_dtype):
    z = _glu(g, w["w_glu"], l, act_dtype)
    merged = _merge(z, y_att, w["w_branch"], l, proj, act_dtype)
    x1, hn = _out_norm(merged, w["w_out"], l, x, w["norm_ffn"][l], act_dtype)
    act = _ffn_up(hn, w["w_ffn_up"], l, act_dtype)
    return _ffn_down(act, w["w_ffn_down"], l, x1)


def _s5_layer(proj, nb, s, w, l, h0_re, h0_im):
    return _s5(proj, nb, s, l, w["bb_re"], w["bb_im"], w["c_re"], w["c_im"], w["ab_re"][l], w["ab_im"][l],
               w["ssm_d"][l], h0_re, h0_im)


def _prompt_layer(x, nb, s, tabs, w, l):
    xn = _rmsnorm(x, w["norm_mix"][l], BF16)
    proj = _in_proj(xn, w["w_in"], l, PROJ_TN, F32)
    q, k_out, k_bf, v_out, v_bf, qi, ki_out, ki_lo, ki_hi = _qk_post(
        proj, w["ki_col"], tabs, s, w["q_norm"][l], w["k_norm"][l], BF16)
    y_att = _attn_prompt(q, qi, proj, w["ki_col"], k_bf, v_bf, ki_lo, ki_hi, nb, s)
    zeros = jnp.zeros((nb, SSM_STATE), F32)
    g, h_re, h_im = _s5_layer(proj, nb, s, w, l, zeros, zeros)
    x2 = _layer_tail(proj, g, y_att, x, w, l, BF16)
    return x2, k_out, v_out, ki_out, h_re, h_im


def _sample_layer(x, nb, tabs, w, l, page_table, cache_k2, cache_v2, cache_kidx_t, h0_re, h0_im):
    xn = _rmsnorm(x, w["norm_mix"][l], F32)
    proj = _in_proj(xn, w["w_in"], l, PROJ_TN, F32)
    ki_col = w["ki_col"]
    q, k_out, _, v_out, _, qi, ki_out, _, _ = _qk_post(proj, ki_col, tabs, nb, w["q_norm"][l], w["k_norm"][l], F32)
    wi = proj[:, ki_col + IDX_DIM:ki_col + IDX_DIM + N_IDX_HEADS]
    idx, n_sel, bias_new = _sample_index(page_table, cache_kidx_t, l, qi, wi, ki_out)
    y_att = _sample_attn(page_table, cache_k2, cache_v2, l, idx, n_sel, bias_new, q, k_out, v_out)
    g, h_re, h_im = _s5_layer(proj, nb, 1, w, l, h0_re, h0_im)
    x2 = _layer_tail(proj, g, y_att, x, w, l, F32)
    return x2, k_out, v_out, ki_out, h_re, h_im


def kernel(x_prompt, x_sample, cache_k, cache_v, cache_kidx, state_ssm_re, state_ssm_im, page_table, norm_mix, w_in, q_norm, k_norm, ssm_a_re, ssm_a_im, ssm_log_dt, ssm_b_re, ssm_b_im, ssm_c_re, ssm_c_im, ssm_d, w_glu, w_branch, w_out, norm_ffn, w_ffn_up, w_ffn_down):
    nb_p, s, d_model = x_prompt.shape
    nb_s, dec_seq, _ = x_sample.shape
    assert dec_seq == 1, "the sample group decodes one token per sequence"
    depth = w_in.shape[0]
    past = page_table.shape[1] * PAGE_SIZE

    ki_orig = COL_GATE
    gate_orig = ki_orig + IDX_DIM + N_IDX_HEADS
    n_gate = w_in.shape[2] - gate_orig
    ki_col = COL_GATE + n_gate
    assert ki_col % LANES == 0
    cols = ki_col + LANES
    cols_pad = -(-cols // PROJ_TN) * PROJ_TN
    w_in_r = jnp.concatenate(
        [w_in[:, :, :COL_GATE], w_in[:, :, gate_orig:], w_in[:, :, ki_orig:gate_orig],
         jnp.zeros((depth, d_model, cols_pad - ki_col - IDX_DIM - N_IDX_HEADS), w_in.dtype)], axis=2).astype(BF16)

    ab_re, ab_im, bbt_re, bbt_im = _s5_discretize(ssm_a_re, ssm_a_im, ssm_log_dt, ssm_b_re, ssm_b_im)
    bb_re, bb_im = _block_diag_in(bbt_re), _block_diag_in(bbt_im)
    c_re, c_im = _block_diag_out(ssm_c_re), _block_diag_out(ssm_c_im)

    tabs_p = _rope_tables(jnp.arange(s, dtype=jnp.int32))
    tabs_s = _rope_tables(jnp.full((nb_s,), past, dtype=jnp.int32))

    xp = x_prompt.reshape(nb_p * s, d_model)
    xs = x_sample.reshape(nb_s, d_model)
    cache_k2 = cache_k.reshape(cache_k.shape[:2] + (PAGE_SIZE * N_KV_HEADS, HEAD_DIM))
    cache_v2 = cache_v.reshape(cache_v.shape[:2] + (PAGE_SIZE * N_KV_HEADS, HEAD_DIM))
    cache_kidx_t = jnp.swapaxes(cache_kidx, 2, 3)
    w = dict(ki_col=ki_col, norm_mix=norm_mix, w_in=w_in_r, q_norm=q_norm, k_norm=k_norm,
             bb_re=bb_re, bb_im=bb_im, c_re=c_re, c_im=c_im, ab_re=ab_re, ab_im=ab_im, ssm_d=ssm_d,
             w_glu=w_glu, w_branch=w_branch, w_out=w_out, norm_ffn=norm_ffn,
             w_ffn_up=w_ffn_up, w_ffn_down=w_ffn_down)
    outs_p, outs_s = [], []
    for l in range(depth):
        xp, *rest_p = _prompt_layer(xp, nb_p, s, tabs_p, w, l)
        xs, *rest_s = _sample_layer(xs, nb_s, tabs_s, w, l, page_table, cache_k2, cache_v2, cache_kidx_t,
                                    state_ssm_re[l].reshape(nb_s, SSM_STATE), state_ssm_im[l].reshape(nb_s, SSM_STATE))
        outs_p.append(rest_p)
        outs_s.append(rest_s)

    def stack(outs, idx, shape):
        return jnp.stack([o[idx].reshape(shape) for o in outs])

    kv_p, kv_s = (nb_p, s, N_KV_HEADS, HEAD_DIM), (nb_s, dec_seq, N_KV_HEADS, HEAD_DIM)
    st_p, st_s = (nb_p, SSM_GROUPS, SSM_P), (nb_s, SSM_GROUPS, SSM_P)
    return (xp.reshape(nb_p, s, d_model), xs.reshape(nb_s, dec_seq, d_model),
            stack(outs_p, 0, kv_p), stack(outs_p, 1, kv_p), stack(outs_p, 2, (nb_p, s, IDX_DIM)),
            stack(outs_p, 3, st_p), stack(outs_p, 4, st_p),
            stack(outs_s, 0, kv_s), stack(outs_s, 1, kv_s), stack(outs_s, 2, (nb_s, dec_seq, IDX_DIM)),
            stack(outs_s, 3, st_s), stack(outs_s, 4, st_s))
```

```python
import functools
import math

import jax
import jax.numpy as jnp
from jax import lax
from jax.experimental import pallas as pl
from jax.experimental.pallas import tpu as pltpu

F32 = jnp.float32
BF16 = jnp.bfloat16

SSM_W = 1024
SSM_GROUP = 16
SSM_GROUPS = SSM_W // SSM_GROUP
SSM_P = 64
SSM_STATE = SSM_GROUPS * SSM_P
N_HEADS = 8
N_KV_HEADS = 4
HEAD_DIM = 128
GQA_GROUP = N_HEADS // N_KV_HEADS
ATT_W = N_HEADS * HEAD_DIM
KV_W = N_KV_HEADS * HEAD_DIM
N_IDX_HEADS = 16
IDX_DIM = 64
IDX_W = N_IDX_HEADS * IDX_DIM
IDX_SCALE = (IDX_DIM * N_IDX_HEADS) ** -0.5
ATT_SCALE = HEAD_DIM ** -0.5
TOPK_MAX = 256
PAGE_SIZE = 128
ROPE_THETA = 10000.0
EPS = 1e-6

LANES = 128
SUBLANES = 8
VMEM_LIMIT = 56 * 1024 * 1024

COL_U = 0
COL_Q = SSM_W
COL_K = COL_Q + ATT_W
COL_V = COL_K + KV_W
COL_QI = COL_V + KV_W
COL_GATE = COL_QI + IDX_W
PROJ_TN = 768

MASKED = -1e30
INT_MIN = -(2 ** 31)
KEY_NEG_INF = -2139095041

SCAN_LANES = 512
SSM_BLOCKS = SSM_W // LANES
SSM_BLOCK_STATE = SSM_STATE // SSM_BLOCKS


def _params(**kw):
    return pltpu.CompilerParams(vmem_limit_bytes=VMEM_LIMIT, **kw)


def _pick(n, cands):
    for c in cands:
        if n % c == 0:
            return c
    return n


def _nt_dot(a, b):
    return lax.dot_general(a, b, (((1,), (1,)), ((), ())), preferred_element_type=F32)


def _dot(a, b):
    return jnp.dot(a, b, preferred_element_type=F32)


def _sigmoid(x):
    return 1.0 / (1.0 + jnp.exp(-x))


def _gelu_tanh(x):
    c = math.sqrt(2.0 / math.pi)
    return 0.5 * x * (1.0 + jnp.tanh(c * (x + 0.044715 * (x * x * x))))


def _sort_key(s):
    bits = pltpu.bitcast(s, jnp.int32)
    return bits ^ ((bits >> 31) & jnp.int32(0x7FFFFFFF))


def _rmsnorm_body(x_ref, g_ref, o_ref):
    x = x_ref[...].astype(F32)
    ms = jnp.mean(x * x, axis=-1, keepdims=True)
    o_ref[...] = ((x * lax.rsqrt(ms + EPS)) * g_ref[...]).astype(o_ref.dtype)


def _rmsnorm(x, g, out_dtype):
    m, d = x.shape
    tm = _pick(m, (512, 256, 128, 8))
    return pl.pallas_call(
        _rmsnorm_body,
        grid=(m // tm,),
        in_specs=[pl.BlockSpec((tm, d), lambda i: (i, 0)), pl.BlockSpec((1, d), lambda i: (0, 0))],
        out_specs=pl.BlockSpec((tm, d), lambda i: (i, 0)),
        out_shape=jax.ShapeDtypeStruct((m, d), out_dtype),
        compiler_params=_params(),
        name="rmsnorm",
    )(x, g.reshape(1, d))


def _mm_body(a_ref, w_ref, o_ref):
    o_ref[...] = _dot(a_ref[...].astype(BF16), w_ref[...]).astype(o_ref.dtype)


def _in_proj(a, w, l, tn, out_dtype):
    m, k = a.shape
    n = w.shape[2]
    tm = _pick(m, (1024, 512, 256, 128, 8))
    return pl.pallas_call(
        _mm_body,
        grid=(n // tn, m // tm),
        in_specs=[pl.BlockSpec((tm, k), lambda j, i: (i, 0)), pl.BlockSpec((None, k, tn), lambda j, i: (l, 0, j))],
        out_specs=pl.BlockSpec((tm, tn), lambda j, i: (i, j)),
        out_shape=jax.ShapeDtypeStruct((m, n), out_dtype),
        compiler_params=_params(),
        name="in_proj",
    )(a, w)


def _cast_weight_once(step, w_ref, wb_scr):
    @pl.when(step == 0)
    def _():
        wb_scr[...] = w_ref[...].astype(BF16)


def _glu_body(g_ref, w_ref, o_ref, wb):
    _cast_weight_once(pl.program_id(0), w_ref, wb)
    g = g_ref[...]
    z = _dot(g.astype(BF16), wb[...])
    o_ref[...] = (g * _sigmoid(z)).astype(o_ref.dtype)


def _glu(g, w, l, out_dtype):
    m, n = g.shape
    tm = _pick(m, (512, 256, 128, 8))
    return pl.pallas_call(
        _glu_body,
        grid=(m // tm,),
        in_specs=[pl.BlockSpec((tm, n), lambda i: (i, 0)),
                  pl.BlockSpec((None, n, n), lambda i: (l, 0, 0), pipeline_mode=pl.Buffered(1))],
        out_specs=pl.BlockSpec((tm, n), lambda i: (i, 0)),
        out_shape=jax.ShapeDtypeStruct((m, n), out_dtype),
        scratch_shapes=[pltpu.VMEM((n, n), BF16)],
        compiler_params=_params(),
        name="glu",
    )(g, w)


def _merge_body(a0_ref, a1_ref, w0_ref, w1_ref, g0_ref, g1_ref, o_ref, wb0, wb1):
    _cast_weight_once(pl.program_id(1), w0_ref, wb0)
    _cast_weight_once(pl.program_id(1), w1_ref, wb1)
    y0 = _dot(a0_ref[...].astype(BF16), wb0[...])
    y1 = _dot(a1_ref[...].astype(BF16), wb1[...])
    o_ref[...] = (_sigmoid(g0_ref[...]) * y0 + _sigmoid(g1_ref[...]) * y1).astype(o_ref.dtype)


def _merge(a0, a1, w_branch, l, proj, out_dtype):
    m, k = a0.shape
    n = w_branch.shape[3]
    tm = _pick(m, (512, 256, 128, 8))
    tn = _pick(n, (1024, 512, 256, 128))
    g0_blk = COL_GATE // tn
    g1_blk = (COL_GATE + n) // tn
    wspec = lambda br: pl.BlockSpec((None, None, k, tn), lambda j, i: (l, br, 0, j))
    return pl.pallas_call(
        _merge_body,
        grid=(n // tn, m // tm),
        in_specs=[
            pl.BlockSpec((tm, k), lambda j, i: (i, 0)),
            pl.BlockSpec((tm, k), lambda j, i: (i, 0)),
            wspec(0), wspec(1),
            pl.BlockSpec((tm, tn), lambda j, i: (i, g0_blk + j)),
            pl.BlockSpec((tm, tn), lambda j, i: (i, g1_blk + j)),
        ],
        out_specs=pl.BlockSpec((tm, tn), lambda j, i: (i, j)),
        out_shape=jax.ShapeDtypeStruct((m, n), out_dtype),
        scratch_shapes=[pltpu.VMEM((k, tn), BF16), pltpu.VMEM((k, tn), BF16)],
        compiler_params=_params(),
        name="merge",
    )(a0, a1, w_branch, w_branch, proj, proj)


def _out_norm_body(a_ref, w_ref, x_ref, g_ref, x1_ref, hn_ref, wb):
    _cast_weight_once(pl.program_id(0), w_ref, wb)
    x1 = x_ref[...] + _dot(a_ref[...].astype(BF16), wb[...])
    x1_ref[...] = x1
    ms = jnp.mean(x1 * x1, axis=-1, keepdims=True)
    hn_ref[...] = ((x1 * lax.rsqrt(ms + EPS)) * g_ref[...]).astype(hn_ref.dtype)


def _out_norm(a, w, l, x, g, hn_dtype):
    m, k = a.shape
    n = w.shape[2]
    tm = _pick(m, (256, 128, 8))
    return pl.pallas_call(
        _out_norm_body,
        grid=(m // tm,),
        in_specs=[
            pl.BlockSpec((tm, k), lambda i: (i, 0)),
            pl.BlockSpec((None, k, n), lambda i: (l, 0, 0), pipeline_mode=pl.Buffered(1)),
            pl.BlockSpec((tm, n), lambda i: (i, 0)),
            pl.BlockSpec((1, n), lambda i: (0, 0)),
        ],
        out_specs=[pl.BlockSpec((tm, n), lambda i: (i, 0)), pl.BlockSpec((tm, n), lambda i: (i, 0))],
        out_shape=[jax.ShapeDtypeStruct((m, n), F32), jax.ShapeDtypeStruct((m, n), hn_dtype)],
        scratch_shapes=[pltpu.VMEM((k, n), BF16)],
        compiler_params=_params(),
        name="out_proj_norm",
    )(a, w, x, g.reshape(1, n))


def _ffn_up_body(a_ref, wg_ref, wu_ref, o_ref, wgb, wub):
    _cast_weight_once(pl.program_id(1), wg_ref, wgb)
    _cast_weight_once(pl.program_id(1), wu_ref, wub)
    a = a_ref[...].astype(BF16)
    gate = _dot(a, wgb[...])
    up = _dot(a, wub[...])
    o_ref[...] = ((gate * _sigmoid(gate)) * up).astype(o_ref.dtype)


def _ffn_up(a, w_up, l, out_dtype):
    m, k = a.shape
    d_ff = w_up.shape[2] // 2
    tm = _pick(m, (1024, 512, 256, 128, 8))
    tn = _pick(d_ff, (512, 256, 128))
    up_blk = d_ff // tn
    return pl.pallas_call(
        _ffn_up_body,
        grid=(d_ff // tn, m // tm),
        in_specs=[
            pl.BlockSpec((tm, k), lambda j, i: (i, 0)),
            pl.BlockSpec((None, k, tn), lambda j, i: (l, 0, j)),
            pl.BlockSpec((None, k, tn), lambda j, i: (l, 0, up_blk + j)),
        ],
        out_specs=pl.BlockSpec((tm, tn), lambda j, i: (i, j)),
        out_shape=jax.ShapeDtypeStruct((m, d_ff), out_dtype),
        scratch_shapes=[pltpu.VMEM((k, tn), BF16), pltpu.VMEM((k, tn), BF16)],
        compiler_params=_params(),
        name="ffn_up",
    )(a, w_up, w_up)


def _ffn_down_body(a_ref, w_ref, x_ref, o_ref, wb):
    _cast_weight_once(pl.program_id(1), w_ref, wb)
    o_ref[...] = x_ref[...] + _dot(a_ref[...].astype(BF16), wb[...])


def _ffn_down(a, w, l, x):
    m, k = a.shape
    n = w.shape[2]
    tm = _pick(m, (256, 128, 8))
    tn = _pick(n, (1024, 512, 256, 128))
    return pl.pallas_call(
        _ffn_down_body,
        grid=(n // tn, m // tm),
        in_specs=[
            pl.BlockSpec((tm, k), lambda j, i: (i, 0)),
            pl.BlockSpec((None, k, tn), lambda j, i: (l, 0, j), pipeline_mode=pl.Buffered(1)),
            pl.BlockSpec((tm, tn), lambda j, i: (i, j)),
        ],
        out_specs=pl.BlockSpec((tm, tn), lambda j, i: (i, j)),
        out_shape=jax.ShapeDtypeStruct((m, n), F32),
        scratch_shapes=[pltpu.VMEM((k, tn), BF16)],
        compiler_params=_params(),
        name="ffn_down",
    )(a, w, x)


def _rope_tables(pos):
    def tables(dim):
        half = dim // 2
        freqs = ROPE_THETA ** (-jnp.arange(half, dtype=F32) / half)
        ang = pos.astype(F32)[:, None] * freqs[None, :]
        cos, sin = jnp.cos(ang), jnp.sin(ang)
        reps = LANES // dim
        cos_t = jnp.tile(jnp.concatenate([cos, cos], axis=1), (1, reps))
        sin_t = jnp.tile(jnp.concatenate([-sin, sin], axis=1), (1, reps))
        return cos_t, sin_t
    return tables(HEAD_DIM) + tables(IDX_DIM)


def _rope_head(x, cos_t, sin_t):
    return x * cos_t + pltpu.roll(x, HEAD_DIM // 2, axis=1) * sin_t


def _rope_idx(x, cos_t, sin_t, first_half):
    half = IDX_DIM // 2
    partner = jnp.where(first_half, pltpu.roll(x, LANES - half, axis=1), pltpu.roll(x, half, axis=1))
    return x * cos_t + partner * sin_t


def _head_norm(x, g):
    ms = jnp.mean(x * x, axis=-1, keepdims=True)
    return (x * lax.rsqrt(ms + EPS)) * g


def _qk_post_body(q_ref, k_ref, v_ref, qi_ref, kiw_ref, ch_ref, sh_ref, ci_ref, si_ref, qn_ref, kn_ref,
                  qo_ref, ko_ref, kb_ref, vo_ref, vb_ref, qio_ref, kio_ref, kilo_ref, kihi_ref):
    ch, sh, ci, si = ch_ref[...], sh_ref[...], ci_ref[...], si_ref[...]
    tm = ch.shape[0]
    lane = lax.broadcasted_iota(jnp.int32, ch.shape, 1)
    first_half = (lane & (IDX_DIM - 1)) < (IDX_DIM // 2)
    for h in range(N_HEADS):
        sl = slice(h * HEAD_DIM, (h + 1) * HEAD_DIM)
        qo_ref[:, sl] = _rope_head(_head_norm(q_ref[:, sl], qn_ref[...]), ch, sh).astype(qo_ref.dtype)
    for h in range(N_KV_HEADS):
        sl = slice(h * HEAD_DIM, (h + 1) * HEAD_DIM)
        kr = _rope_head(_head_norm(k_ref[:, sl], kn_ref[...]), ch, sh)
        ko_ref[pl.ds(h, tm, stride=N_KV_HEADS), :] = kr
        vo_ref[pl.ds(h, tm, stride=N_KV_HEADS), :] = v_ref[:, sl]
        kb_ref[:, sl] = kr.astype(kb_ref.dtype)
    vb_ref[...] = v_ref[...].astype(vb_ref.dtype)
    for h in range(IDX_W // LANES):
        sl = slice(h * LANES, (h + 1) * LANES)
        qio_ref[:, sl] = _rope_idx(qi_ref[:, sl], ci, si, first_half).astype(qio_ref.dtype)
    kir = _rope_idx(kiw_ref[...], ci, si, first_half)
    kio_ref[...] = kir[:, :IDX_DIM]
    zero = jnp.zeros_like(kir)
    kilo_ref[...] = jnp.where(lane < IDX_DIM, kir, zero).astype(kilo_ref.dtype)
    kihi_ref[...] = jnp.where(lane < IDX_DIM, zero, pltpu.roll(kir, IDX_DIM, axis=1)).astype(kihi_ref.dtype)


def _qk_post(proj, ki_col, tabs, tab_rows, q_norm, k_norm, act_dtype):
    m = proj.shape[0]
    tm = _pick(min(m, tab_rows), (512, 256, 128, 8))
    n_tab = tab_rows // tm
    col = lambda off, w: (lambda i: (i, off // w))
    tab_spec = pl.BlockSpec((tm, LANES), lambda i: (i % n_tab, 0))
    row = lambda w: pl.BlockSpec((tm, w), lambda i: (i, 0))
    head_rows = pl.BlockSpec((tm * N_KV_HEADS, HEAD_DIM), lambda i: (i, 0))
    ki_blk = ki_col // LANES
    outs = pl.pallas_call(
        _qk_post_body,
        grid=(m // tm,),
        in_specs=[
            pl.BlockSpec((tm, ATT_W), col(COL_Q, ATT_W)),
            pl.BlockSpec((tm, KV_W), col(COL_K, KV_W)),
            pl.BlockSpec((tm, KV_W), col(COL_V, KV_W)),
            pl.BlockSpec((tm, IDX_W), col(COL_QI, IDX_W)),
            pl.BlockSpec((tm, LANES), lambda i: (i, ki_blk)),
            tab_spec, tab_spec, tab_spec, tab_spec,
            pl.BlockSpec((1, HEAD_DIM), lambda i: (0, 0)),
            pl.BlockSpec((1, HEAD_DIM), lambda i: (0, 0)),
        ],
        out_specs=[row(ATT_W), head_rows, row(KV_W), head_rows, row(KV_W), row(IDX_W), row(IDX_DIM),
                   row(LANES), row(LANES)],
        out_shape=[
            jax.ShapeDtypeStruct((m, ATT_W), act_dtype),
            jax.ShapeDtypeStruct((m * N_KV_HEADS, HEAD_DIM), F32),
            jax.ShapeDtypeStruct((m, KV_W), act_dtype),
            jax.ShapeDtypeStruct((m * N_KV_HEADS, HEAD_DIM), F32),
            jax.ShapeDtypeStruct((m, KV_W), act_dtype),
            jax.ShapeDtypeStruct((m, IDX_W), act_dtype),
            jax.ShapeDtypeStruct((m, IDX_DIM), F32),
            jax.ShapeDtypeStruct((m, LANES), act_dtype),
            jax.ShapeDtypeStruct((m, LANES), act_dtype),
        ],
        compiler_params=_params(),
        name="qk_post",
    )(proj, proj, proj, proj, proj, *tabs, q_norm.reshape(1, HEAD_DIM), k_norm.reshape(1, HEAD_DIM))
    return outs


def _strict_upper(n):
    r = lax.broadcasted_iota(jnp.int32, (n, n), 0)
    c = lax.broadcasted_iota(jnp.int32, (n, n), 1)
    return jnp.where(r < c, 1.0, 0.0).astype(BF16)


def _strict_lower(n):
    r = lax.broadcasted_iota(jnp.int32, (n, n), 0)
    c = lax.broadcasted_iota(jnp.int32, (n, n), 1)
    return jnp.where(c < r, 1.0, 0.0).astype(BF16)


def _attn_prompt_body(q_ref, qi_ref, kiw_ref, k_ref, v_ref, kilo_ref, kihi_ref, o_ref,
                      key_scr, bias_scr, lg_scr, m_scr, l_scr, acc_scr, *, topk):
    i = pl.program_id(1)
    tq = q_ref.shape[0]
    wc = key_scr.shape[1]
    n_ch = i + 1
    wi_t = jnp.transpose(kiw_ref[...])
    q_pos = i * tq + lax.broadcasted_iota(jnp.int32, (wc, tq), 1)
    k_off = lax.broadcasted_iota(jnp.int32, (wc, tq), 0)

    def key_count(mask):
        return jnp.sum(jnp.where(mask, 1.0, 0.0), axis=0, keepdims=True)

    def score_chunk(kc, carry):
        r0 = pl.multiple_of(kc * wc, wc)
        ki_lo = kilo_ref[pl.ds(r0, wc), :]
        ki_hi = kihi_ref[pl.ds(r0, wc), :]
        score = jnp.zeros((wc, tq), F32)
        for pair in range(N_IDX_HEADS // 2):
            x = qi_ref[:, pair * LANES:(pair + 1) * LANES]
            w_lo = wi_t[IDX_DIM + 2 * pair:IDX_DIM + 2 * pair + 1, :]
            w_hi = wi_t[IDX_DIM + 2 * pair + 1:IDX_DIM + 2 * pair + 2, :]
            score = score + w_lo * jnp.maximum(_nt_dot(ki_lo, x), 0.0)
            score = score + w_hi * jnp.maximum(_nt_dot(ki_hi, x), 0.0)
        key_scr[kc] = jnp.where(kc * wc + k_off <= q_pos, _sort_key(score * IDX_SCALE), INT_MIN)
        return carry

    lax.fori_loop(0, n_ch, score_chunk, 0)

    def search(it, res):
        trial = res + jnp.left_shift(jnp.int32(1), 31 - it)

        def count_chunk(kc, acc):
            hit = jnp.where(key_scr[kc] >= trial, 1.0, 0.0)
            for j in range(wc // SUBLANES):
                acc = acc + hit[j * SUBLANES:(j + 1) * SUBLANES, :]
            return acc

        acc = lax.fori_loop(0, n_ch, count_chunk, jnp.zeros((SUBLANES, tq), F32))
        return jnp.where(jnp.sum(acc, axis=0, keepdims=True) >= float(topk), trial, res)

    thr = lax.fori_loop(0, 32, search, jnp.full((1, tq), INT_MIN, jnp.int32))

    def bias_chunk(kc, n_keep):
        key = key_scr[kc]
        keep = (key >= thr) & (key > KEY_NEG_INF)
        bias_scr[kc] = jnp.transpose(jnp.where(keep, 0.0, MASKED))
        return n_keep + key_count(keep)

    n_keep = lax.fori_loop(0, n_ch, bias_chunk, jnp.zeros((1, tq), F32))

    @pl.when(jnp.max(n_keep) > float(topk))
    def _():
        n_gt = lax.fori_loop(0, n_ch, lambda kc, n: n + key_count(key_scr[kc] > thr), jnp.zeros((1, tq), F32))
        need = float(topk) - n_gt
        lower = _strict_lower(wc)

        def tie_chunk(kc, before):
            key = key_scr[kc]
            eq = key == thr
            eqf = jnp.where(eq, 1.0, 0.0)
            rank = before + _dot(lower, eqf.astype(BF16))
            keep = ((key > thr) | (eq & (rank < need))) & (key > KEY_NEG_INF)
            bias_scr[kc] = jnp.transpose(jnp.where(keep, 0.0, MASKED))
            return before + jnp.sum(eqf, axis=0, keepdims=True)

        lax.fori_loop(0, n_ch, tie_chunk, jnp.zeros((1, tq), F32))

    m_scr[...] = jnp.full(m_scr.shape, MASKED, F32)
    l_scr[...] = jnp.zeros(l_scr.shape, F32)
    acc_scr[...] = jnp.zeros(acc_scr.shape, F32)
    lane_blocks = [slice(j * LANES, (j + 1) * LANES) for j in range(wc // LANES)]

    def logit_chunk(kc, carry):
        r0 = pl.multiple_of(kc * wc, wc)
        bias = bias_scr[kc]
        for h in range(N_HEADS):
            g = h // GQA_GROUP
            kg = k_ref[pl.ds(r0, wc), g * HEAD_DIM:(g + 1) * HEAD_DIM]
            lg = _nt_dot(q_ref[:, h * HEAD_DIM:(h + 1) * HEAD_DIM], kg) * ATT_SCALE + bias
            lg_scr[h, kc] = lg
            mx = m_scr[h]
            for lb in lane_blocks:
                mx = jnp.maximum(mx, lg[:, lb])
            m_scr[h] = mx
        return carry

    lax.fori_loop(0, n_ch, logit_chunk, 0)
    for h in range(N_HEADS):
        m_scr[h] = jnp.broadcast_to(jnp.max(m_scr[h], axis=1, keepdims=True), (tq, LANES))

    def prob_chunk(kc, carry):
        r0 = pl.multiple_of(kc * wc, wc)
        for h in range(N_HEADS):
            g = h // GQA_GROUP
            sl = slice(h * HEAD_DIM, (h + 1) * HEAD_DIM)
            vg = v_ref[pl.ds(r0, wc), g * HEAD_DIM:(g + 1) * HEAD_DIM]
            lg = lg_scr[h, kc]
            mx = m_scr[h]
            ps = [jnp.exp(lg[:, lb] - mx) for lb in lane_blocks]
            den = l_scr[h]
            for p in ps:
                den = den + p
            l_scr[h] = den
            acc_scr[:, sl] = acc_scr[:, sl] + _dot(jnp.concatenate(ps, axis=1).astype(BF16), vg)
        return carry

    lax.fori_loop(0, n_ch, prob_chunk, 0)
    for h in range(N_HEADS):
        sl = slice(h * HEAD_DIM, (h + 1) * HEAD_DIM)
        den = jnp.sum(l_scr[h], axis=1, keepdims=True)
        o_ref[:, sl] = (acc_scr[:, sl] / den).astype(o_ref.dtype)


def _attn_prompt(q, qi, proj, ki_col, k, v, ki_lo, ki_hi, nb, s):
    tq = _pick(s, (256, 128))
    nq = s // tq
    topk = min(TOPK_MAX, s // 4)
    ki_blk = ki_col // LANES
    qrow = lambda w: pl.BlockSpec((tq, w), lambda b, i: (b * nq + i, 0))
    full = lambda w: pl.BlockSpec((s, w), lambda b, i: (b, 0))
    return pl.pallas_call(
        functools.partial(_attn_prompt_body, topk=topk),
        grid=(nb, nq),
        in_specs=[qrow(ATT_W), qrow(IDX_W), pl.BlockSpec((tq, LANES), lambda b, i: (b * nq + i, ki_blk)),
                  full(KV_W), full(KV_W), full(LANES), full(LANES)],
        out_specs=qrow(ATT_W),
        out_shape=jax.ShapeDtypeStruct((nb * s, ATT_W), BF16),
        scratch_shapes=[pltpu.VMEM((nq, tq, tq), jnp.int32), pltpu.VMEM((nq, tq, tq), F32),
                        pltpu.VMEM((N_HEADS, nq, tq, tq), F32),
                        pltpu.VMEM((N_HEADS, tq, LANES), F32), pltpu.VMEM((N_HEADS, tq, LANES), F32),
                        pltpu.VMEM((tq, ATT_W), F32)],
        compiler_params=_params(),
        name="attn_prompt",
    )(q, qi, proj, k, v, ki_lo, ki_hi)


def _s5_discretize_body(are_ref, aim_ref, ldt_ref, bre_ref, bim_ref, abre_ref, abim_ref, bbre_ref, bbim_ref):
    dt = jnp.exp(ldt_ref[...])
    lr, li = are_ref[...], aim_ref[...]
    mag = jnp.exp(lr * dt)
    ab_re, ab_im = mag * jnp.cos(li * dt), mag * jnp.sin(li * dt)
    zr, zi = ab_re - 1.0, ab_im
    den = lr * lr + li * li
    fr = (zr * lr + zi * li) / den
    fi = (zi * lr - zr * li) / den
    abre_ref[...] = ab_re
    abim_ref[...] = ab_im
    br, bi = bre_ref[...], bim_ref[...]
    bbre_ref[...] = fr * br - fi * bi
    bbim_ref[...] = fr * bi + fi * br


def _s5_discretize(a_re, a_im, log_dt, b_re, b_im):
    depth = a_re.shape[0]
    gp = pl.BlockSpec((None, SSM_GROUPS, 1, SSM_P), lambda l: (l, 0, 0, 0))
    gwp = pl.BlockSpec((None, SSM_GROUPS, SSM_GROUP, SSM_P), lambda l: (l, 0, 0, 0))
    bt = lambda b: jnp.swapaxes(b, 2, 3)
    a4 = lambda a: a.reshape(depth, SSM_GROUPS, 1, SSM_P)
    return pl.pallas_call(
        _s5_discretize_body,
        grid=(depth,),
        in_specs=[gp, gp, pl.BlockSpec((None, SSM_GROUPS, 1, 1), lambda l: (l, 0, 0, 0)), gwp, gwp],
        out_specs=[gp, gp, gwp, gwp],
        out_shape=[jax.ShapeDtypeStruct((depth, SSM_GROUPS, 1, SSM_P), F32)] * 2
        + [jax.ShapeDtypeStruct((depth, SSM_GROUPS, SSM_GROUP, SSM_P), F32)] * 2,
        compiler_params=_params(),
        name="s5_discretize",
    )(a4(a_re), a4(a_im), log_dt.reshape(depth, SSM_GROUPS, 1, 1), bt(b_re), bt(b_im))


def _block_diag_in(bb_t):
    depth = bb_t.shape[0]
    gpb = SSM_GROUPS // SSM_BLOCKS
    x = bb_t.reshape(depth, SSM_BLOCKS, gpb, SSM_GROUP, SSM_P)
    eye = jnp.eye(gpb, dtype=bb_t.dtype)
    out = jnp.einsum('lkgwp,gh->lkgwhp', x, eye)
    return out.reshape(depth, SSM_BLOCKS, gpb * SSM_GROUP, gpb * SSM_P).astype(BF16)


def _block_diag_out(c):
    depth = c.shape[0]
    gpb = SSM_GROUPS // SSM_BLOCKS
    x = c.reshape(depth, SSM_BLOCKS, gpb, SSM_GROUP, SSM_P)
    eye = jnp.eye(gpb, dtype=c.dtype)
    out = jnp.einsum('lkgwp,gh->lkgphw', x, eye)
    return out.reshape(depth, SSM_BLOCKS, gpb * SSM_P, gpb * SSM_GROUP).astype(BF16)


def _s5_body(u_ref, bbre_ref, bbim_ref, cre_ref, cim_ref, abre_ref, abim_ref, d_ref, h0re_ref, h0im_ref,
             g_ref, htre_ref, htim_ref, bure, buim, hre, him, ust, gst, *, nb):
    c = pl.program_id(0)
    rows = bure.shape[0]
    n_t = rows // nb
    per_tile = SUBLANES // nb

    @pl.when(c == 0)
    def _():
        hre[...] = h0re_ref[...]
        him[...] = h0im_ref[...]

    for k in range(SSM_BLOCKS):
        ln = slice(k * LANES, (k + 1) * LANES)
        for b in range(nb):
            if n_t == 1:
                ust[k, b:b + 1, :] = u_ref[b, :, ln]
            else:
                ust[k, pl.ds(b, n_t, stride=nb), :] = u_ref[b, :, ln]
        uk = ust[k].astype(BF16)
        st = slice(k * SSM_BLOCK_STATE, (k + 1) * SSM_BLOCK_STATE)
        bure[:, st] = _dot(uk, bbre_ref[k])
        buim[:, st] = _dot(uk, bbim_ref[k])

    for cc in range(SSM_STATE // SCAN_LANES):
        sl = slice(cc * SCAN_LANES, (cc + 1) * SCAN_LANES)
        ar = jnp.broadcast_to(abre_ref[:, sl], (SUBLANES, SCAN_LANES))
        ai = jnp.broadcast_to(abim_ref[:, sl], (SUBLANES, SCAN_LANES))
        if per_tile == 1:
            def step(j, carry, sl=sl, ar=ar, ai=ai):
                hr, hi = carry
                r0 = pl.multiple_of(j * SUBLANES, SUBLANES)
                nr = ar * hr - ai * hi + bure[pl.ds(r0, SUBLANES), sl]
                ni = ar * hi + ai * hr + buim[pl.ds(r0, SUBLANES), sl]
                bure[pl.ds(r0, SUBLANES), sl] = nr
                buim[pl.ds(r0, SUBLANES), sl] = ni
                return nr, ni

            hr, hi = lax.fori_loop(0, rows // SUBLANES, step, (hre[:, sl], him[:, sl]))
            hre[:, sl] = hr
            him[:, sl] = hi
        else:
            assert per_tile == 2
            second = lax.broadcasted_iota(jnp.int32, (SUBLANES, SCAN_LANES), 0) >= nb
            m1r = jnp.where(second, ar * ar - ai * ai, ar)
            m1i = jnp.where(second, 2.0 * ar * ai, ai)
            m2r = jnp.where(second, ar, 0.0)
            m2i = jnp.where(second, ai, 0.0)

            def step(j, carry, sl=sl, m1r=m1r, m1i=m1i, m2r=m2r, m2i=m2i, second=second):
                cr, ci = carry
                r0 = pl.multiple_of(j * SUBLANES, SUBLANES)
                xr = bure[pl.ds(r0, SUBLANES), sl]
                xi = buim[pl.ds(r0, SUBLANES), sl]
                sr = pltpu.roll(xr, nb, axis=0)
                si = pltpu.roll(xi, nb, axis=0)
                nr = (m1r * cr - m1i * ci) + xr + (m2r * sr - m2i * si)
                ni = (m1r * ci + m1i * cr) + xi + (m2r * si + m2i * sr)
                bure[pl.ds(r0, SUBLANES), sl] = nr
                buim[pl.ds(r0, SUBLANES), sl] = ni
                return (jnp.where(second, nr, pltpu.roll(nr, nb, axis=0)),
                        jnp.where(second, ni, pltpu.roll(ni, nb, axis=0)))

            dup = lambda h: jnp.concatenate([h, h], axis=0)
            cr, ci = lax.fori_loop(0, rows // SUBLANES, step, (dup(hre[:, sl]), dup(him[:, sl])))
            hre[:, sl] = cr[:nb]
            him[:, sl] = ci[:nb]

    for k in range(SSM_BLOCKS):
        st = slice(k * SSM_BLOCK_STATE, (k + 1) * SSM_BLOCK_STATE)
        ln = slice(k * LANES, (k + 1) * LANES)
        y = _dot(bure[:, st].astype(BF16), cre_ref[k]) - _dot(buim[:, st].astype(BF16), cim_ref[k])
        y = y + d_ref[:, ln] * ust[k]
        gst[k] = _gelu_tanh(y)
        for b in range(nb):
            if n_t == 1:
                g_ref[b, :, ln] = gst[k, b:b + 1, :]
            else:
                g_ref[b, :, ln] = gst[k, pl.ds(b, n_t, stride=nb), :]

    @pl.when(c == pl.num_programs(0) - 1)
    def _():
        htre_ref[...] = hre[...]
        htim_ref[...] = him[...]


def _s5(proj, nb, s, l, bb_re, bb_im, c_re, c_im, ab_re, ab_im, d, h0_re, h0_im):
    n_t = min(s, 512 // nb)
    rows = n_t * nb
    blk4 = lambda a: pl.BlockSpec((None,) + a.shape[1:], lambda c: (l, 0, 0, 0))
    full2 = lambda a: pl.BlockSpec(a.shape, lambda c: (0, 0))
    vec = lambda a: a.reshape(1, -1)
    ab_re, ab_im, d = vec(ab_re), vec(ab_im), vec(d)
    u_spec = pl.BlockSpec((nb, n_t, SSM_W), lambda c: (0, c, 0))
    g, h_re, h_im = pl.pallas_call(
        functools.partial(_s5_body, nb=nb),
        grid=(s // n_t,),
        in_specs=[u_spec, blk4(bb_re), blk4(bb_im), blk4(c_re), blk4(c_im),
                  full2(ab_re), full2(ab_im), full2(d), full2(h0_re), full2(h0_im)],
        out_specs=[u_spec, full2(h0_re), full2(h0_im)],
        out_shape=[jax.ShapeDtypeStruct((nb, s, SSM_W), F32),
                   jax.ShapeDtypeStruct(h0_re.shape, F32), jax.ShapeDtypeStruct(h0_im.shape, F32)],
        scratch_shapes=[pltpu.VMEM((rows, SSM_STATE), F32), pltpu.VMEM((rows, SSM_STATE), F32),
                        pltpu.VMEM((nb, SSM_STATE), F32), pltpu.VMEM((nb, SSM_STATE), F32),
                        pltpu.VMEM((SSM_BLOCKS, rows, LANES), F32), pltpu.VMEM((SSM_BLOCKS, rows, LANES), F32)],
        compiler_params=_params(),
        name="s5_scan",
    )(proj.reshape(nb, s, proj.shape[1]), bb_re, bb_im, c_re, c_im, ab_re, ab_im, d, h0_re, h0_im)
    return g.reshape(nb * s, SSM_W), h_re, h_im


IDX_PAGES_PER_STEP = 16


def _sample_index_body(pt_ref, qi_ref, wi_ref, kin_ref, *rest, n_pg, topk):
    del pt_ref
    pages = rest[:n_pg]
    idx_ref, nsel_ref, bnew_ref, sc_scr = rest[n_pg:]
    c = pl.program_id(1)
    n_pages = sc_scr.shape[0]
    qi = qi_ref[...].astype(BF16)
    wi = wi_ref[...]

    def weighted(d):
        return jnp.sum(wi * jnp.maximum(d, 0.0), axis=0, keepdims=True) * IDX_SCALE

    for p in range(n_pg):
        sc_scr[pl.ds(c * n_pg + p, 1), :] = weighted(_dot(qi, pages[p][...].astype(BF16)))

    @pl.when(c == pl.num_programs(1) - 1)
    def _():
        kin = kin_ref[...].astype(BF16).astype(F32)
        d_new = jnp.sum(qi.astype(F32) * kin, axis=1, keepdims=True)
        key_new = _sort_key(jnp.broadcast_to(weighted(d_new), (1, LANES)))
        key = _sort_key(sc_scr[...])

        def count(mask, mask_new):
            per_lane = jnp.sum(jnp.where(mask, 1.0, 0.0), axis=0, keepdims=True)
            return jnp.sum(per_lane, axis=1, keepdims=True) + jnp.where(mask_new, 1.0, 0.0)

        def body(it, res):
            trial = res + jnp.left_shift(jnp.int32(1), 31 - it)
            return jnp.where(count(key >= trial, key_new >= trial) >= float(topk), trial, res)

        thr = lax.fori_loop(0, 32, body, jnp.full((1, LANES), INT_MIN, jnp.int32))
        need = float(topk) - count(key > thr, key_new > thr)
        eq = key == thr
        eqf = jnp.where(eq, 1.0, 0.0).astype(BF16)
        in_page = _dot(eqf, _strict_upper(LANES))
        lower = _strict_lower(n_pages)
        per_page = jnp.sum(jnp.where(eq, 1.0, 0.0), axis=1, keepdims=True)
        earlier = _dot(lower, jnp.broadcast_to(per_page, (n_pages, LANES)).astype(BF16))
        keep = ((key > thr) | (eq & (in_page + earlier < need))) & (key > KEY_NEG_INF)
        n_eq_past = jnp.sum(jnp.sum(jnp.where(eq, 1.0, 0.0), axis=0, keepdims=True), axis=1, keepdims=True)
        keep_new = ((key_new > thr) | ((key_new == thr) & (n_eq_past < need))) & (key_new > KEY_NEG_INF)
        bnew_ref[...] = jnp.where(keep_new, 0.0, MASKED)

        keepf = jnp.where(keep, 1.0, 0.0).astype(BF16)
        in_page_rank = jnp.where(keep, _dot(keepf, _strict_upper(LANES)), -1.0).astype(BF16)
        per_page = _nt_dot(jnp.ones((SUBLANES, LANES), BF16), keepf)[:1]
        pr = lax.broadcasted_iota(jnp.int32, (n_pages, n_pages), 0)
        pc = lax.broadcasted_iota(jnp.int32, (n_pages, n_pages), 1)
        upto = jnp.where(pr <= pc, 1.0, 0.0).astype(BF16)
        cum = _dot(jnp.broadcast_to(per_page, (SUBLANES, n_pages)).astype(BF16), upto)[:1]
        slot = lax.broadcasted_iota(jnp.int32, (topk, 1), 0).astype(F32)
        passed = cum <= slot
        page_of = jnp.sum(jnp.where(passed, 1.0, 0.0), axis=1, keepdims=True)
        start_of = jnp.sum(jnp.where(passed, per_page, 0.0), axis=1, keepdims=True)
        page_lane = lax.broadcasted_iota(jnp.int32, (topk, n_pages), 1).astype(F32)
        ranks = _dot(jnp.where(page_lane == page_of, 1.0, 0.0).astype(BF16), in_page_rank)
        lane = lax.broadcasted_iota(jnp.int32, (topk, LANES), 1).astype(F32)
        lane_of = jnp.sum(jnp.where(ranks == slot - start_of, lane, 0.0), axis=1, keepdims=True)
        n_kept = jnp.sum(per_page, axis=1, keepdims=True)
        idx_ref[...] = jnp.where(slot < n_kept, page_of * PAGE_SIZE + lane_of, 0.0).astype(jnp.int32)
        nsel_ref[...] = jnp.broadcast_to(n_kept, (1, LANES)).astype(jnp.int32)


def _sample_index(page_table, cache_kidx_t, layer, qi, wi, ki_new):
    nb, n_pages = page_table.shape
    n_pg = _pick(n_pages, (IDX_PAGES_PER_STEP, 8, 4, 2, 1))
    topk = min(TOPK_MAX, (n_pages * PAGE_SIZE + 1) // 4)
    page_spec = lambda p: pl.BlockSpec((None, None, IDX_DIM, PAGE_SIZE),
                                       lambda b, c, pt: (layer, pt[b, c * n_pg + p], 0, 0))
    per_b = lambda shape: pl.BlockSpec((None,) + shape, lambda b, c, pt: (b, 0, 0))
    grid_spec = pltpu.PrefetchScalarGridSpec(
        num_scalar_prefetch=1,
        grid=(nb, n_pages // n_pg),
        in_specs=[per_b((N_IDX_HEADS, IDX_DIM)), per_b((N_IDX_HEADS, 1)), per_b((1, IDX_DIM))]
        + [page_spec(p) for p in range(n_pg)],
        out_specs=[per_b((topk, 1)), per_b((1, LANES)), per_b((1, LANES))],
        scratch_shapes=[pltpu.VMEM((n_pages, LANES), F32)],
    )
    idx, n_sel, bias_new = pl.pallas_call(
        functools.partial(_sample_index_body, n_pg=n_pg, topk=topk),
        grid_spec=grid_spec,
        out_shape=[jax.ShapeDtypeStruct((nb, topk, 1), jnp.int32), jax.ShapeDtypeStruct((nb, 1, LANES), jnp.int32),
                   jax.ShapeDtypeStruct((nb, 1, LANES), F32)],
        compiler_params=_params(),
        name="sample_index",
    )(page_table, qi.reshape(nb, N_IDX_HEADS, IDX_DIM), wi.reshape(nb, N_IDX_HEADS, 1),
      ki_new.reshape(nb, 1, IDX_DIM), *([cache_kidx_t] * n_pg))
    return idx.reshape(nb, topk), n_sel[:, 0, 0], bias_new


SLOTS_PER_TILE = SUBLANES // N_KV_HEADS


def _group_rows(n_cols):
    head = lax.broadcasted_iota(jnp.int32, (N_HEADS, n_cols), 0)
    return [(head >= g * GQA_GROUP) & (head < (g + 1) * GQA_GROUP) for g in range(N_KV_HEADS)]


def _sample_attn_body(pt_ref, idx_ref, nsel_ref, q_ref, kn_ref, vn_ref, bnew_ref, odd_ref, ck_hbm, cv_hbm,
                      o_ref, kbuf, vbuf, sem, *, layer, topk):
    b = pl.program_id(0)

    def tile_copy(src_hbm, buf, which, phys, src_row, dst_row):
        return pltpu.make_async_copy(src_hbm.at[layer, phys, pl.ds(src_row, SUBLANES), :],
                                     buf.at[pl.ds(dst_row, SUBLANES), :], sem.at[which])

    def issue(r, carry):
        idx = idx_ref[b, r]
        phys = pt_ref[b, idx // PAGE_SIZE]
        src_row = pl.multiple_of(((idx % PAGE_SIZE) // SLOTS_PER_TILE) * SUBLANES, SUBLANES)
        dst_row = pl.multiple_of(r * SUBLANES, SUBLANES)
        tile_copy(ck_hbm, kbuf, 0, phys, src_row, dst_row).start()
        tile_copy(cv_hbm, vbuf, 1, phys, src_row, dst_row).start()
        return carry

    lax.fori_loop(0, topk, issue, 0)

    def drain(r, carry):
        tile_copy(ck_hbm, kbuf, 0, 0, 0, 0).wait()
        tile_copy(cv_hbm, vbuf, 1, 0, 0, 0).wait()
        return carry

    lax.fori_loop(0, topk, drain, 0)

    q = q_ref[...].astype(BF16)
    odd = odd_ref[...] == 1
    live = lax.broadcasted_iota(jnp.int32, (1, topk), 1) < nsel_ref[b]

    def gathered(buf, g):
        first = buf[pl.ds(g, topk, stride=SUBLANES), :]
        second = buf[pl.ds(N_KV_HEADS + g, topk, stride=SUBLANES), :]
        return jnp.where(odd, second, first).astype(BF16)

    in_group = _group_rows(topk)
    lg = jnp.zeros((N_HEADS, topk), F32)
    for g in range(N_KV_HEADS):
        lg = jnp.where(in_group[g], _nt_dot(q, gathered(kbuf, g)), lg)
    lg = jnp.where(live, lg * ATT_SCALE, MASKED)
    qf = q.astype(F32)
    kn = kn_ref[...].astype(BF16).astype(F32)
    vn = vn_ref[...].astype(BF16).astype(F32)
    lg_new = jnp.sum(qf * kn, axis=1, keepdims=True) * ATT_SCALE + bnew_ref[:, :1]
    mx = jnp.maximum(jnp.max(lg, axis=1, keepdims=True), lg_new)
    pr = jnp.where(live, jnp.exp(lg - mx), 0.0)
    pr_new = jnp.where(bnew_ref[:, :1] == 0.0, jnp.exp(lg_new - mx), 0.0)
    den = jnp.sum(pr, axis=1, keepdims=True) + pr_new
    prb = pr.astype(BF16)
    in_group_d = _group_rows(HEAD_DIM)
    pv = jnp.zeros((N_HEADS, HEAD_DIM), F32)
    for g in range(N_KV_HEADS):
        pv = jnp.where(in_group_d[g], _dot(prb, gathered(vbuf, g)), pv)
    o_ref[...] = ((pv + pr_new.astype(BF16).astype(F32) * vn) / den).astype(o_ref.dtype)


def _sample_attn(page_table, cache_k2, cache_v2, layer, idx, n_sel, bias_new, q, k_new, v_new):
    nb = page_table.shape[0]
    topk = idx.shape[1]
    per_b = lambda shape: pl.BlockSpec((None,) + shape, lambda b, pt, ix, ns: (b, 0, 0))
    per_head = lambda a: jnp.repeat(a.reshape(nb, N_KV_HEADS, HEAD_DIM), GQA_GROUP, axis=1)
    odd = (idx % SLOTS_PER_TILE).reshape(nb, topk, 1)
    grid_spec = pltpu.PrefetchScalarGridSpec(
        num_scalar_prefetch=3,
        grid=(nb,),
        in_specs=[per_b((N_HEADS, HEAD_DIM)), per_b((N_HEADS, HEAD_DIM)), per_b((N_HEADS, HEAD_DIM)),
                  per_b((1, LANES)), per_b((topk, 1)),
                  pl.BlockSpec(memory_space=pl.ANY), pl.BlockSpec(memory_space=pl.ANY)],
        out_specs=per_b((N_HEADS, HEAD_DIM)),
        scratch_shapes=[pltpu.VMEM((topk * SUBLANES, HEAD_DIM), F32), pltpu.VMEM((topk * SUBLANES, HEAD_DIM), F32),
                        pltpu.SemaphoreType.DMA((2,))],
    )
    out = pl.pallas_call(
        functools.partial(_sample_attn_body, layer=layer, topk=topk),
        grid_spec=grid_spec,
        out_shape=jax.ShapeDtypeStruct((nb, N_HEADS, HEAD_DIM), F32),
        compiler_params=_params(),
        name="sample_attn",
    )(page_table, idx, n_sel, q.reshape(nb, N_HEADS, HEAD_DIM), per_head(k_new), per_head(v_new), bias_new, odd,
      cache_k2, cache_v2)
    return out.reshape(nb, ATT_W)


def _layer_tail(proj, g, y_att, x, w, l, act_dtype):
    z = _glu(g, w["w_glu"], l, act_dtype)
    merged = _merge(z, y_att, w["w_branch"], l, proj, act_dtype)
    x1, hn = _out_norm(merged, w["w_out"], l, x, w["norm_ffn"][l], act_dtype)
    act = _ffn_up(hn, w["w_ffn_up"], l, act_dtype)
    return _ffn_down(act, w["w_ffn_down"], l, x1)


def _s5_layer(proj, nb, s, w, l, h0_re, h0_im):
    return _s5(proj, nb, s, l, w["bb_re"], w["bb_im"], w["c_re"], w["c_im"], w["ab_re"][l], w["ab_im"][l],
               w["ssm_d"][l], h0_re, h0_im)


def _prompt_layer(x, nb, s, tabs, w, l):
    xn = _rmsnorm(x, w["norm_mix"][l], BF16)
    proj = _in_proj(xn, w["w_in"], l, PROJ_TN, F32)
    q, k_out, k_bf, v_out, v_bf, qi, ki_out, ki_lo, ki_hi = _qk_post(
        proj, w["ki_col"], tabs, s, w["q_norm"][l], w["k_norm"][l], BF16)
    y_att = _attn_prompt(q, qi, proj, w["ki_col"], k_bf, v_bf, ki_lo, ki_hi, nb, s)
    zeros = jnp.zeros((nb, SSM_STATE), F32)
    g, h_re, h_im = _s5_layer(proj, nb, s, w, l, zeros, zeros)
    x2 = _layer_tail(proj, g, y_att, x, w, l, BF16)
    return x2, k_out, v_out, ki_out, h_re, h_im


def _sample_layer(x, nb, tabs, w, l, page_table, cache_k2, cache_v2, cache_kidx_t, h0_re, h0_im):
    xn = _rmsnorm(x, w["norm_mix"][l], F32)
    proj = _in_proj(xn, w["w_in"], l, PROJ_TN, F32)
    ki_col = w["ki_col"]
    q, k_out, _, v_out, _, qi, ki_out, _, _ = _qk_post(proj, ki_col, tabs, nb, w["q_norm"][l], w["k_norm"][l], F32)
    wi = proj[:, ki_col + IDX_DIM:ki_col + IDX_DIM + N_IDX_HEADS]
    idx, n_sel, bias_new = _sample_index(page_table, cache_kidx_t, l, qi, wi, ki_out)
    y_att = _sample_attn(page_table, cache_k2, cache_v2, l, idx, n_sel, bias_new, q, k_out, v_out)
    g, h_re, h_im = _s5_layer(proj, nb, 1, w, l, h0_re, h0_im)
    x2 = _layer_tail(proj, g, y_att, x, w, l, F32)
    return x2, k_out, v_out, ki_out, h_re, h_im


def kernel(x_prompt, x_sample, cache_k, cache_v, cache_kidx, state_ssm_re, state_ssm_im, page_table, norm_mix, w_in, q_norm, k_norm, ssm_a_re, ssm_a_im, ssm_log_dt, ssm_b_re, ssm_b_im, ssm_c_re, ssm_c_im, ssm_d, w_glu, w_branch, w_out, norm_ffn, w_ffn_up, w_ffn_down):
    nb_p, s, d_model = x_prompt.shape
    nb_s, dec_seq, _ = x_sample.shape
    assert dec_seq == 1, "the sample group decodes one token per sequence"
    depth = w_in.shape[0]
    past = page_table.shape[1] * PAGE_SIZE

    ki_orig = COL_GATE
    gate_orig = ki_orig + IDX_DIM + N_IDX_HEADS
    n_gate = w_in.shape[2] - gate_orig
    ki_col = COL_GATE + n_gate
    assert ki_col % LANES == 0
    cols = ki_col + LANES
    cols_pad = -(-cols // PROJ_TN) * PROJ_TN
    w_in_r = jnp.concatenate(
        [w_in[:, :, :COL_GATE], w_in[:, :, gate_orig:], w_in[:, :, ki_orig:gate_orig],
         jnp.zeros((depth, d_model, cols_pad - ki_col - IDX_DIM - N_IDX_HEADS), w_in.dtype)], axis=2).astype(BF16)

    ab_re, ab_im, bbt_re, bbt_im = _s5_discretize(ssm_a_re, ssm_a_im, ssm_log_dt, ssm_b_re, ssm_b_im)
    bb_re, bb_im = _block_diag_in(bbt_re), _block_diag_in(bbt_im)
    c_re, c_im = _block_diag_out(ssm_c_re), _block_diag_out(ssm_c_im)

    tabs_p = _rope_tables(jnp.arange(s, dtype=jnp.int32))
    tabs_s = _rope_tables(jnp.full((nb_s,), past, dtype=jnp.int32))

    xp = x_prompt.reshape(nb_p * s, d_model)
    xs = x_sample.reshape(nb_s, d_model)
    cache_k2 = cache_k.reshape(cache_k.shape[:2] + (PAGE_SIZE * N_KV_HEADS, HEAD_DIM))
    cache_v2 = cache_v.reshape(cache_v.shape[:2] + (PAGE_SIZE * N_KV_HEADS, HEAD_DIM))
    cache_kidx_t = jnp.swapaxes(cache_kidx, 2, 3)
    w = dict(ki_col=ki_col, norm_mix=norm_mix, w_in=w_in_r, q_norm=q_norm, k_norm=k_norm,
             bb_re=bb_re, bb_im=bb_im, c_re=c_re, c_im=c_im, ab_re=ab_re, ab_im=ab_im, ssm_d=ssm_d,
             w_glu=w_glu, w_branch=w_branch, w_out=w_out, norm_ffn=norm_ffn,
             w_ffn_up=w_ffn_up, w_ffn_down=w_ffn_down)
    outs_p, outs_s = [], []
    for l in range(depth):
        xp, *rest_p = _prompt_layer(xp, nb_p, s, tabs_p, w, l)
        xs, *rest_s = _sample_layer(xs, nb_s, tabs_s, w, l, page_table, cache_k2, cache_v2, cache_kidx_t,
                                    state_ssm_re[l].reshape(nb_s, SSM_STATE), state_ssm_im[l].reshape(nb_s, SSM_STATE))
        outs_p.append(rest_p)
        outs_s.append(rest_s)

    def stack(outs, idx, shape):
        return jnp.stack([o[idx].reshape(shape) for o in outs])

    kv_p, kv_s = (nb_p, s, N_KV_HEADS, HEAD_DIM), (nb_s, dec_seq, N_KV_HEADS, HEAD_DIM)
    st_p, st_s = (nb_p, SSM_GROUPS, SSM_P), (nb_s, SSM_GROUPS, SSM_P)
    return (xp.reshape(nb_p, s, d_model), xs.reshape(nb_s, dec_seq, d_model),
            stack(outs_p, 0, kv_p), stack(outs_p, 1, kv_p), stack(outs_p, 2, (nb_p, s, IDX_DIM)),
            stack(outs_p, 3, st_p), stack(outs_p, 4, st_p),
            stack(outs_s, 0, kv_s), stack(outs_s, 1, kv_s), stack(outs_s, 2, (nb_s, dec_seq, IDX_DIM)),
            stack(outs_s, 3, st_s), stack(outs_s, 4, st_s))
```

```python
import functools
import math

import jax
import jax.numpy as jnp
from jax import lax
from jax.experimental import pallas as pl
from jax.experimental.pallas import tpu as pltpu

F32 = jnp.float32
BF16 = jnp.bfloat16

SSM_W = 1024
SSM_GROUP = 16
SSM_GROUPS = SSM_W // SSM_GROUP
SSM_P = 64
SSM_STATE = SSM_GROUPS * SSM_P
N_HEADS = 8
N_KV_HEADS = 4
HEAD_DIM = 128
GQA_GROUP = N_HEADS // N_KV_HEADS
ATT_W = N_HEADS * HEAD_DIM
KV_W = N_KV_HEADS * HEAD_DIM
N_IDX_HEADS = 16
IDX_DIM = 64
IDX_W = N_IDX_HEADS * IDX_DIM
IDX_SCALE = (IDX_DIM * N_IDX_HEADS) ** -0.5
ATT_SCALE = HEAD_DIM ** -0.5
TOPK_MAX = 256
PAGE_SIZE = 128
ROPE_THETA = 10000.0
EPS = 1e-6

LANES = 128
SUBLANES = 8
VMEM_LIMIT = 56 * 1024 * 1024

COL_U = 0
COL_Q = SSM_W
COL_K = COL_Q + ATT_W
COL_V = COL_K + KV_W
COL_QI = COL_V + KV_W
COL_GATE = COL_QI + IDX_W
PROJ_TN = 768

MASKED = -1e30
INT_MIN = -(2 ** 31)
KEY_NEG_INF = -2139095041

SCAN_LANES = 512
SSM_BLOCKS = SSM_W // LANES
SSM_BLOCK_STATE = SSM_STATE // SSM_BLOCKS


def _params(**kw):
    return pltpu.CompilerParams(vmem_limit_bytes=VMEM_LIMIT, **kw)


def _pick(n, cands):
    for c in cands:
        if n % c == 0:
            return c
    return n


def _nt_dot(a, b):
    return lax.dot_general(a, b, (((1,), (1,)), ((), ())), preferred_element_type=F32)


def _dot(a, b):
    return jnp.dot(a, b, preferred_element_type=F32)


def _sigmoid(x):
    return 1.0 / (1.0 + jnp.exp(-x))


def _gelu_tanh(x):
    c = math.sqrt(2.0 / math.pi)
    return 0.5 * x * (1.0 + jnp.tanh(c * (x + 0.044715 * (x * x * x))))


def _sort_key(s):
    bits = pltpu.bitcast(s, jnp.int32)
    return bits ^ ((bits >> 31) & jnp.int32(0x7FFFFFFF))


def _rmsnorm_body(x_ref, g_ref, o_ref):
    x = x_ref[...].astype(F32)
    ms = jnp.mean(x * x, axis=-1, keepdims=True)
    o_ref[...] = ((x * lax.rsqrt(ms + EPS)) * g_ref[...]).astype(o_ref.dtype)


def _rmsnorm(x, g, out_dtype):
    m, d = x.shape
    tm = _pick(m, (512, 256, 128, 8))
    return pl.pallas_call(
        _rmsnorm_body,
        grid=(m // tm,),
        in_specs=[pl.BlockSpec((tm, d), lambda i: (i, 0)), pl.BlockSpec((1, d), lambda i: (0, 0))],
        out_specs=pl.BlockSpec((tm, d), lambda i: (i, 0)),
        out_shape=jax.ShapeDtypeStruct((m, d), out_dtype),
        compiler_params=_params(),
        name="rmsnorm",
    )(x, g.reshape(1, d))


def _mm_body(a_ref, w_ref, o_ref):
    o_ref[...] = _nt_dot(a_ref[...].astype(BF16), w_ref[...]).astype(o_ref.dtype)


def _in_proj(a, w, l, tn, out_dtype):
    m, k = a.shape
    n = w.shape[1]
    tm = _pick(m, (2048, 1024, 512, 256, 128, 8))
    return pl.pallas_call(
        _mm_body,
        grid=(n // tn, m // tm),
        in_specs=[pl.BlockSpec((tm, k), lambda j, i: (i, 0)), pl.BlockSpec((None, tn, k), lambda j, i: (l, j, 0))],
        out_specs=pl.BlockSpec((tm, tn), lambda j, i: (i, j)),
        out_shape=jax.ShapeDtypeStruct((m, n), out_dtype),
        compiler_params=_params(),
        name="in_proj",
    )(a, w)


def _cast_weight_once(step, w_ref, wb_scr):
    @pl.when(step == 0)
    def _():
        wb_scr[...] = w_ref[...].astype(BF16)


def _glu_body(g_ref, w_ref, o_ref, wb):
    _cast_weight_once(pl.program_id(0), w_ref, wb)
    g = g_ref[...]
    z = _dot(g.astype(BF16), wb[...])
    o_ref[...] = (g * _sigmoid(z)).astype(o_ref.dtype)


def _glu(g, w, l, out_dtype):
    m, n = g.shape
    tm = _pick(m, (512, 256, 128, 8))
    return pl.pallas_call(
        _glu_body,
        grid=(m // tm,),
        in_specs=[pl.BlockSpec((tm, n), lambda i: (i, 0)),
                  pl.BlockSpec((None, n, n), lambda i: (l, 0, 0), pipeline_mode=pl.Buffered(1))],
        out_specs=pl.BlockSpec((tm, n), lambda i: (i, 0)),
        out_shape=jax.ShapeDtypeStruct((m, n), out_dtype),
        scratch_shapes=[pltpu.VMEM((n, n), BF16)],
        compiler_params=_params(),
        name="glu",
    )(g, w)


def _merge_body(a0_ref, a1_ref, w0_ref, w1_ref, g0_ref, g1_ref, o_ref, wb0, wb1):
    _cast_weight_once(pl.program_id(1), w0_ref, wb0)
    _cast_weight_once(pl.program_id(1), w1_ref, wb1)
    y0 = _dot(a0_ref[...].astype(BF16), wb0[...])
    y1 = _dot(a1_ref[...].astype(BF16), wb1[...])
    o_ref[...] = (_sigmoid(g0_ref[...]) * y0 + _sigmoid(g1_ref[...]) * y1).astype(o_ref.dtype)


def _merge(a0, a1, w_branch, l, proj, out_dtype):
    m, k = a0.shape
    n = w_branch.shape[3]
    tm = _pick(m, (512, 256, 128, 8))
    tn = _pick(n, (1024, 512, 256, 128))
    g0_blk = COL_GATE // tn
    g1_blk = (COL_GATE + n) // tn
    wspec = lambda br: pl.BlockSpec((None, None, k, tn), lambda j, i: (l, br, 0, j))
    return pl.pallas_call(
        _merge_body,
        grid=(n // tn, m // tm),
        in_specs=[
            pl.BlockSpec((tm, k), lambda j, i: (i, 0)),
            pl.BlockSpec((tm, k), lambda j, i: (i, 0)),
            wspec(0), wspec(1),
            pl.BlockSpec((tm, tn), lambda j, i: (i, g0_blk + j)),
            pl.BlockSpec((tm, tn), lambda j, i: (i, g1_blk + j)),
        ],
        out_specs=pl.BlockSpec((tm, tn), lambda j, i: (i, j)),
        out_shape=jax.ShapeDtypeStruct((m, n), out_dtype),
        scratch_shapes=[pltpu.VMEM((k, tn), BF16), pltpu.VMEM((k, tn), BF16)],
        compiler_params=_params(),
        name="merge",
    )(a0, a1, w_branch, w_branch, proj, proj)


def _out_norm_body(a_ref, w_ref, x_ref, g_ref, x1_ref, hn_ref, wb):
    _cast_weight_once(pl.program_id(0), w_ref, wb)
    x1 = x_ref[...] + _dot(a_ref[...].astype(BF16), wb[...])
    x1_ref[...] = x1
    ms = jnp.mean(x1 * x1, axis=-1, keepdims=True)
    hn_ref[...] = ((x1 * lax.rsqrt(ms + EPS)) * g_ref[...]).astype(hn_ref.dtype)


def _out_norm(a, w, l, x, g, hn_dtype):
    m, k = a.shape
    n = w.shape[2]
    tm = _pick(m, (256, 128, 8))
    return pl.pallas_call(
        _out_norm_body,
        grid=(m // tm,),
        in_specs=[
            pl.BlockSpec((tm, k), lambda i: (i, 0)),
            pl.BlockSpec((None, k, n), lambda i: (l, 0, 0), pipeline_mode=pl.Buffered(1)),
            pl.BlockSpec((tm, n), lambda i: (i, 0)),
            pl.BlockSpec((1, n), lambda i: (0, 0)),
        ],
        out_specs=[pl.BlockSpec((tm, n), lambda i: (i, 0)), pl.BlockSpec((tm, n), lambda i: (i, 0))],
        out_shape=[jax.ShapeDtypeStruct((m, n), F32), jax.ShapeDtypeStruct((m, n), hn_dtype)],
        scratch_shapes=[pltpu.VMEM((k, n), BF16)],
        compiler_params=_params(),
        name="out_proj_norm",
    )(a, w, x, g.reshape(1, n))


def _ffn_up_body(a_ref, wg_ref, wu_ref, o_ref, wgb, wub):
    _cast_weight_once(pl.program_id(1), wg_ref, wgb)
    _cast_weight_once(pl.program_id(1), wu_ref, wub)
    a = a_ref[...].astype(BF16)
    gate = _dot(a, wgb[...])
    up = _dot(a, wub[...])
    o_ref[...] = ((gate * _sigmoid(gate)) * up).astype(o_ref.dtype)


def _ffn_up(a, w_up, l, out_dtype):
    m, k = a.shape
    d_ff = w_up.shape[2] // 2
    tm = _pick(m, (2048, 1024, 512, 256, 128, 8))
    tn = _pick(d_ff, (512, 256, 128))
    up_blk = d_ff // tn
    return pl.pallas_call(
        _ffn_up_body,
        grid=(d_ff // tn, m // tm),
        in_specs=[
            pl.BlockSpec((tm, k), lambda j, i: (i, 0)),
            pl.BlockSpec((None, k, tn), lambda j, i: (l, 0, j)),
            pl.BlockSpec((None, k, tn), lambda j, i: (l, 0, up_blk + j)),
        ],
        out_specs=pl.BlockSpec((tm, tn), lambda j, i: (i, j)),
        out_shape=jax.ShapeDtypeStruct((m, d_ff), out_dtype),
        scratch_shapes=[pltpu.VMEM((k, tn), BF16), pltpu.VMEM((k, tn), BF16)],
        compiler_params=_params(),
        name="ffn_up",
    )(a, w_up, w_up)


def _ffn_down_body(a_ref, w_ref, x_ref, o_ref, wb):
    _cast_weight_once(pl.program_id(1), w_ref, wb)
    o_ref[...] = x_ref[...] + _dot(a_ref[...].astype(BF16), wb[...])


def _ffn_down(a, w, l, x):
    m, k = a.shape
    n = w.shape[2]
    tm = _pick(m, (256, 128, 8))
    tn = _pick(n, (1024, 512, 256, 128))
    return pl.pallas_call(
        _ffn_down_body,
        grid=(n // tn, m // tm),
        in_specs=[
            pl.BlockSpec((tm, k), lambda j, i: (i, 0)),
            pl.BlockSpec((None, k, tn), lambda j, i: (l, 0, j), pipeline_mode=pl.Buffered(1)),
            pl.BlockSpec((tm, tn), lambda j, i: (i, j)),
        ],
        out_specs=pl.BlockSpec((tm, tn), lambda j, i: (i, j)),
        out_shape=jax.ShapeDtypeStruct((m, n), F32),
        scratch_shapes=[pltpu.VMEM((k, tn), BF16)],
        compiler_params=_params(),
        name="ffn_down",
    )(a, w, x)


def _rope_tables(pos):
    def tables(dim):
        half = dim // 2
        freqs = ROPE_THETA ** (-jnp.arange(half, dtype=F32) / half)
        ang = pos.astype(F32)[:, None] * freqs[None, :]
        cos, sin = jnp.cos(ang), jnp.sin(ang)
        reps = LANES // dim
        cos_t = jnp.tile(jnp.concatenate([cos, cos], axis=1), (1, reps))
        sin_t = jnp.tile(jnp.concatenate([-sin, sin], axis=1), (1, reps))
        return cos_t, sin_t
    return tables(HEAD_DIM) + tables(IDX_DIM)


def _rope_head(x, cos_t, sin_t):
    return x * cos_t + pltpu.roll(x, HEAD_DIM // 2, axis=1) * sin_t


def _rope_idx(x, cos_t, sin_t, first_half):
    half = IDX_DIM // 2
    partner = jnp.where(first_half, pltpu.roll(x, LANES - half, axis=1), pltpu.roll(x, half, axis=1))
    return x * cos_t + partner * sin_t


def _head_norm(x, g):
    ms = jnp.mean(x * x, axis=-1, keepdims=True)
    return (x * lax.rsqrt(ms + EPS)) * g


def _qk_post_body(q_ref, k_ref, v_ref, qi_ref, kiw_ref, ch_ref, sh_ref, ci_ref, si_ref, qn_ref, kn_ref,
                  qo_ref, ko_ref, kb_ref, vo_ref, vb_ref, qio_ref, kio_ref, kilo_ref, kihi_ref):
    ch, sh, ci, si = ch_ref[...], sh_ref[...], ci_ref[...], si_ref[...]
    tm = ch.shape[0]
    lane = lax.broadcasted_iota(jnp.int32, ch.shape, 1)
    first_half = (lane & (IDX_DIM - 1)) < (IDX_DIM // 2)
    for h in range(N_HEADS):
        sl = slice(h * HEAD_DIM, (h + 1) * HEAD_DIM)
        qo_ref[:, sl] = _rope_head(_head_norm(q_ref[:, sl], qn_ref[...]), ch, sh).astype(qo_ref.dtype)
    for h in range(N_KV_HEADS):
        sl = slice(h * HEAD_DIM, (h + 1) * HEAD_DIM)
        kr = _rope_head(_head_norm(k_ref[:, sl], kn_ref[...]), ch, sh)
        ko_ref[pl.ds(h, tm, stride=N_KV_HEADS), :] = kr
        vo_ref[pl.ds(h, tm, stride=N_KV_HEADS), :] = v_ref[:, sl]
        kb_ref[:, sl] = kr.astype(kb_ref.dtype)
    vb_ref[...] = v_ref[...].astype(vb_ref.dtype)
    for h in range(IDX_W // LANES):
        sl = slice(h * LANES, (h + 1) * LANES)
        qio_ref[:, sl] = _rope_idx(qi_ref[:, sl], ci, si, first_half).astype(qio_ref.dtype)
    kir = _rope_idx(kiw_ref[...], ci, si, first_half)
    kio_ref[...] = kir[:, :IDX_DIM]
    zero = jnp.zeros_like(kir)
    kilo_ref[...] = jnp.where(lane < IDX_DIM, kir, zero).astype(kilo_ref.dtype)
    kihi_ref[...] = jnp.where(lane < IDX_DIM, zero, pltpu.roll(kir, IDX_DIM, axis=1)).astype(kihi_ref.dtype)


def _qk_post(proj, ki_col, tabs, tab_rows, q_norm, k_norm, act_dtype):
    m = proj.shape[0]
    tm = _pick(min(m, tab_rows), (512, 256, 128, 8))
    n_tab = tab_rows // tm
    col = lambda off, w: (lambda i: (i, off // w))
    tab_spec = pl.BlockSpec((tm, LANES), lambda i: (i % n_tab, 0))
    row = lambda w: pl.BlockSpec((tm, w), lambda i: (i, 0))
    head_rows = pl.BlockSpec((tm * N_KV_HEADS, HEAD_DIM), lambda i: (i, 0))
    ki_blk = ki_col // LANES
    outs = pl.pallas_call(
        _qk_post_body,
        grid=(m // tm,),
        in_specs=[
            pl.BlockSpec((tm, ATT_W), col(COL_Q, ATT_W)),
            pl.BlockSpec((tm, KV_W), col(COL_K, KV_W)),
            pl.BlockSpec((tm, KV_W), col(COL_V, KV_W)),
            pl.BlockSpec((tm, IDX_W), col(COL_QI, IDX_W)),
            pl.BlockSpec((tm, LANES), lambda i: (i, ki_blk)),
            tab_spec, tab_spec, tab_spec, tab_spec,
            pl.BlockSpec((1, HEAD_DIM), lambda i: (0, 0)),
            pl.BlockSpec((1, HEAD_DIM), lambda i: (0, 0)),
        ],
        out_specs=[row(ATT_W), head_rows, row(KV_W), head_rows, row(KV_W), row(IDX_W), row(IDX_DIM),
                   row(LANES), row(LANES)],
        out_shape=[
            jax.ShapeDtypeStruct((m, ATT_W), act_dtype),
            jax.ShapeDtypeStruct((m * N_KV_HEADS, HEAD_DIM), F32),
            jax.ShapeDtypeStruct((m, KV_W), act_dtype),
            jax.ShapeDtypeStruct((m * N_KV_HEADS, HEAD_DIM), F32),
            jax.ShapeDtypeStruct((m, KV_W), act_dtype),
            jax.ShapeDtypeStruct((m, IDX_W), act_dtype),
            jax.ShapeDtypeStruct((m, IDX_DIM), F32),
            jax.ShapeDtypeStruct((m, LANES), act_dtype),
            jax.ShapeDtypeStruct((m, LANES), act_dtype),
        ],
        compiler_params=_params(),
        name="qk_post",
    )(proj, proj, proj, proj, proj, *tabs, q_norm.reshape(1, HEAD_DIM), k_norm.reshape(1, HEAD_DIM))
    return outs


def _strict_upper(n):
    r = lax.broadcasted_iota(jnp.int32, (n, n), 0)
    c = lax.broadcasted_iota(jnp.int32, (n, n), 1)
    return jnp.where(r < c, 1.0, 0.0).astype(BF16)


def _strict_lower(n):
    r = lax.broadcasted_iota(jnp.int32, (n, n), 0)
    c = lax.broadcasted_iota(jnp.int32, (n, n), 1)
    return jnp.where(c < r, 1.0, 0.0).astype(BF16)


def _attn_prompt_body(q_ref, qi_ref, kiw_ref, k_ref, v_ref, kilo_ref, kihi_ref, o_ref,
                      key_scr, bias_scr, lg_scr, m_scr, l_scr, acc_scr, *, topk):
    i = pl.program_id(1)
    tq = q_ref.shape[0]
    wc = key_scr.shape[1]
    n_ch = i + 1
    wi_t = jnp.transpose(kiw_ref[...])
    q_pos = i * tq + lax.broadcasted_iota(jnp.int32, (wc, tq), 1)
    k_off = lax.broadcasted_iota(jnp.int32, (wc, tq), 0)

    def key_count(mask):
        return jnp.sum(jnp.where(mask, 1.0, 0.0), axis=0, keepdims=True)

    def score_chunk(kc, carry):
        r0 = pl.multiple_of(kc * wc, wc)
        ki_lo = kilo_ref[pl.ds(r0, wc), :]
        ki_hi = kihi_ref[pl.ds(r0, wc), :]
        score = jnp.zeros((wc, tq), F32)
        for pair in range(N_IDX_HEADS // 2):
            x = qi_ref[:, pair * LANES:(pair + 1) * LANES]
            w_lo = wi_t[IDX_DIM + 2 * pair:IDX_DIM + 2 * pair + 1, :]
            w_hi = wi_t[IDX_DIM + 2 * pair + 1:IDX_DIM + 2 * pair + 2, :]
            score = score + w_lo * jnp.maximum(_nt_dot(ki_lo, x), 0.0)
            score = score + w_hi * jnp.maximum(_nt_dot(ki_hi, x), 0.0)
        key_scr[kc] = jnp.where(kc * wc + k_off <= q_pos, _sort_key(score * IDX_SCALE), INT_MIN)
        return carry

    lax.fori_loop(0, n_ch, score_chunk, 0)

    def search(it, res):
        trial = res + jnp.left_shift(jnp.int32(1), 31 - it)

        def count_chunk(kc, acc):
            hit = jnp.where(key_scr[kc] >= trial, 1.0, 0.0)
            for j in range(wc // SUBLANES):
                acc = acc + hit[j * SUBLANES:(j + 1) * SUBLANES, :]
            return acc

        acc = lax.fori_loop(0, n_ch, count_chunk, jnp.zeros((SUBLANES, tq), F32))
        return jnp.where(jnp.sum(acc, axis=0, keepdims=True) >= float(topk), trial, res)

    thr = lax.fori_loop(0, 32, search, jnp.full((1, tq), INT_MIN, jnp.int32))

    def bias_chunk(kc, n_keep):
        key = key_scr[kc]
        keep = (key >= thr) & (key > KEY_NEG_INF)
        bias_scr[kc] = jnp.transpose(jnp.where(keep, 0.0, MASKED))
        return n_keep + key_count(keep)

    n_keep = lax.fori_loop(0, n_ch, bias_chunk, jnp.zeros((1, tq), F32))

    @pl.when(jnp.max(n_keep) > float(topk))
    def _():
        n_gt = lax.fori_loop(0, n_ch, lambda kc, n: n + key_count(key_scr[kc] > thr), jnp.zeros((1, tq), F32))
        need = float(topk) - n_gt
        lower = _strict_lower(wc)

        def tie_chunk(kc, before):
            key = key_scr[kc]
            eq = key == thr
            eqf = jnp.where(eq, 1.0, 0.0)
            rank = before + _dot(lower, eqf.astype(BF16))
            keep = ((key > thr) | (eq & (rank < need))) & (key > KEY_NEG_INF)
            bias_scr[kc] = jnp.transpose(jnp.where(keep, 0.0, MASKED))
            return before + jnp.sum(eqf, axis=0, keepdims=True)

        lax.fori_loop(0, n_ch, tie_chunk, jnp.zeros((1, tq), F32))

    m_scr[...] = jnp.full(m_scr.shape, MASKED, F32)
    l_scr[...] = jnp.zeros(l_scr.shape, F32)
    acc_scr[...] = jnp.zeros(acc_scr.shape, F32)
    lane_blocks = [slice(j * LANES, (j + 1) * LANES) for j in range(wc // LANES)]

    def logit_chunk(kc, carry):
        r0 = pl.multiple_of(kc * wc, wc)
        bias = bias_scr[kc]
        for h in range(N_HEADS):
            g = h // GQA_GROUP
            kg = k_ref[pl.ds(r0, wc), g * HEAD_DIM:(g + 1) * HEAD_DIM]
            lg = _nt_dot(q_ref[:, h * HEAD_DIM:(h + 1) * HEAD_DIM], kg) * ATT_SCALE + bias
            lg_scr[h, kc] = lg
            mx = m_scr[h]
            for lb in lane_blocks:
                mx = jnp.maximum(mx, lg[:, lb])
            m_scr[h] = mx
        return carry

    lax.fori_loop(0, n_ch, logit_chunk, 0)
    for h in range(N_HEADS):
        m_scr[h] = jnp.broadcast_to(jnp.max(m_scr[h], axis=1, keepdims=True), (tq, LANES))

    def prob_chunk(kc, carry):
        r0 = pl.multiple_of(kc * wc, wc)
        for h in range(N_HEADS):
            g = h // GQA_GROUP
            sl = slice(h * HEAD_DIM, (h + 1) * HEAD_DIM)
            vg = v_ref[pl.ds(r0, wc), g * HEAD_DIM:(g + 1) * HEAD_DIM]
            lg = lg_scr[h, kc]
            mx = m_scr[h]
            ps = [jnp.exp(lg[:, lb] - mx) for lb in lane_blocks]
            den = l_scr[h]
            for p in ps:
                den = den + p
            l_scr[h] = den
            acc_scr[:, sl] = acc_scr[:, sl] + _dot(jnp.concatenate(ps, axis=1).astype(BF16), vg)
        return carry

    lax.fori_loop(0, n_ch, prob_chunk, 0)
    for h in range(N_HEADS):
        sl = slice(h * HEAD_DIM, (h + 1) * HEAD_DIM)
        den = jnp.sum(l_scr[h], axis=1, keepdims=True)
        o_ref[:, sl] = (acc_scr[:, sl] / den).astype(o_ref.dtype)


def _attn_prompt(q, qi, proj, ki_col, k, v, ki_lo, ki_hi, nb, s):
    tq = _pick(s, (256, 128))
    nq = s // tq
    topk = min(TOPK_MAX, s // 4)
    ki_blk = ki_col // LANES
    qrow = lambda w: pl.BlockSpec((tq, w), lambda b, i: (b * nq + i, 0))
    full = lambda w: pl.BlockSpec((s, w), lambda b, i: (b, 0))
    return pl.pallas_call(
        functools.partial(_attn_prompt_body, topk=topk),
        grid=(nb, nq),
        in_specs=[qrow(ATT_W), qrow(IDX_W), pl.BlockSpec((tq, LANES), lambda b, i: (b * nq + i, ki_blk)),
                  full(KV_W), full(KV_W), full(LANES), full(LANES)],
        out_specs=qrow(ATT_W),
        out_shape=jax.ShapeDtypeStruct((nb * s, ATT_W), BF16),
        scratch_shapes=[pltpu.VMEM((nq, tq, tq), jnp.int32), pltpu.VMEM((nq, tq, tq), F32),
                        pltpu.VMEM((N_HEADS, nq, tq, tq), F32),
                        pltpu.VMEM((N_HEADS, tq, LANES), F32), pltpu.VMEM((N_HEADS, tq, LANES), F32),
                        pltpu.VMEM((tq, ATT_W), F32)],
        compiler_params=_params(),
        name="attn_prompt",
    )(q, qi, proj, k, v, ki_lo, ki_hi)


def _s5_discretize_body(are_ref, aim_ref, ldt_ref, bre_ref, bim_ref, abre_ref, abim_ref, bbre_ref, bbim_ref):
    dt = jnp.exp(ldt_ref[...])
    lr, li = are_ref[...], aim_ref[...]
    mag = jnp.exp(lr * dt)
    ab_re, ab_im = mag * jnp.cos(li * dt), mag * jnp.sin(li * dt)
    zr, zi = ab_re - 1.0, ab_im
    den = lr * lr + li * li
    fr = (zr * lr + zi * li) / den
    fi = (zi * lr - zr * li) / den
    abre_ref[...] = ab_re
    abim_ref[...] = ab_im
    br, bi = bre_ref[...], bim_ref[...]
    bbre_ref[...] = fr * br - fi * bi
    bbim_ref[...] = fr * bi + fi * br


def _s5_discretize(a_re, a_im, log_dt, b_re, b_im):
    depth = a_re.shape[0]
    gp = pl.BlockSpec((None, SSM_GROUPS, 1, SSM_P), lambda l: (l, 0, 0, 0))
    gwp = pl.BlockSpec((None, SSM_GROUPS, SSM_GROUP, SSM_P), lambda l: (l, 0, 0, 0))
    bt = lambda b: jnp.swapaxes(b, 2, 3)
    a4 = lambda a: a.reshape(depth, SSM_GROUPS, 1, SSM_P)
    return pl.pallas_call(
        _s5_discretize_body,
        grid=(depth,),
        in_specs=[gp, gp, pl.BlockSpec((None, SSM_GROUPS, 1, 1), lambda l: (l, 0, 0, 0)), gwp, gwp],
        out_specs=[gp, gp, gwp, gwp],
        out_shape=[jax.ShapeDtypeStruct((depth, SSM_GROUPS, 1, SSM_P), F32)] * 2
        + [jax.ShapeDtypeStruct((depth, SSM_GROUPS, SSM_GROUP, SSM_P), F32)] * 2,
        compiler_params=_params(),
        name="s5_discretize",
    )(a4(a_re), a4(a_im), log_dt.reshape(depth, SSM_GROUPS, 1, 1), bt(b_re), bt(b_im))


def _block_diag_in(bb_t):
    depth = bb_t.shape[0]
    gpb = SSM_GROUPS // SSM_BLOCKS
    x = bb_t.reshape(depth, SSM_BLOCKS, gpb, SSM_GROUP, SSM_P)
    eye = jnp.eye(gpb, dtype=bb_t.dtype)
    out = jnp.einsum('lkgwp,gh->lkgwhp', x, eye)
    return out.reshape(depth, SSM_BLOCKS, gpb * SSM_GROUP, gpb * SSM_P).astype(BF16)


def _block_diag_out(c):
    depth = c.shape[0]
    gpb = SSM_GROUPS // SSM_BLOCKS
    x = c.reshape(depth, SSM_BLOCKS, gpb, SSM_GROUP, SSM_P)
    eye = jnp.eye(gpb, dtype=c.dtype)
    out = jnp.einsum('lkgwp,gh->lkgphw', x, eye)
    return out.reshape(depth, SSM_BLOCKS, gpb * SSM_P, gpb * SSM_GROUP).astype(BF16)


def _s5_body(u_ref, bbre_ref, bbim_ref, cre_ref, cim_ref, abre_ref, abim_ref, d_ref, h0re_ref, h0im_ref,
             g_ref, htre_ref, htim_ref, bure, buim, hre, him, ust, gst, *, nb):
    c = pl.program_id(0)
    rows = bure.shape[0]
    n_t = rows // nb
    per_tile = SUBLANES // nb

    @pl.when(c == 0)
    def _():
        hre[...] = h0re_ref[...]
        him[...] = h0im_ref[...]

    for k in range(SSM_BLOCKS):
        ln = slice(k * LANES, (k + 1) * LANES)
        for b in range(nb):
            if n_t == 1:
                ust[k, b:b + 1, :] = u_ref[b, :, ln]
            else:
                ust[k, pl.ds(b, n_t, stride=nb), :] = u_ref[b, :, ln]
        uk = ust[k].astype(BF16)
        st = slice(k * SSM_BLOCK_STATE, (k + 1) * SSM_BLOCK_STATE)
        bure[:, st] = _dot(uk, bbre_ref[k])
        buim[:, st] = _dot(uk, bbim_ref[k])

    for cc in range(SSM_STATE // SCAN_LANES):
        sl = slice(cc * SCAN_LANES, (cc + 1) * SCAN_LANES)
        ar = jnp.broadcast_to(abre_ref[:, sl], (nb, SCAN_LANES))
        ai = jnp.broadcast_to(abim_ref[:, sl], (nb, SCAN_LANES))

        def step(j, carry, sl=sl, ar=ar, ai=ai):
            hr, hi = carry
            r0 = pl.multiple_of(j * SUBLANES, SUBLANES)
            xr = bure[pl.ds(r0, SUBLANES), sl]
            xi = buim[pl.ds(r0, SUBLANES), sl]
            out_r, out_i = [], []
            for t in range(per_tile):
                nr = ar * hr - ai * hi + xr[t * nb:(t + 1) * nb]
                ni = ar * hi + ai * hr + xi[t * nb:(t + 1) * nb]
                hr, hi = nr, ni
                out_r.append(hr)
                out_i.append(hi)
            bure[pl.ds(r0, SUBLANES), sl] = out_r[0] if per_tile == 1 else jnp.concatenate(out_r, axis=0)
            buim[pl.ds(r0, SUBLANES), sl] = out_i[0] if per_tile == 1 else jnp.concatenate(out_i, axis=0)
            return hr, hi

        hr, hi = lax.fori_loop(0, rows // SUBLANES, step, (hre[:, sl], him[:, sl]))
        hre[:, sl] = hr
        him[:, sl] = hi

    for k in range(SSM_BLOCKS):
        st = slice(k * SSM_BLOCK_STATE, (k + 1) * SSM_BLOCK_STATE)
        ln = slice(k * LANES, (k + 1) * LANES)
        y = _dot(bure[:, st].astype(BF16), cre_ref[k]) - _dot(buim[:, st].astype(BF16), cim_ref[k])
        y = y + d_ref[:, ln] * ust[k]
        gst[k] = _gelu_tanh(y)
        for b in range(nb):
            if n_t == 1:
                g_ref[b, :, ln] = gst[k, b:b + 1, :]
            else:
                g_ref[b, :, ln] = gst[k, pl.ds(b, n_t, stride=nb), :]

    @pl.when(c == pl.num_programs(0) - 1)
    def _():
        htre_ref[...] = hre[...]
        htim_ref[...] = him[...]


def _s5(proj, nb, s, l, bb_re, bb_im, c_re, c_im, ab_re, ab_im, d, h0_re, h0_im):
    n_t = min(s, 512 // nb)
    rows = n_t * nb
    blk4 = lambda a: pl.BlockSpec((None,) + a.shape[1:], lambda c: (l, 0, 0, 0))
    full2 = lambda a: pl.BlockSpec(a.shape, lambda c: (0, 0))
    vec = lambda a: a.reshape(1, -1)
    ab_re, ab_im, d = vec(ab_re), vec(ab_im), vec(d)
    u_spec = pl.BlockSpec((nb, n_t, SSM_W), lambda c: (0, c, 0))
    g, h_re, h_im = pl.pallas_call(
        functools.partial(_s5_body, nb=nb),
        grid=(s // n_t,),
        in_specs=[u_spec, blk4(bb_re), blk4(bb_im), blk4(c_re), blk4(c_im),
                  full2(ab_re), full2(ab_im), full2(d), full2(h0_re), full2(h0_im)],
        out_specs=[u_spec, full2(h0_re), full2(h0_im)],
        out_shape=[jax.ShapeDtypeStruct((nb, s, SSM_W), F32),
                   jax.ShapeDtypeStruct(h0_re.shape, F32), jax.ShapeDtypeStruct(h0_im.shape, F32)],
        scratch_shapes=[pltpu.VMEM((rows, SSM_STATE), F32), pltpu.VMEM((rows, SSM_STATE), F32),
                        pltpu.VMEM((nb, SSM_STATE), F32), pltpu.VMEM((nb, SSM_STATE), F32),
                        pltpu.VMEM((SSM_BLOCKS, rows, LANES), F32), pltpu.VMEM((SSM_BLOCKS, rows, LANES), F32)],
        compiler_params=_params(),
        name="s5_scan",
    )(proj.reshape(nb, s, proj.shape[1]), bb_re, bb_im, c_re, c_im, ab_re, ab_im, d, h0_re, h0_im)
    return g.reshape(nb * s, SSM_W), h_re, h_im


IDX_PAGES_PER_STEP = 16


def _sample_index_body(pt_ref, qi_ref, wi_ref, kin_ref, *rest, n_pg, topk):
    del pt_ref
    pages = rest[:n_pg]
    idx_ref, nsel_ref, bnew_ref, sc_scr = rest[n_pg:]
    c = pl.program_id(1)
    n_pages = sc_scr.shape[0]
    qi = qi_ref[...].astype(BF16)
    wi = wi_ref[...]

    def weighted(d):
        return jnp.sum(wi * jnp.maximum(d, 0.0), axis=0, keepdims=True) * IDX_SCALE

    for p in range(n_pg):
        sc_scr[pl.ds(c * n_pg + p, 1), :] = weighted(_dot(qi, pages[p][...].astype(BF16)))

    @pl.when(c == pl.num_programs(1) - 1)
    def _():
        kin = kin_ref[...].astype(BF16).astype(F32)
        d_new = jnp.sum(qi.astype(F32) * kin, axis=1, keepdims=True)
        key_new = _sort_key(jnp.broadcast_to(weighted(d_new), (1, LANES)))
        key = _sort_key(sc_scr[...])

        def count(mask, mask_new):
            per_lane = jnp.sum(jnp.where(mask, 1.0, 0.0), axis=0, keepdims=True)
            return jnp.sum(per_lane, axis=1, keepdims=True) + jnp.where(mask_new, 1.0, 0.0)

        def body(it, res):
            trial = res + jnp.left_shift(jnp.int32(1), 31 - it)
            return jnp.where(count(key >= trial, key_new >= trial) >= float(topk), trial, res)

        thr = lax.fori_loop(0, 32, body, jnp.full((1, LANES), INT_MIN, jnp.int32))
        need = float(topk) - count(key > thr, key_new > thr)
        eq = key == thr
        eqf = jnp.where(eq, 1.0, 0.0).astype(BF16)
        in_page = _dot(eqf, _strict_upper(LANES))
        lower = _strict_lower(n_pages)
        per_page = jnp.sum(jnp.where(eq, 1.0, 0.0), axis=1, keepdims=True)
        earlier = _dot(lower, jnp.broadcast_to(per_page, (n_pages, LANES)).astype(BF16))
        keep = ((key > thr) | (eq & (in_page + earlier < need))) & (key > KEY_NEG_INF)
        n_eq_past = jnp.sum(jnp.sum(jnp.where(eq, 1.0, 0.0), axis=0, keepdims=True), axis=1, keepdims=True)
        keep_new = ((key_new > thr) | ((key_new == thr) & (n_eq_past < need))) & (key_new > KEY_NEG_INF)
        bnew_ref[...] = jnp.where(keep_new, 0.0, MASKED)

        keepf = jnp.where(keep, 1.0, 0.0).astype(BF16)
        in_page_rank = jnp.where(keep, _dot(keepf, _strict_upper(LANES)), -1.0).astype(BF16)
        per_page = _nt_dot(jnp.ones((SUBLANES, LANES), BF16), keepf)[:1]
        pr = lax.broadcasted_iota(jnp.int32, (n_pages, n_pages), 0)
        pc = lax.broadcasted_iota(jnp.int32, (n_pages, n_pages), 1)
        upto = jnp.where(pr <= pc, 1.0, 0.0).astype(BF16)
        cum = _dot(jnp.broadcast_to(per_page, (SUBLANES, n_pages)).astype(BF16), upto)[:1]
        slot = lax.broadcasted_iota(jnp.int32, (topk, 1), 0).astype(F32)
        passed = cum <= slot
        page_of = jnp.sum(jnp.where(passed, 1.0, 0.0), axis=1, keepdims=True)
        start_of = jnp.sum(jnp.where(passed, per_page, 0.0), axis=1, keepdims=True)
        page_lane = lax.broadcasted_iota(jnp.int32, (topk, n_pages), 1).astype(F32)
        ranks = _dot(jnp.where(page_lane == page_of, 1.0, 0.0).astype(BF16), in_page_rank)
        lane = lax.broadcasted_iota(jnp.int32, (topk, LANES), 1).astype(F32)
        lane_of = jnp.sum(jnp.where(ranks == slot - start_of, lane, 0.0), axis=1, keepdims=True)
        n_kept = jnp.sum(per_page, axis=1, keepdims=True)
        idx_ref[...] = jnp.where(slot < n_kept, page_of * PAGE_SIZE + lane_of, 0.0).astype(jnp.int32)
        nsel_ref[...] = jnp.broadcast_to(n_kept, (1, LANES)).astype(jnp.int32)


def _sample_index(page_table, cache_kidx_t, layer, qi, wi, ki_new):
    nb, n_pages = page_table.shape
    n_pg = _pick(n_pages, (IDX_PAGES_PER_STEP, 8, 4, 2, 1))
    topk = min(TOPK_MAX, (n_pages * PAGE_SIZE + 1) // 4)
    page_spec = lambda p: pl.BlockSpec((None, None, IDX_DIM, PAGE_SIZE),
                                       lambda b, c, pt: (layer, pt[b, c * n_pg + p], 0, 0))
    per_b = lambda shape: pl.BlockSpec((None,) + shape, lambda b, c, pt: (b, 0, 0))
    grid_spec = pltpu.PrefetchScalarGridSpec(
        num_scalar_prefetch=1,
        grid=(nb, n_pages // n_pg),
        in_specs=[per_b((N_IDX_HEADS, IDX_DIM)), per_b((N_IDX_HEADS, 1)), per_b((1, IDX_DIM))]
        + [page_spec(p) for p in range(n_pg)],
        out_specs=[per_b((topk, 1)), per_b((1, LANES)), per_b((1, LANES))],
        scratch_shapes=[pltpu.VMEM((n_pages, LANES), F32)],
    )
    idx, n_sel, bias_new = pl.pallas_call(
        functools.partial(_sample_index_body, n_pg=n_pg, topk=topk),
        grid_spec=grid_spec,
        out_shape=[jax.ShapeDtypeStruct((nb, topk, 1), jnp.int32), jax.ShapeDtypeStruct((nb, 1, LANES), jnp.int32),
                   jax.ShapeDtypeStruct((nb, 1, LANES), F32)],
        compiler_params=_params(),
        name="sample_index",
    )(page_table, qi.reshape(nb, N_IDX_HEADS, IDX_DIM), wi.reshape(nb, N_IDX_HEADS, 1),
      ki_new.reshape(nb, 1, IDX_DIM), *([cache_kidx_t] * n_pg))
    return idx.reshape(nb, topk), n_sel[:, 0, 0], bias_new


SLOTS_PER_TILE = SUBLANES // N_KV_HEADS


def _group_rows(n_cols):
    head = lax.broadcasted_iota(jnp.int32, (N_HEADS, n_cols), 0)
    return [(head >= g * GQA_GROUP) & (head < (g + 1) * GQA_GROUP) for g in range(N_KV_HEADS)]


def _sample_attn_body(pt_ref, idx_ref, nsel_ref, q_ref, kn_ref, vn_ref, bnew_ref, odd_ref, ck_hbm, cv_hbm,
                      o_ref, kbuf, vbuf, sem, *, layer, topk):
    b = pl.program_id(0)

    def tile_copy(src_hbm, buf, which, phys, src_row, dst_row):
        return pltpu.make_async_copy(src_hbm.at[layer, phys, pl.ds(src_row, SUBLANES), :],
                                     buf.at[pl.ds(dst_row, SUBLANES), :], sem.at[which])

    def issue(r, carry):
        idx = idx_ref[b, r]
        phys = pt_ref[b, idx // PAGE_SIZE]
        src_row = pl.multiple_of(((idx % PAGE_SIZE) // SLOTS_PER_TILE) * SUBLANES, SUBLANES)
        dst_row = pl.multiple_of(r * SUBLANES, SUBLANES)
        tile_copy(ck_hbm, kbuf, 0, phys, src_row, dst_row).start()
        tile_copy(cv_hbm, vbuf, 1, phys, src_row, dst_row).start()
        return carry

    lax.fori_loop(0, topk, issue, 0)

    def drain(r, carry):
        tile_copy(ck_hbm, kbuf, 0, 0, 0, 0).wait()
        tile_copy(cv_hbm, vbuf, 1, 0, 0, 0).wait()
        return carry

    lax.fori_loop(0, topk, drain, 0)

    q = q_ref[...].astype(BF16)
    odd = odd_ref[...] == 1
    live = lax.broadcasted_iota(jnp.int32, (1, topk), 1) < nsel_ref[b]

    def gathered(buf, g):
        first = buf[pl.ds(g, topk, stride=SUBLANES), :]
        second = buf[pl.ds(N_KV_HEADS + g, topk, stride=SUBLANES), :]
        return jnp.where(odd, second, first).astype(BF16)

    in_group = _group_rows(topk)
    lg = jnp.zeros((N_HEADS, topk), F32)
    for g in range(N_KV_HEADS):
        lg = jnp.where(in_group[g], _nt_dot(q, gathered(kbuf, g)), lg)
    lg = jnp.where(live, lg * ATT_SCALE, MASKED)
    qf = q.astype(F32)
    kn = kn_ref[...].astype(BF16).astype(F32)
    vn = vn_ref[...].astype(BF16).astype(F32)
    lg_new = jnp.sum(qf * kn, axis=1, keepdims=True) * ATT_SCALE + bnew_ref[:, :1]
    mx = jnp.maximum(jnp.max(lg, axis=1, keepdims=True), lg_new)
    pr = jnp.where(live, jnp.exp(lg - mx), 0.0)
    pr_new = jnp.where(bnew_ref[:, :1] == 0.0, jnp.exp(lg_new - mx), 0.0)
    den = jnp.sum(pr, axis=1, keepdims=True) + pr_new
    prb = pr.astype(BF16)
    in_group_d = _group_rows(HEAD_DIM)
    pv = jnp.zeros((N_HEADS, HEAD_DIM), F32)
    for g in range(N_KV_HEADS):
        pv = jnp.where(in_group_d[g], _dot(prb, gathered(vbuf, g)), pv)
    o_ref[...] = ((pv + pr_new.astype(BF16).astype(F32) * vn) / den).astype(o_ref.dtype)


def _sample_attn(page_table, cache_k2, cache_v2, layer, idx, n_sel, bias_new, q, k_new, v_new):
    nb = page_table.shape[0]
    topk = idx.shape[1]
    per_b = lambda shape: pl.BlockSpec((None,) + shape, lambda b, pt, ix, ns: (b, 0, 0))
    per_head = lambda a: jnp.repeat(a.reshape(nb, N_KV_HEADS, HEAD_DIM), GQA_GROUP, axis=1)
    odd = (idx % SLOTS_PER_TILE).reshape(nb, topk, 1)
    grid_spec = pltpu.PrefetchScalarGridSpec(
        num_scalar_prefetch=3,
        grid=(nb,),
        in_specs=[per_b((N_HEADS, HEAD_DIM)), per_b((N_HEADS, HEAD_DIM)), per_b((N_HEADS, HEAD_DIM)),
                  per_b((1, LANES)), per_b((topk, 1)),
                  pl.BlockSpec(memory_space=pl.ANY), pl.BlockSpec(memory_space=pl.ANY)],
        out_specs=per_b((N_HEADS, HEAD_DIM)),
        scratch_shapes=[pltpu.VMEM((topk * SUBLANES, HEAD_DIM), F32), pltpu.VMEM((topk * SUBLANES, HEAD_DIM), F32),
                        pltpu.SemaphoreType.DMA((2,))],
    )
    out = pl.pallas_call(
        functools.partial(_sample_attn_body, layer=layer, topk=topk),
        grid_spec=grid_spec,
        out_shape=jax.ShapeDtypeStruct((nb, N_HEADS, HEAD_DIM), F32),
        compiler_params=_params(),
        name="sample_attn",
    )(page_table, idx, n_sel, q.reshape(nb, N_HEADS, HEAD_DIM), per_head(k_new), per_head(v_new), bias_new, odd,
      cache_k2, cache_v2)
    return out.reshape(nb, ATT_W)


def _layer_tail(proj, g, y_att, x, w, l, act_dtype):
    z = _glu(g, w["w_glu"], l, act_dtype)
    merged = _merge(z, y_att, w["w_branch"], l, proj, act_dtype)
    x1, hn = _out_norm(merged, w["w_out"], l, x, w["norm_ffn"][l], act_dtype)
    act = _ffn_up(hn, w["w_ffn_up"], l, act_dtype)
    return _ffn_down(act, w["w_ffn_down"], l, x1)


def _s5_layer(proj, nb, s, w, l, h0_re, h0_im):
    return _s5(proj, nb, s, l, w["bb_re"], w["bb_im"], w["c_re"], w["c_im"], w["ab_re"][l], w["ab_im"][l],
               w["ssm_d"][l], h0_re, h0_im)


def _prompt_layer(x, nb, s, tabs, w, l):
    xn = _rmsnorm(x, w["norm_mix"][l], BF16)
    proj = _in_proj(xn, w["w_in"], l, PROJ_TN, F32)
    q, k_out, k_bf, v_out, v_bf, qi, ki_out, ki_lo, ki_hi = _qk_post(
        proj, w["ki_col"], tabs, s, w["q_norm"][l], w["k_norm"][l], BF16)
    y_att = _attn_prompt(q, qi, proj, w["ki_col"], k_bf, v_bf, ki_lo, ki_hi, nb, s)
    zeros = jnp.zeros((nb, SSM_STATE), F32)
    g, h_re, h_im = _s5_layer(proj, nb, s, w, l, zeros, zeros)
    x2 = _layer_tail(proj, g, y_att, x, w, l, BF16)
    return x2, k_out, v_out, ki_out, h_re, h_im


def _sample_layer(x, nb, tabs, w, l, page_table, cache_k2, cache_v2, cache_kidx_t, h0_re, h0_im):
    xn = _rmsnorm(x, w["norm_mix"][l], F32)
    proj = _in_proj(xn, w["w_in"], l, PROJ_TN, F32)
    ki_col = w["ki_col"]
    q, k_out, _, v_out, _, qi, ki_out, _, _ = _qk_post(proj, ki_col, tabs, nb, w["q_norm"][l], w["k_norm"][l], F32)
    wi = proj[:, ki_col + IDX_DIM:ki_col + IDX_DIM + N_IDX_HEADS]
    idx, n_sel, bias_new = _sample_index(page_table, cache_kidx_t, l, qi, wi, ki_out)
    y_att = _sample_attn(page_table, cache_k2, cache_v2, l, idx, n_sel, bias_new, q, k_out, v_out)
    g, h_re, h_im = _s5_layer(proj, nb, 1, w, l, h0_re, h0_im)
    x2 = _layer_tail(proj, g, y_att, x, w, l, F32)
    return x2, k_out, v_out, ki_out, h_re, h_im


def kernel(x_prompt, x_sample, cache_k, cache_v, cache_kidx, state_ssm_re, state_ssm_im, page_table, norm_mix, w_in, q_norm, k_norm, ssm_a_re, ssm_a_im, ssm_log_dt, ssm_b_re, ssm_b_im, ssm_c_re, ssm_c_im, ssm_d, w_glu, w_branch, w_out, norm_ffn, w_ffn_up, w_ffn_down):
    nb_p, s, d_model = x_prompt.shape
    nb_s, dec_seq, _ = x_sample.shape
    assert dec_seq == 1, "the sample group decodes one token per sequence"
    depth = w_in.shape[0]
    past = page_table.shape[1] * PAGE_SIZE

    ki_orig = COL_GATE
    gate_orig = ki_orig + IDX_DIM + N_IDX_HEADS
    n_gate = w_in.shape[2] - gate_orig
    ki_col = COL_GATE + n_gate
    assert ki_col % LANES == 0
    cols = ki_col + LANES
    cols_pad = -(-cols // PROJ_TN) * PROJ_TN
    w_in_t = jnp.swapaxes(w_in, 1, 2)
    w_in_r = jnp.concatenate(
        [w_in_t[:, :COL_GATE], w_in_t[:, gate_orig:], w_in_t[:, ki_orig:gate_orig],
         jnp.zeros((depth, cols_pad - ki_col - IDX_DIM - N_IDX_HEADS, d_model), w_in.dtype)], axis=1).astype(BF16)

    ab_re, ab_im, bbt_re, bbt_im = _s5_discretize(ssm_a_re, ssm_a_im, ssm_log_dt, ssm_b_re, ssm_b_im)
    bb_re, bb_im = _block_diag_in(bbt_re), _block_diag_in(bbt_im)
    c_re, c_im = _block_diag_out(ssm_c_re), _block_diag_out(ssm_c_im)

    tabs_p = _rope_tables(jnp.arange(s, dtype=jnp.int32))
    tabs_s = _rope_tables(jnp.full((nb_s,), past, dtype=jnp.int32))

    xp = x_prompt.reshape(nb_p * s, d_model)
    xs = x_sample.reshape(nb_s, d_model)
    cache_k2 = cache_k.reshape(cache_k.shape[:2] + (PAGE_SIZE * N_KV_HEADS, HEAD_DIM))
    cache_v2 = cache_v.reshape(cache_v.shape[:2] + (PAGE_SIZE * N_KV_HEADS, HEAD_DIM))
    cache_kidx_t = jnp.swapaxes(cache_kidx, 2, 3)
    w = dict(ki_col=ki_col, norm_mix=norm_mix, w_in=w_in_r, q_norm=q_norm, k_norm=k_norm,
             bb_re=bb_re, bb_im=bb_im, c_re=c_re, c_im=c_im, ab_re=ab_re, ab_im=ab_im, ssm_d=ssm_d,
             w_glu=w_glu, w_branch=w_branch, w_out=w_out, norm_ffn=norm_ffn,
             w_ffn_up=w_ffn_up, w_ffn_down=w_ffn_down)
    outs_p, outs_s = [], []
    for l in range(depth):
        xp, *rest_p = _prompt_layer(xp, nb_p, s, tabs_p, w, l)
        xs, *rest_s = _sample_layer(xs, nb_s, tabs_s, w, l, page_table, cache_k2, cache_v2, cache_kidx_t,
                                    state_ssm_re[l].reshape(nb_s, SSM_STATE), state_ssm_im[l].reshape(nb_s, SSM_STATE))
        outs_p.append(rest_p)
        outs_s.append(rest_s)

    def stack(outs, idx, shape):
        return jnp.stack([o[idx].reshape(shape) for o in outs])

    kv_p, kv_s = (nb_p, s, N_KV_HEADS, HEAD_DIM), (nb_s, dec_seq, N_KV_HEADS, HEAD_DIM)
    st_p, st_s = (nb_p, SSM_GROUPS, SSM_P), (nb_s, SSM_GROUPS, SSM_P)
    return (xp.reshape(nb_p, s, d_model), xs.reshape(nb_s, dec_seq, d_model),
            stack(outs_p, 0, kv_p), stack(outs_p, 1, kv_p), stack(outs_p, 2, (nb_p, s, IDX_DIM)),
            stack(outs_p, 3, st_p), stack(outs_p, 4, st_p),
            stack(outs_s, 0, kv_s), stack(outs_s, 1, kv_s), stack(outs_s, 2, (nb_s, dec_seq, IDX_DIM)),
            stack(outs_s, 3, st_s), stack(outs_s, 4, st_s))
```

```python
import functools
import math

import jax
import jax.numpy as jnp
from jax import lax
from jax.experimental import pallas as pl
from jax.experimental.pallas import tpu as pltpu

F32 = jnp.float32
BF16 = jnp.bfloat16

SSM_W = 1024
SSM_GROUP = 16
SSM_GROUPS = SSM_W // SSM_GROUP
SSM_P = 64
SSM_STATE = SSM_GROUPS * SSM_P
N_HEADS = 8
N_KV_HEADS = 4
HEAD_DIM = 128
GQA_GROUP = N_HEADS // N_KV_HEADS
ATT_W = N_HEADS * HEAD_DIM
KV_W = N_KV_HEADS * HEAD_DIM
N_IDX_HEADS = 16
IDX_DIM = 64
IDX_W = N_IDX_HEADS * IDX_DIM
IDX_SCALE = (IDX_DIM * N_IDX_HEADS) ** -0.5
ATT_SCALE = HEAD_DIM ** -0.5
TOPK_MAX = 256
PAGE_SIZE = 128
ROPE_THETA = 10000.0
EPS = 1e-6

LANES = 128
SUBLANES = 8
VMEM_LIMIT = 56 * 1024 * 1024

COL_U = 0
COL_Q = SSM_W
COL_K = COL_Q + ATT_W
COL_V = COL_K + KV_W
COL_QI = COL_V + KV_W
COL_GATE = COL_QI + IDX_W
PROJ_TN = 768

MASKED = -1e30
INT_MIN = -(2 ** 31)
KEY_NEG_INF = -2139095041

SCAN_LANES = 512
SSM_BLOCKS = SSM_W // LANES
SSM_BLOCK_STATE = SSM_STATE // SSM_BLOCKS


def _params(**kw):
    return pltpu.CompilerParams(vmem_limit_bytes=VMEM_LIMIT, **kw)


def _pick(n, cands):
    for c in cands:
        if n % c == 0:
            return c
    return n


def _nt_dot(a, b):
    return lax.dot_general(a, b, (((1,), (1,)), ((), ())), preferred_element_type=F32)


def _dot(a, b):
    return jnp.dot(a, b, preferred_element_type=F32)


def _sigmoid(x):
    return 1.0 / (1.0 + jnp.exp(-x))


def _gelu_tanh(x):
    c = math.sqrt(2.0 / math.pi)
    return 0.5 * x * (1.0 + jnp.tanh(c * (x + 0.044715 * (x * x * x))))


def _sort_key(s):
    bits = pltpu.bitcast(s, jnp.int32)
    return bits ^ ((bits >> 31) & jnp.int32(0x7FFFFFFF))


def _in_proj_body(x_ref, g_ref, w_ref, o_ref, xn_scr):
    @pl.when(pl.program_id(1) == 0)
    def _():
        x = x_ref[...]
        ms = jnp.mean(x * x, axis=-1, keepdims=True)
        xn_scr[...] = ((x * lax.rsqrt(ms + EPS)) * g_ref[...]).astype(xn_scr.dtype)

    o_ref[...] = _nt_dot(xn_scr[...].astype(BF16), w_ref[...]).astype(o_ref.dtype)


def _in_proj(x, g, w, l, tn, out_dtype):
    m, k = x.shape
    n = w.shape[1]
    tm = _pick(m, (1024, 512, 256, 128, 8))
    bf16_rows = 2 * SUBLANES
    return pl.pallas_call(
        _in_proj_body,
        grid=(m // tm, n // tn),
        in_specs=[pl.BlockSpec((tm, k), lambda i, j: (i, 0)), pl.BlockSpec((1, k), lambda i, j: (0, 0)),
                  pl.BlockSpec((None, tn, k), lambda i, j: (l, j, 0))],
        out_specs=pl.BlockSpec((tm, tn), lambda i, j: (i, j)),
        out_shape=jax.ShapeDtypeStruct((m, n), out_dtype),
        scratch_shapes=[pltpu.VMEM((tm, k), BF16 if tm % bf16_rows == 0 else F32)],
        compiler_params=_params(),
        name="in_proj",
    )(x, g.reshape(1, k), w)


def _cast_weight_once(step, w_ref, wb_scr):
    @pl.when(step == 0)
    def _():
        wb_scr[...] = w_ref[...].astype(BF16)


def _glu_body(g_ref, w_ref, o_ref, wb):
    _cast_weight_once(pl.program_id(0), w_ref, wb)
    g = g_ref[...]
    z = _dot(g.astype(BF16), wb[...])
    o_ref[...] = (g * _sigmoid(z)).astype(o_ref.dtype)


def _glu(g, w, l, out_dtype):
    m, n = g.shape
    tm = _pick(m, (512, 256, 128, 8))
    return pl.pallas_call(
        _glu_body,
        grid=(m // tm,),
        in_specs=[pl.BlockSpec((tm, n), lambda i: (i, 0)),
                  pl.BlockSpec((None, n, n), lambda i: (l, 0, 0), pipeline_mode=pl.Buffered(1))],
        out_specs=pl.BlockSpec((tm, n), lambda i: (i, 0)),
        out_shape=jax.ShapeDtypeStruct((m, n), out_dtype),
        scratch_shapes=[pltpu.VMEM((n, n), BF16)],
        compiler_params=_params(),
        name="glu",
    )(g, w)


def _merge_body(a0_ref, a1_ref, w0_ref, w1_ref, g0_ref, g1_ref, o_ref, wb0, wb1):
    _cast_weight_once(pl.program_id(1), w0_ref, wb0)
    _cast_weight_once(pl.program_id(1), w1_ref, wb1)
    y0 = _dot(a0_ref[...].astype(BF16), wb0[...])
    y1 = _dot(a1_ref[...].astype(BF16), wb1[...])
    o_ref[...] = (_sigmoid(g0_ref[...]) * y0 + _sigmoid(g1_ref[...]) * y1).astype(o_ref.dtype)


def _merge(a0, a1, w_branch, l, proj, out_dtype):
    m, k = a0.shape
    n = w_branch.shape[3]
    tm = _pick(m, (512, 256, 128, 8))
    tn = _pick(n, (1024, 512, 256, 128))
    g0_blk = COL_GATE // tn
    g1_blk = (COL_GATE + n) // tn
    wspec = lambda br: pl.BlockSpec((None, None, k, tn), lambda j, i: (l, br, 0, j))
    return pl.pallas_call(
        _merge_body,
        grid=(n // tn, m // tm),
        in_specs=[
            pl.BlockSpec((tm, k), lambda j, i: (i, 0)),
            pl.BlockSpec((tm, k), lambda j, i: (i, 0)),
            wspec(0), wspec(1),
            pl.BlockSpec((tm, tn), lambda j, i: (i, g0_blk + j)),
            pl.BlockSpec((tm, tn), lambda j, i: (i, g1_blk + j)),
        ],
        out_specs=pl.BlockSpec((tm, tn), lambda j, i: (i, j)),
        out_shape=jax.ShapeDtypeStruct((m, n), out_dtype),
        scratch_shapes=[pltpu.VMEM((k, tn), BF16), pltpu.VMEM((k, tn), BF16)],
        compiler_params=_params(),
        name="merge",
    )(a0, a1, w_branch, w_branch, proj, proj)


def _out_norm_body(a_ref, w_ref, x_ref, g_ref, x1_ref, hn_ref, wb):
    _cast_weight_once(pl.program_id(0), w_ref, wb)
    x1 = x_ref[...] + _dot(a_ref[...].astype(BF16), wb[...])
    x1_ref[...] = x1
    ms = jnp.mean(x1 * x1, axis=-1, keepdims=True)
    hn_ref[...] = ((x1 * lax.rsqrt(ms + EPS)) * g_ref[...]).astype(hn_ref.dtype)


def _out_norm(a, w, l, x, g, hn_dtype):
    m, k = a.shape
    n = w.shape[2]
    tm = _pick(m, (256, 128, 8))
    return pl.pallas_call(
        _out_norm_body,
        grid=(m // tm,),
        in_specs=[
            pl.BlockSpec((tm, k), lambda i: (i, 0)),
            pl.BlockSpec((None, k, n), lambda i: (l, 0, 0), pipeline_mode=pl.Buffered(1)),
            pl.BlockSpec((tm, n), lambda i: (i, 0)),
            pl.BlockSpec((1, n), lambda i: (0, 0)),
        ],
        out_specs=[pl.BlockSpec((tm, n), lambda i: (i, 0)), pl.BlockSpec((tm, n), lambda i: (i, 0))],
        out_shape=[jax.ShapeDtypeStruct((m, n), F32), jax.ShapeDtypeStruct((m, n), hn_dtype)],
        scratch_shapes=[pltpu.VMEM((k, n), BF16)],
        compiler_params=_params(),
        name="out_proj_norm",
    )(a, w, x, g.reshape(1, n))


def _ffn_up_body(a_ref, wg_ref, wu_ref, o_ref, wgb, wub):
    _cast_weight_once(pl.program_id(1), wg_ref, wgb)
    _cast_weight_once(pl.program_id(1), wu_ref, wub)
    a = a_ref[...].astype(BF16)
    gate = _dot(a, wgb[...])
    up = _dot(a, wub[...])
    o_ref[...] = ((gate * _sigmoid(gate)) * up).astype(o_ref.dtype)


def _ffn_up(a, w_up, l, out_dtype):
    m, k = a.shape
    d_ff = w_up.shape[2] // 2
    tm = _pick(m, (2048, 1024, 512, 256, 128, 8))
    tn = _pick(d_ff, (512, 256, 128))
    up_blk = d_ff // tn
    return pl.pallas_call(
        _ffn_up_body,
        grid=(d_ff // tn, m // tm),
        in_specs=[
            pl.BlockSpec((tm, k), lambda j, i: (i, 0)),
            pl.BlockSpec((None, k, tn), lambda j, i: (l, 0, j)),
            pl.BlockSpec((None, k, tn), lambda j, i: (l, 0, up_blk + j)),
        ],
        out_specs=pl.BlockSpec((tm, tn), lambda j, i: (i, j)),
        out_shape=jax.ShapeDtypeStruct((m, d_ff), out_dtype),
        scratch_shapes=[pltpu.VMEM((k, tn), BF16), pltpu.VMEM((k, tn), BF16)],
        compiler_params=_params(),
        name="ffn_up",
    )(a, w_up, w_up)


def _ffn_down_body(a_ref, w_ref, x_ref, o_ref, wb):
    _cast_weight_once(pl.program_id(1), w_ref, wb)
    o_ref[...] = x_ref[...] + _dot(a_ref[...].astype(BF16), wb[...])


def _ffn_down(a, w, l, x):
    m, k = a.shape
    n = w.shape[2]
    tm = _pick(m, (256, 128, 8))
    tn = _pick(n, (1024, 512, 256, 128))
    return pl.pallas_call(
        _ffn_down_body,
        grid=(n // tn, m // tm),
        in_specs=[
            pl.BlockSpec((tm, k), lambda j, i: (i, 0)),
            pl.BlockSpec((None, k, tn), lambda j, i: (l, 0, j), pipeline_mode=pl.Buffered(1)),
            pl.BlockSpec((tm, tn), lambda j, i: (i, j)),
        ],
        out_specs=pl.BlockSpec((tm, tn), lambda j, i: (i, j)),
        out_shape=jax.ShapeDtypeStruct((m, n), F32),
        scratch_shapes=[pltpu.VMEM((k, tn), BF16)],
        compiler_params=_params(),
        name="ffn_down",
    )(a, w, x)


def _rope_tables(pos):
    def tables(dim):
        half = dim // 2
        freqs = ROPE_THETA ** (-jnp.arange(half, dtype=F32) / half)
        ang = pos.astype(F32)[:, None] * freqs[None, :]
        cos, sin = jnp.cos(ang), jnp.sin(ang)
        reps = LANES // dim
        cos_t = jnp.tile(jnp.concatenate([cos, cos], axis=1), (1, reps))
        sin_t = jnp.tile(jnp.concatenate([-sin, sin], axis=1), (1, reps))
        return cos_t, sin_t
    return tables(HEAD_DIM) + tables(IDX_DIM)


def _rope_head(x, cos_t, sin_t):
    return x * cos_t + pltpu.roll(x, HEAD_DIM // 2, axis=1) * sin_t


def _rope_idx(x, cos_t, sin_t, first_half):
    half = IDX_DIM // 2
    partner = jnp.where(first_half, pltpu.roll(x, LANES - half, axis=1), pltpu.roll(x, half, axis=1))
    return x * cos_t + partner * sin_t


def _head_norm(x, g):
    ms = jnp.mean(x * x, axis=-1, keepdims=True)
    return (x * lax.rsqrt(ms + EPS)) * g


def _qk_post_body(q_ref, k_ref, v_ref, qi_ref, kiw_ref, ch_ref, sh_ref, ci_ref, si_ref, qn_ref, kn_ref,
                  qo_ref, ko_ref, kb_ref, vo_ref, vb_ref, qio_ref, kio_ref, kilo_ref, kihi_ref):
    ch, sh, ci, si = ch_ref[...], sh_ref[...], ci_ref[...], si_ref[...]
    tm = ch.shape[0]
    lane = lax.broadcasted_iota(jnp.int32, ch.shape, 1)
    first_half = (lane & (IDX_DIM - 1)) < (IDX_DIM // 2)
    for h in range(N_HEADS):
        sl = slice(h * HEAD_DIM, (h + 1) * HEAD_DIM)
        qo_ref[:, sl] = _rope_head(_head_norm(q_ref[:, sl], qn_ref[...]), ch, sh).astype(qo_ref.dtype)
    for h in range(N_KV_HEADS):
        sl = slice(h * HEAD_DIM, (h + 1) * HEAD_DIM)
        kr = _rope_head(_head_norm(k_ref[:, sl], kn_ref[...]), ch, sh)
        ko_ref[pl.ds(h, tm, stride=N_KV_HEADS), :] = kr
        vo_ref[pl.ds(h, tm, stride=N_KV_HEADS), :] = v_ref[:, sl]
        kb_ref[:, sl] = kr.astype(kb_ref.dtype)
    vb_ref[...] = v_ref[...].astype(vb_ref.dtype)
    for h in range(IDX_W // LANES):
        sl = slice(h * LANES, (h + 1) * LANES)
        qio_ref[:, sl] = _rope_idx(qi_ref[:, sl], ci, si, first_half).astype(qio_ref.dtype)
    kir = _rope_idx(kiw_ref[...], ci, si, first_half)
    kio_ref[...] = kir[:, :IDX_DIM]
    zero = jnp.zeros_like(kir)
    kilo_ref[...] = jnp.where(lane < IDX_DIM, kir, zero).astype(kilo_ref.dtype)
    kihi_ref[...] = jnp.where(lane < IDX_DIM, zero, pltpu.roll(kir, IDX_DIM, axis=1)).astype(kihi_ref.dtype)


def _qk_post(proj, ki_col, tabs, tab_rows, q_norm, k_norm, act_dtype):
    m = proj.shape[0]
    tm = _pick(min(m, tab_rows), (512, 256, 128, 8))
    n_tab = tab_rows // tm
    col = lambda off, w: (lambda i: (i, off // w))
    tab_spec = pl.BlockSpec((tm, LANES), lambda i: (i % n_tab, 0))
    row = lambda w: pl.BlockSpec((tm, w), lambda i: (i, 0))
    head_rows = pl.BlockSpec((tm * N_KV_HEADS, HEAD_DIM), lambda i: (i, 0))
    ki_blk = ki_col // LANES
    outs = pl.pallas_call(
        _qk_post_body,
        grid=(m // tm,),
        in_specs=[
            pl.BlockSpec((tm, ATT_W), col(COL_Q, ATT_W)),
            pl.BlockSpec((tm, KV_W), col(COL_K, KV_W)),
            pl.BlockSpec((tm, KV_W), col(COL_V, KV_W)),
            pl.BlockSpec((tm, IDX_W), col(COL_QI, IDX_W)),
            pl.BlockSpec((tm, LANES), lambda i: (i, ki_blk)),
            tab_spec, tab_spec, tab_spec, tab_spec,
            pl.BlockSpec((1, HEAD_DIM), lambda i: (0, 0)),
            pl.BlockSpec((1, HEAD_DIM), lambda i: (0, 0)),
        ],
        out_specs=[row(ATT_W), head_rows, row(KV_W), head_rows, row(KV_W), row(IDX_W), row(IDX_DIM),
                   row(LANES), row(LANES)],
        out_shape=[
            jax.ShapeDtypeStruct((m, ATT_W), act_dtype),
            jax.ShapeDtypeStruct((m * N_KV_HEADS, HEAD_DIM), F32),
            jax.ShapeDtypeStruct((m, KV_W), act_dtype),
            jax.ShapeDtypeStruct((m * N_KV_HEADS, HEAD_DIM), F32),
            jax.ShapeDtypeStruct((m, KV_W), act_dtype),
            jax.ShapeDtypeStruct((m, IDX_W), act_dtype),
            jax.ShapeDtypeStruct((m, IDX_DIM), F32),
            jax.ShapeDtypeStruct((m, LANES), act_dtype),
            jax.ShapeDtypeStruct((m, LANES), act_dtype),
        ],
        compiler_params=_params(),
        name="qk_post",
    )(proj, proj, proj, proj, proj, *tabs, q_norm.reshape(1, HEAD_DIM), k_norm.reshape(1, HEAD_DIM))
    return outs


def _strict_upper(n):
    r = lax.broadcasted_iota(jnp.int32, (n, n), 0)
    c = lax.broadcasted_iota(jnp.int32, (n, n), 1)
    return jnp.where(r < c, 1.0, 0.0).astype(BF16)


def _strict_lower(n):
    r = lax.broadcasted_iota(jnp.int32, (n, n), 0)
    c = lax.broadcasted_iota(jnp.int32, (n, n), 1)
    return jnp.where(c < r, 1.0, 0.0).astype(BF16)


def _attn_prompt_body(q_ref, qi_ref, kiw_ref, k_ref, v_ref, kilo_ref, kihi_ref, o_ref,
                      key_scr, bias_scr, lg_scr, m_scr, l_scr, acc_scr, *, topk):
    i = pl.program_id(1)
    tq = q_ref.shape[0]
    wc = key_scr.shape[1]
    n_ch = i + 1
    wi_t = jnp.transpose(kiw_ref[...])
    q_pos = i * tq + lax.broadcasted_iota(jnp.int32, (wc, tq), 1)
    k_off = lax.broadcasted_iota(jnp.int32, (wc, tq), 0)

    def key_count(mask):
        return jnp.sum(jnp.where(mask, 1.0, 0.0), axis=0, keepdims=True)

    def score_chunk(kc, carry):
        r0 = pl.multiple_of(kc * wc, wc)
        ki_lo = kilo_ref[pl.ds(r0, wc), :]
        ki_hi = kihi_ref[pl.ds(r0, wc), :]
        score = jnp.zeros((wc, tq), F32)
        for pair in range(N_IDX_HEADS // 2):
            x = qi_ref[:, pair * LANES:(pair + 1) * LANES]
            w_lo = wi_t[IDX_DIM + 2 * pair:IDX_DIM + 2 * pair + 1, :]
            w_hi = wi_t[IDX_DIM + 2 * pair + 1:IDX_DIM + 2 * pair + 2, :]
            score = score + w_lo * jnp.maximum(_nt_dot(ki_lo, x), 0.0)
            score = score + w_hi * jnp.maximum(_nt_dot(ki_hi, x), 0.0)
        key_scr[kc] = jnp.where(kc * wc + k_off <= q_pos, _sort_key(score * IDX_SCALE), INT_MIN)
        return carry

    lax.fori_loop(0, n_ch, score_chunk, 0)

    def search(it, res):
        trial = res + jnp.left_shift(jnp.int32(1), 31 - it)

        def count_chunk(kc, acc):
            hit = jnp.where(key_scr[kc] >= trial, 1.0, 0.0)
            for j in range(wc // SUBLANES):
                acc = acc + hit[j * SUBLANES:(j + 1) * SUBLANES, :]
            return acc

        acc = lax.fori_loop(0, n_ch, count_chunk, jnp.zeros((SUBLANES, tq), F32))
        return jnp.where(jnp.sum(acc, axis=0, keepdims=True) >= float(topk), trial, res)

    thr = lax.fori_loop(0, 32, search, jnp.full((1, tq), INT_MIN, jnp.int32))

    def bias_chunk(kc, n_keep):
        key = key_scr[kc]
        keep = (key >= thr) & (key > KEY_NEG_INF)
        bias_scr[kc] = jnp.transpose(jnp.where(keep, 0.0, MASKED))
        return n_keep + key_count(keep)

    n_keep = lax.fori_loop(0, n_ch, bias_chunk, jnp.zeros((1, tq), F32))

    @pl.when(jnp.max(n_keep) > float(topk))
    def _():
        n_gt = lax.fori_loop(0, n_ch, lambda kc, n: n + key_count(key_scr[kc] > thr), jnp.zeros((1, tq), F32))
        need = float(topk) - n_gt
        lower = _strict_lower(wc)

        def tie_chunk(kc, before):
            key = key_scr[kc]
            eq = key == thr
            eqf = jnp.where(eq, 1.0, 0.0)
            rank = before + _dot(lower, eqf.astype(BF16))
            keep = ((key > thr) | (eq & (rank < need))) & (key > KEY_NEG_INF)
            bias_scr[kc] = jnp.transpose(jnp.where(keep, 0.0, MASKED))
            return before + jnp.sum(eqf, axis=0, keepdims=True)

        lax.fori_loop(0, n_ch, tie_chunk, jnp.zeros((1, tq), F32))

    m_scr[...] = jnp.full(m_scr.shape, MASKED, F32)
    l_scr[...] = jnp.zeros(l_scr.shape, F32)
    acc_scr[...] = jnp.zeros(acc_scr.shape, F32)
    lane_blocks = [slice(j * LANES, (j + 1) * LANES) for j in range(wc // LANES)]

    def logit_chunk(kc, carry):
        r0 = pl.multiple_of(kc * wc, wc)
        bias = bias_scr[kc]
        for h in range(N_HEADS):
            g = h // GQA_GROUP
            kg = k_ref[pl.ds(r0, wc), g * HEAD_DIM:(g + 1) * HEAD_DIM]
            lg = _nt_dot(q_ref[:, h * HEAD_DIM:(h + 1) * HEAD_DIM], kg) * ATT_SCALE + bias
            lg_scr[h, kc] = lg
            mx = m_scr[h]
            for lb in lane_blocks:
                mx = jnp.maximum(mx, lg[:, lb])
            m_scr[h] = mx
        return carry

    lax.fori_loop(0, n_ch, logit_chunk, 0)
    for h in range(N_HEADS):
        m_scr[h] = jnp.broadcast_to(jnp.max(m_scr[h], axis=1, keepdims=True), (tq, LANES))

    def prob_chunk(kc, carry):
        r0 = pl.multiple_of(kc * wc, wc)
        for h in range(N_HEADS):
            g = h // GQA_GROUP
            sl = slice(h * HEAD_DIM, (h + 1) * HEAD_DIM)
            vg = v_ref[pl.ds(r0, wc), g * HEAD_DIM:(g + 1) * HEAD_DIM]
            lg = lg_scr[h, kc]
            mx = m_scr[h]
            ps = [jnp.exp(lg[:, lb] - mx) for lb in lane_blocks]
            den = l_scr[h]
            for p in ps:
                den = den + p
            l_scr[h] = den
            acc_scr[:, sl] = acc_scr[:, sl] + _dot(jnp.concatenate(ps, axis=1).astype(BF16), vg)
        return carry

    lax.fori_loop(0, n_ch, prob_chunk, 0)
    for h in range(N_HEADS):
        sl = slice(h * HEAD_DIM, (h + 1) * HEAD_DIM)
        den = jnp.sum(l_scr[h], axis=1, keepdims=True)
        o_ref[:, sl] = (acc_scr[:, sl] / den).astype(o_ref.dtype)


def _attn_prompt(q, qi, proj, ki_col, k, v, ki_lo, ki_hi, nb, s):
    tq = _pick(s, (256, 128))
    nq = s // tq
    topk = min(TOPK_MAX, s // 4)
    ki_blk = ki_col // LANES
    qrow = lambda w: pl.BlockSpec((tq, w), lambda b, i: (b * nq + i, 0))
    full = lambda w: pl.BlockSpec((s, w), lambda b, i: (b, 0))
    return pl.pallas_call(
        functools.partial(_attn_prompt_body, topk=topk),
        grid=(nb, nq),
        in_specs=[qrow(ATT_W), qrow(IDX_W), pl.BlockSpec((tq, LANES), lambda b, i: (b * nq + i, ki_blk)),
                  full(KV_W), full(KV_W), full(LANES), full(LANES)],
        out_specs=qrow(ATT_W),
        out_shape=jax.ShapeDtypeStruct((nb * s, ATT_W), BF16),
        scratch_shapes=[pltpu.VMEM((nq, tq, tq), jnp.int32), pltpu.VMEM((nq, tq, tq), F32),
                        pltpu.VMEM((N_HEADS, nq, tq, tq), F32),
                        pltpu.VMEM((N_HEADS, tq, LANES), F32), pltpu.VMEM((N_HEADS, tq, LANES), F32),
                        pltpu.VMEM((tq, ATT_W), F32)],
        compiler_params=_params(),
        name="attn_prompt",
    )(q, qi, proj, k, v, ki_lo, ki_hi)


def _s5_discretize_body(are_ref, aim_ref, ldt_ref, bre_ref, bim_ref, abre_ref, abim_ref, bbre_ref, bbim_ref):
    dt = jnp.exp(ldt_ref[...])
    lr, li = are_ref[...], aim_ref[...]
    mag = jnp.exp(lr * dt)
    ab_re, ab_im = mag * jnp.cos(li * dt), mag * jnp.sin(li * dt)
    zr, zi = ab_re - 1.0, ab_im
    den = lr * lr + li * li
    fr = (zr * lr + zi * li) / den
    fi = (zi * lr - zr * li) / den
    abre_ref[...] = ab_re
    abim_ref[...] = ab_im
    br, bi = bre_ref[...], bim_ref[...]
    bbre_ref[...] = fr * br - fi * bi
    bbim_ref[...] = fr * bi + fi * br


def _s5_discretize(a_re, a_im, log_dt, b_re, b_im):
    depth = a_re.shape[0]
    gp = pl.BlockSpec((None, SSM_GROUPS, 1, SSM_P), lambda l: (l, 0, 0, 0))
    gwp = pl.BlockSpec((None, SSM_GROUPS, SSM_GROUP, SSM_P), lambda l: (l, 0, 0, 0))
    bt = lambda b: jnp.swapaxes(b, 2, 3)
    a4 = lambda a: a.reshape(depth, SSM_GROUPS, 1, SSM_P)
    return pl.pallas_call(
        _s5_discretize_body,
        grid=(depth,),
        in_specs=[gp, gp, pl.BlockSpec((None, SSM_GROUPS, 1, 1), lambda l: (l, 0, 0, 0)), gwp, gwp],
        out_specs=[gp, gp, gwp, gwp],
        out_shape=[jax.ShapeDtypeStruct((depth, SSM_GROUPS, 1, SSM_P), F32)] * 2
        + [jax.ShapeDtypeStruct((depth, SSM_GROUPS, SSM_GROUP, SSM_P), F32)] * 2,
        compiler_params=_params(),
        name="s5_discretize",
    )(a4(a_re), a4(a_im), log_dt.reshape(depth, SSM_GROUPS, 1, 1), bt(b_re), bt(b_im))


def _block_diag_in(bb_t):
    depth = bb_t.shape[0]
    gpb = SSM_GROUPS // SSM_BLOCKS
    x = bb_t.reshape(depth, SSM_BLOCKS, gpb, SSM_GROUP, SSM_P)
    eye = jnp.eye(gpb, dtype=bb_t.dtype)
    out = jnp.einsum('lkgwp,gh->lkgwhp', x, eye)
    return out.reshape(depth, SSM_BLOCKS, gpb * SSM_GROUP, gpb * SSM_P).astype(BF16)


def _block_diag_out(c):
    depth = c.shape[0]
    gpb = SSM_GROUPS // SSM_BLOCKS
    x = c.reshape(depth, SSM_BLOCKS, gpb, SSM_GROUP, SSM_P)
    eye = jnp.eye(gpb, dtype=c.dtype)
    out = jnp.einsum('lkgwp,gh->lkgphw', x, eye)
    return out.reshape(depth, SSM_BLOCKS, gpb * SSM_P, gpb * SSM_GROUP).astype(BF16)


def _s5_body(u_ref, bbre_ref, bbim_ref, cre_ref, cim_ref, abre_ref, abim_ref, d_ref, h0re_ref, h0im_ref,
             g_ref, htre_ref, htim_ref, bure, buim, hre, him, ust, gst, *, nb):
    c = pl.program_id(0)
    rows = bure.shape[0]
    n_t = rows // nb
    per_tile = SUBLANES // nb

    @pl.when(c == 0)
    def _():
        hre[...] = h0re_ref[...]
        him[...] = h0im_ref[...]

    for k in range(SSM_BLOCKS):
        ln = slice(k * LANES, (k + 1) * LANES)
        for b in range(nb):
            if n_t == 1:
                ust[k, b:b + 1, :] = u_ref[b, :, ln]
            else:
                ust[k, pl.ds(b, n_t, stride=nb), :] = u_ref[b, :, ln]
        uk = ust[k].astype(BF16)
        st = slice(k * SSM_BLOCK_STATE, (k + 1) * SSM_BLOCK_STATE)
        bure[:, st] = _dot(uk, bbre_ref[k])
        buim[:, st] = _dot(uk, bbim_ref[k])

    for cc in range(SSM_STATE // SCAN_LANES):
        sl = slice(cc * SCAN_LANES, (cc + 1) * SCAN_LANES)
        ar = jnp.broadcast_to(abre_ref[:, sl], (nb, SCAN_LANES))
        ai = jnp.broadcast_to(abim_ref[:, sl], (nb, SCAN_LANES))

        def step(j, carry, sl=sl, ar=ar, ai=ai):
            hr, hi = carry
            r0 = pl.multiple_of(j * SUBLANES, SUBLANES)
            xr = bure[pl.ds(r0, SUBLANES), sl]
            xi = buim[pl.ds(r0, SUBLANES), sl]
            out_r, out_i = [], []
            for t in range(per_tile):
                nr = ar * hr - ai * hi + xr[t * nb:(t + 1) * nb]
                ni = ar * hi + ai * hr + xi[t * nb:(t + 1) * nb]
                hr, hi = nr, ni
                out_r.append(hr)
                out_i.append(hi)
            bure[pl.ds(r0, SUBLANES), sl] = out_r[0] if per_tile == 1 else jnp.concatenate(out_r, axis=0)
            buim[pl.ds(r0, SUBLANES), sl] = out_i[0] if per_tile == 1 else jnp.concatenate(out_i, axis=0)
            return hr, hi

        hr, hi = lax.fori_loop(0, rows // SUBLANES, step, (hre[:, sl], him[:, sl]))
        hre[:, sl] = hr
        him[:, sl] = hi

    for k in range(SSM_BLOCKS):
        st = slice(k * SSM_BLOCK_STATE, (k + 1) * SSM_BLOCK_STATE)
        ln = slice(k * LANES, (k + 1) * LANES)
        y = _dot(bure[:, st].astype(BF16), cre_ref[k]) - _dot(buim[:, st].astype(BF16), cim_ref[k])
        y = y + d_ref[:, ln] * ust[k]
        gst[k] = _gelu_tanh(y)
        for b in range(nb):
            if n_t == 1:
                g_ref[b, :, ln] = gst[k, b:b + 1, :]
            else:
                g_ref[b, :, ln] = gst[k, pl.ds(b, n_t, stride=nb), :]

    @pl.when(c == pl.num_programs(0) - 1)
    def _():
        htre_ref[...] = hre[...]
        htim_ref[...] = him[...]


def _s5(proj, nb, s, l, bb_re, bb_im, c_re, c_im, ab_re, ab_im, d, h0_re, h0_im):
    n_t = min(s, 512 // nb)
    rows = n_t * nb
    blk4 = lambda a: pl.BlockSpec((None,) + a.shape[1:], lambda c: (l, 0, 0, 0))
    full2 = lambda a: pl.BlockSpec(a.shape, lambda c: (0, 0))
    vec = lambda a: a.reshape(1, -1)
    ab_re, ab_im, d = vec(ab_re), vec(ab_im), vec(d)
    u_spec = pl.BlockSpec((nb, n_t, SSM_W), lambda c: (0, c, 0))
    g, h_re, h_im = pl.pallas_call(
        functools.partial(_s5_body, nb=nb),
        grid=(s // n_t,),
        in_specs=[u_spec, blk4(bb_re), blk4(bb_im), blk4(c_re), blk4(c_im),
                  full2(ab_re), full2(ab_im), full2(d), full2(h0_re), full2(h0_im)],
        out_specs=[u_spec, full2(h0_re), full2(h0_im)],
        out_shape=[jax.ShapeDtypeStruct((nb, s, SSM_W), F32),
                   jax.ShapeDtypeStruct(h0_re.shape, F32), jax.ShapeDtypeStruct(h0_im.shape, F32)],
        scratch_shapes=[pltpu.VMEM((rows, SSM_STATE), F32), pltpu.VMEM((rows, SSM_STATE), F32),
                        pltpu.VMEM((nb, SSM_STATE), F32), pltpu.VMEM((nb, SSM_STATE), F32),
                        pltpu.VMEM((SSM_BLOCKS, rows, LANES), F32), pltpu.VMEM((SSM_BLOCKS, rows, LANES), F32)],
        compiler_params=_params(),
        name="s5_scan",
    )(proj.reshape(nb, s, proj.shape[1]), bb_re, bb_im, c_re, c_im, ab_re, ab_im, d, h0_re, h0_im)
    return g.reshape(nb * s, SSM_W), h_re, h_im


IDX_PAGES_PER_STEP = 16


def _sample_index_body(pt_ref, qi_ref, wi_ref, kin_ref, *rest, n_pg, topk):
    del pt_ref
    pages = rest[:n_pg]
    idx_ref, nsel_ref, bnew_ref, sc_scr = rest[n_pg:]
    c = pl.program_id(1)
    n_pages = sc_scr.shape[0]
    qi = qi_ref[...].astype(BF16)
    wi = wi_ref[...]

    def weighted(d):
        return jnp.sum(wi * jnp.maximum(d, 0.0), axis=0, keepdims=True) * IDX_SCALE

    step_keys = jnp.concatenate([pages[p][...] for p in range(n_pg)], axis=1).astype(BF16)
    step_scores = weighted(_dot(qi, step_keys))
    for p in range(n_pg):
        sc_scr[pl.ds(c * n_pg + p, 1), :] = step_scores[:, p * PAGE_SIZE:(p + 1) * PAGE_SIZE]

    @pl.when(c == pl.num_programs(1) - 1)
    def _():
        kin = kin_ref[...].astype(BF16).astype(F32)
        d_new = jnp.sum(qi.astype(F32) * kin, axis=1, keepdims=True)
        key_new = _sort_key(jnp.broadcast_to(weighted(d_new), (1, LANES)))
        key = _sort_key(sc_scr[...])

        def count(mask, mask_new):
            per_lane = jnp.sum(jnp.where(mask, 1.0, 0.0), axis=0, keepdims=True)
            return jnp.sum(per_lane, axis=1, keepdims=True) + jnp.where(mask_new, 1.0, 0.0)

        def body(it, res):
            trial = res + jnp.left_shift(jnp.int32(1), 31 - it)
            return jnp.where(count(key >= trial, key_new >= trial) >= float(topk), trial, res)

        thr = lax.fori_loop(0, 32, body, jnp.full((1, LANES), INT_MIN, jnp.int32))
        need = float(topk) - count(key > thr, key_new > thr)
        eq = key == thr
        eqf = jnp.where(eq, 1.0, 0.0).astype(BF16)
        in_page = _dot(eqf, _strict_upper(LANES))
        lower = _strict_lower(n_pages)
        per_page = jnp.sum(jnp.where(eq, 1.0, 0.0), axis=1, keepdims=True)
        earlier = _dot(lower, jnp.broadcast_to(per_page, (n_pages, LANES)).astype(BF16))
        keep = ((key > thr) | (eq & (in_page + earlier < need))) & (key > KEY_NEG_INF)
        n_eq_past = jnp.sum(jnp.sum(jnp.where(eq, 1.0, 0.0), axis=0, keepdims=True), axis=1, keepdims=True)
        keep_new = ((key_new > thr) | ((key_new == thr) & (n_eq_past < need))) & (key_new > KEY_NEG_INF)
        bnew_ref[...] = jnp.where(keep_new, 0.0, MASKED)

        keepf = jnp.where(keep, 1.0, 0.0).astype(BF16)
        in_page_rank = jnp.where(keep, _dot(keepf, _strict_upper(LANES)), -1.0).astype(BF16)
        per_page = _nt_dot(jnp.ones((SUBLANES, LANES), BF16), keepf)[:1]
        pr = lax.broadcasted_iota(jnp.int32, (n_pages, n_pages), 0)
        pc = lax.broadcasted_iota(jnp.int32, (n_pages, n_pages), 1)
        upto = jnp.where(pr <= pc, 1.0, 0.0).astype(BF16)
        cum = _dot(jnp.broadcast_to(per_page, (SUBLANES, n_pages)).astype(BF16), upto)[:1]
        slot = lax.broadcasted_iota(jnp.int32, (topk, 1), 0).astype(F32)
        passed = cum <= slot
        page_of = jnp.sum(jnp.where(passed, 1.0, 0.0), axis=1, keepdims=True)
        start_of = jnp.sum(jnp.where(passed, per_page, 0.0), axis=1, keepdims=True)
        page_lane = lax.broadcasted_iota(jnp.int32, (topk, n_pages), 1).astype(F32)
        ranks = _dot(jnp.where(page_lane == page_of, 1.0, 0.0).astype(BF16), in_page_rank)
        lane = lax.broadcasted_iota(jnp.int32, (topk, LANES), 1).astype(F32)
        lane_of = jnp.sum(jnp.where(ranks == slot - start_of, lane, 0.0), axis=1, keepdims=True)
        n_kept = jnp.sum(per_page, axis=1, keepdims=True)
        idx_ref[...] = jnp.where(slot < n_kept, page_of * PAGE_SIZE + lane_of, 0.0).astype(jnp.int32)
        nsel_ref[...] = jnp.broadcast_to(n_kept, (1, LANES)).astype(jnp.int32)


def _sample_index(page_table, cache_kidx_t, layer, qi, wi, ki_new):
    nb, n_pages = page_table.shape
    n_pg = _pick(n_pages, (IDX_PAGES_PER_STEP, 8, 4, 2, 1))
    topk = min(TOPK_MAX, (n_pages * PAGE_SIZE + 1) // 4)
    page_spec = lambda p: pl.BlockSpec((None, None, IDX_DIM, PAGE_SIZE),
                                       lambda b, c, pt: (layer, pt[b, c * n_pg + p], 0, 0))
    per_b = lambda shape: pl.BlockSpec((None,) + shape, lambda b, c, pt: (b, 0, 0))
    grid_spec = pltpu.PrefetchScalarGridSpec(
        num_scalar_prefetch=1,
        grid=(nb, n_pages // n_pg),
        in_specs=[per_b((N_IDX_HEADS, IDX_DIM)), per_b((N_IDX_HEADS, 1)), per_b((1, IDX_DIM))]
        + [page_spec(p) for p in range(n_pg)],
        out_specs=[per_b((topk, 1)), per_b((1, LANES)), per_b((1, LANES))],
        scratch_shapes=[pltpu.VMEM((n_pages, LANES), F32)],
    )
    idx, n_sel, bias_new = pl.pallas_call(
        functools.partial(_sample_index_body, n_pg=n_pg, topk=topk),
        grid_spec=grid_spec,
        out_shape=[jax.ShapeDtypeStruct((nb, topk, 1), jnp.int32), jax.ShapeDtypeStruct((nb, 1, LANES), jnp.int32),
                   jax.ShapeDtypeStruct((nb, 1, LANES), F32)],
        compiler_params=_params(),
        name="sample_index",
    )(page_table, qi.reshape(nb, N_IDX_HEADS, IDX_DIM), wi.reshape(nb, N_IDX_HEADS, 1),
      ki_new.reshape(nb, 1, IDX_DIM), *([cache_kidx_t] * n_pg))
    return idx.reshape(nb, topk), n_sel[:, 0, 0], bias_new


SLOTS_PER_TILE = SUBLANES // N_KV_HEADS


def _group_rows(n_cols):
    head = lax.broadcasted_iota(jnp.int32, (N_HEADS, n_cols), 0)
    return [(head >= g * GQA_GROUP) & (head < (g + 1) * GQA_GROUP) for g in range(N_KV_HEADS)]


def _sample_attn_body(pt_ref, idx_ref, nsel_ref, q_ref, kn_ref, vn_ref, bnew_ref, odd_ref, ck_hbm, cv_hbm,
                      o_ref, kbuf, vbuf, sem, *, layer, topk):
    b = pl.program_id(0)

    def tile_copy(src_hbm, buf, which, phys, src_row, dst_row):
        return pltpu.make_async_copy(src_hbm.at[layer, phys, pl.ds(src_row, SUBLANES), :],
                                     buf.at[pl.ds(dst_row, SUBLANES), :], sem.at[which])

    def issue(r, carry):
        idx = idx_ref[b, r]
        phys = pt_ref[b, idx // PAGE_SIZE]
        src_row = pl.multiple_of(((idx % PAGE_SIZE) // SLOTS_PER_TILE) * SUBLANES, SUBLANES)
        dst_row = pl.multiple_of(r * SUBLANES, SUBLANES)
        tile_copy(ck_hbm, kbuf, 0, phys, src_row, dst_row).start()
        tile_copy(cv_hbm, vbuf, 1, phys, src_row, dst_row).start()
        return carry

    lax.fori_loop(0, topk, issue, 0)

    def drain(r, carry):
        tile_copy(ck_hbm, kbuf, 0, 0, 0, 0).wait()
        tile_copy(cv_hbm, vbuf, 1, 0, 0, 0).wait()
        return carry

    lax.fori_loop(0, topk, drain, 0)

    q = q_ref[...].astype(BF16)
    odd = odd_ref[...] == 1
    live = lax.broadcasted_iota(jnp.int32, (1, topk), 1) < nsel_ref[b]

    def gathered(buf, g):
        first = buf[pl.ds(g, topk, stride=SUBLANES), :]
        second = buf[pl.ds(N_KV_HEADS + g, topk, stride=SUBLANES), :]
        return jnp.where(odd, second, first).astype(BF16)

    in_group = _group_rows(topk)
    lg = jnp.zeros((N_HEADS, topk), F32)
    for g in range(N_KV_HEADS):
        lg = jnp.where(in_group[g], _nt_dot(q, gathered(kbuf, g)), lg)
    lg = jnp.where(live, lg * ATT_SCALE, MASKED)
    qf = q.astype(F32)
    kn = kn_ref[...].astype(BF16).astype(F32)
    vn = vn_ref[...].astype(BF16).astype(F32)
    lg_new = jnp.sum(qf * kn, axis=1, keepdims=True) * ATT_SCALE + bnew_ref[:, :1]
    mx = jnp.maximum(jnp.max(lg, axis=1, keepdims=True), lg_new)
    pr = jnp.where(live, jnp.exp(lg - mx), 0.0)
    pr_new = jnp.where(bnew_ref[:, :1] == 0.0, jnp.exp(lg_new - mx), 0.0)
    den = jnp.sum(pr, axis=1, keepdims=True) + pr_new
    prb = pr.astype(BF16)
    in_group_d = _group_rows(HEAD_DIM)
    pv = jnp.zeros((N_HEADS, HEAD_DIM), F32)
    for g in range(N_KV_HEADS):
        pv = jnp.where(in_group_d[g], _dot(prb, gathered(vbuf, g)), pv)
    o_ref[...] = ((pv + pr_new.astype(BF16).astype(F32) * vn) / den).astype(o_ref.dtype)


def _sample_attn(page_table, cache_k2, cache_v2, layer, idx, n_sel, bias_new, q, k_new, v_new):
    nb = page_table.shape[0]
    topk = idx.shape[1]
    per_b = lambda shape: pl.BlockSpec((None,) + shape, lambda b, pt, ix, ns: (b, 0, 0))
    per_head = lambda a: jnp.repeat(a.reshape(nb, N_KV_HEADS, HEAD_DIM), GQA_GROUP, axis=1)
    odd = (idx % SLOTS_PER_TILE).reshape(nb, topk, 1)
    grid_spec = pltpu.PrefetchScalarGridSpec(
        num_scalar_prefetch=3,
        grid=(nb,),
        in_specs=[per_b((N_HEADS, HEAD_DIM)), per_b((N_HEADS, HEAD_DIM)), per_b((N_HEADS, HEAD_DIM)),
                  per_b((1, LANES)), per_b((topk, 1)),
                  pl.BlockSpec(memory_space=pl.ANY), pl.BlockSpec(memory_space=pl.ANY)],
        out_specs=per_b((N_HEADS, HEAD_DIM)),
        scratch_shapes=[pltpu.VMEM((topk * SUBLANES, HEAD_DIM), F32), pltpu.VMEM((topk * SUBLANES, HEAD_DIM), F32),
                        pltpu.SemaphoreType.DMA((2,))],
    )
    out = pl.pallas_call(
        functools.partial(_sample_attn_body, layer=layer, topk=topk),
        grid_spec=grid_spec,
        out_shape=jax.ShapeDtypeStruct((nb, N_HEADS, HEAD_DIM), F32),
        compiler_params=_params(),
        name="sample_attn",
    )(page_table, idx, n_sel, q.reshape(nb, N_HEADS, HEAD_DIM), per_head(k_new), per_head(v_new), bias_new, odd,
      cache_k2, cache_v2)
    return out.reshape(nb, ATT_W)


def _layer_tail(proj, g, y_att, x, w, l, act_dtype):
    z = _glu(g, w["w_glu"], l, act_dtype)
    merged = _merge(z, y_att, w["w_branch"], l, proj, act_dtype)
    x1, hn = _out_norm(merged, w["w_out"], l, x, w["norm_ffn"][l], act_dtype)
    act = _ffn_up(hn, w["w_ffn_up"], l, act_dtype)
    return _ffn_down(act, w["w_ffn_down"], l, x1)


def _s5_layer(proj, nb, s, w, l, h0_re, h0_im):
    return _s5(proj, nb, s, l, w["bb_re"], w["bb_im"], w["c_re"], w["c_im"], w["ab_re"][l], w["ab_im"][l],
               w["ssm_d"][l], h0_re, h0_im)


def _prompt_layer(x, nb, s, tabs, w, l):
    proj = _in_proj(x, w["norm_mix"][l], w["w_in"], l, PROJ_TN, F32)
    q, k_out, k_bf, v_out, v_bf, qi, ki_out, ki_lo, ki_hi = _qk_post(
        proj, w["ki_col"], tabs, s, w["q_norm"][l], w["k_norm"][l], BF16)
    y_att = _attn_prompt(q, qi, proj, w["ki_col"], k_bf, v_bf, ki_lo, ki_hi, nb, s)
    zeros = jnp.zeros((nb, SSM_STATE), F32)
    g, h_re, h_im = _s5_layer(proj, nb, s, w, l, zeros, zeros)
    x2 = _layer_tail(proj, g, y_att, x, w, l, BF16)
    return x2, k_out, v_out, ki_out, h_re, h_im


def _sample_layer(x, nb, tabs, w, l, page_table, cache_k2, cache_v2, cache_kidx_t, h0_re, h0_im):
    proj = _in_proj(x, w["norm_mix"][l], w["w_in"], l, PROJ_TN, F32)
    ki_col = w["ki_col"]
    q, k_out, _, v_out, _, qi, ki_out, _, _ = _qk_post(proj, ki_col, tabs, nb, w["q_norm"][l], w["k_norm"][l], F32)
    wi = proj[:, ki_col + IDX_DIM:ki_col + IDX_DIM + N_IDX_HEADS]
    idx, n_sel, bias_new = _sample_index(page_table, cache_kidx_t, l, qi, wi, ki_out)
    y_att = _sample_attn(page_table, cache_k2, cache_v2, l, idx, n_sel, bias_new, q, k_out, v_out)
    g, h_re, h_im = _s5_layer(proj, nb, 1, w, l, h0_re, h0_im)
    x2 = _layer_tail(proj, g, y_att, x, w, l, F32)
    return x2, k_out, v_out, ki_out, h_re, h_im


def kernel(x_prompt, x_sample, cache_k, cache_v, cache_kidx, state_ssm_re, state_ssm_im, page_table, norm_mix, w_in, q_norm, k_norm, ssm_a_re, ssm_a_im, ssm_log_dt, ssm_b_re, ssm_b_im, ssm_c_re, ssm_c_im, ssm_d, w_glu, w_branch, w_out, norm_ffn, w_ffn_up, w_ffn_down):
    nb_p, s, d_model = x_prompt.shape
    nb_s, dec_seq, _ = x_sample.shape
    assert dec_seq == 1, "the sample group decodes one token per sequence"
    depth = w_in.shape[0]
    past = page_table.shape[1] * PAGE_SIZE

    ki_orig = COL_GATE
    gate_orig = ki_orig + IDX_DIM + N_IDX_HEADS
    n_gate = w_in.shape[2] - gate_orig
    ki_col = COL_GATE + n_gate
    assert ki_col % LANES == 0
    cols = ki_col + LANES
    cols_pad = -(-cols // PROJ_TN) * PROJ_TN
    w_in_t = jnp.swapaxes(w_in, 1, 2)
    w_in_r = jnp.concatenate(
        [w_in_t[:, :COL_GATE], w_in_t[:, gate_orig:], w_in_t[:, ki_orig:gate_orig],
         jnp.zeros((depth, cols_pad - ki_col - IDX_DIM - N_IDX_HEADS, d_model), w_in.dtype)], axis=1).astype(BF16)

    ab_re, ab_im, bbt_re, bbt_im = _s5_discretize(ssm_a_re, ssm_a_im, ssm_log_dt, ssm_b_re, ssm_b_im)
    bb_re, bb_im = _block_diag_in(bbt_re), _block_diag_in(bbt_im)
    c_re, c_im = _block_diag_out(ssm_c_re), _block_diag_out(ssm_c_im)

    tabs_p = _rope_tables(jnp.arange(s, dtype=jnp.int32))
    tabs_s = _rope_tables(jnp.full((nb_s,), past, dtype=jnp.int32))

    xp = x_prompt.reshape(nb_p * s, d_model)
    xs = x_sample.reshape(nb_s, d_model)
    cache_k2 = cache_k.reshape(cache_k.shape[:2] + (PAGE_SIZE * N_KV_HEADS, HEAD_DIM))
    cache_v2 = cache_v.reshape(cache_v.shape[:2] + (PAGE_SIZE * N_KV_HEADS, HEAD_DIM))
    cache_kidx_t = jnp.swapaxes(cache_kidx, 2, 3)
    w = dict(ki_col=ki_col, norm_mix=norm_mix, w_in=w_in_r, q_norm=q_norm, k_norm=k_norm,
             bb_re=bb_re, bb_im=bb_im, c_re=c_re, c_im=c_im, ab_re=ab_re, ab_im=ab_im, ssm_d=ssm_d,
             w_glu=w_glu, w_branch=w_branch, w_out=w_out, norm_ffn=norm_ffn,
             w_ffn_up=w_ffn_up, w_ffn_down=w_ffn_down)
    outs_p, outs_s = [], []
    for l in range(depth):
        xp, *rest_p = _prompt_layer(xp, nb_p, s, tabs_p, w, l)
        xs, *rest_s = _sample_layer(xs, nb_s, tabs_s, w, l, page_table, cache_k2, cache_v2, cache_kidx_t,
                                    state_ssm_re[l].reshape(nb_s, SSM_STATE), state_ssm_im[l].reshape(nb_s, SSM_STATE))
        outs_p.append(rest_p)
        outs_s.append(rest_s)

    def stack(outs, idx, shape):
        return jnp.stack([o[idx].reshape(shape) for o in outs])

    kv_p, kv_s = (nb_p, s, N_KV_HEADS, HEAD_DIM), (nb_s, dec_seq, N_KV_HEADS, HEAD_DIM)
    st_p, st_s = (nb_p, SSM_GROUPS, SSM_P), (nb_s, SSM_GROUPS, SSM_P)
    return (xp.reshape(nb_p, s, d_model), xs.reshape(nb_s, dec_seq, d_model),
            stack(outs_p, 0, kv_p), stack(outs_p, 1, kv_p), stack(outs_p, 2, (nb_p, s, IDX_DIM)),
            stack(outs_p, 3, st_p), stack(outs_p, 4, st_p),
            stack(outs_s, 0, kv_s), stack(outs_s, 1, kv_s), stack(outs_s, 2, (nb_s, dec_seq, IDX_DIM)),
            stack(outs_s, 3, st_s), stack(outs_s, 4, st_s))
```

```python
import functools
import math

import jax
import jax.numpy as jnp
from jax import lax
from jax.experimental import pallas as pl
from jax.experimental.pallas import tpu as pltpu

F32 = jnp.float32
BF16 = jnp.bfloat16

SSM_W = 1024
SSM_GROUP = 16
SSM_GROUPS = SSM_W // SSM_GROUP
SSM_P = 64
SSM_STATE = SSM_GROUPS * SSM_P
N_HEADS = 8
N_KV_HEADS = 4
HEAD_DIM = 128
GQA_GROUP = N_HEADS // N_KV_HEADS
ATT_W = N_HEADS * HEAD_DIM
KV_W = N_KV_HEADS * HEAD_DIM
N_IDX_HEADS = 16
IDX_DIM = 64
IDX_W = N_IDX_HEADS * IDX_DIM
IDX_SCALE = (IDX_DIM * N_IDX_HEADS) ** -0.5
ATT_SCALE = HEAD_DIM ** -0.5
TOPK_MAX = 256
PAGE_SIZE = 128
ROPE_THETA = 10000.0
EPS = 1e-6

LANES = 128
SUBLANES = 8
VMEM_LIMIT = 56 * 1024 * 1024

COL_U = 0
COL_Q = SSM_W
COL_K = COL_Q + ATT_W
COL_V = COL_K + KV_W
COL_QI = COL_V + KV_W
COL_GATE = COL_QI + IDX_W
PROJ_TN = 768

MASKED = -1e30
INT_MIN = -(2 ** 31)
KEY_NEG_INF = -2139095041

SCAN_LANES = 512
SSM_BLOCKS = SSM_W // LANES
SSM_BLOCK_STATE = SSM_STATE // SSM_BLOCKS


def _params(**kw):
    return pltpu.CompilerParams(vmem_limit_bytes=VMEM_LIMIT, **kw)


def _pick(n, cands):
    for c in cands:
        if n % c == 0:
            return c
    return n


def _nt_dot(a, b):
    return lax.dot_general(a, b, (((1,), (1,)), ((), ())), preferred_element_type=F32)


def _dot(a, b):
    return jnp.dot(a, b, preferred_element_type=F32)


def _sigmoid(x):
    return 1.0 / (1.0 + jnp.exp(-x))


def _gelu_tanh(x):
    c = math.sqrt(2.0 / math.pi)
    return 0.5 * x * (1.0 + jnp.tanh(c * (x + 0.044715 * (x * x * x))))


def _sort_key(s):
    bits = pltpu.bitcast(s, jnp.int32)
    return bits ^ ((bits >> 31) & jnp.int32(0x7FFFFFFF))


def _rmsnorm_body(x_ref, g_ref, o_ref):
    x = x_ref[...].astype(F32)
    ms = jnp.mean(x * x, axis=-1, keepdims=True)
    o_ref[...] = ((x * lax.rsqrt(ms + EPS)) * g_ref[...]).astype(o_ref.dtype)


def _rmsnorm(x, g, out_dtype):
    m, d = x.shape
    tm = _pick(m, (512, 256, 128, 8))
    return pl.pallas_call(
        _rmsnorm_body,
        grid=(m // tm,),
        in_specs=[pl.BlockSpec((tm, d), lambda i: (i, 0)), pl.BlockSpec((1, d), lambda i: (0, 0))],
        out_specs=pl.BlockSpec((tm, d), lambda i: (i, 0)),
        out_shape=jax.ShapeDtypeStruct((m, d), out_dtype),
        compiler_params=_params(),
        name="rmsnorm",
    )(x, g.reshape(1, d))


def _mm_body(a_ref, w_ref, o_ref):
    o_ref[...] = _nt_dot(a_ref[...].astype(BF16), w_ref[...]).astype(o_ref.dtype)


def _in_proj(a, w, l, tn, out_dtype):
    m, k = a.shape
    n = w.shape[1]
    tm = _pick(m, (2048, 1024, 512, 256, 128, 8))
    return pl.pallas_call(
        _mm_body,
        grid=(n // tn, m // tm),
        in_specs=[pl.BlockSpec((tm, k), lambda j, i: (i, 0)), pl.BlockSpec((None, tn, k), lambda j, i: (l, j, 0))],
        out_specs=pl.BlockSpec((tm, tn), lambda j, i: (i, j)),
        out_shape=jax.ShapeDtypeStruct((m, n), out_dtype),
        compiler_params=_params(),
        name="in_proj",
    )(a, w)


def _cast_weight_once(step, w_ref, wb_scr):
    @pl.when(step == 0)
    def _():
        wb_scr[...] = w_ref[...].astype(BF16)


def _glu_body(g_ref, w_ref, o_ref, wb):
    _cast_weight_once(pl.program_id(0), w_ref, wb)
    g = g_ref[...]
    z = _dot(g.astype(BF16), wb[...])
    o_ref[...] = (g * _sigmoid(z)).astype(o_ref.dtype)


def _glu(g, w, l, out_dtype):
    m, n = g.shape
    tm = _pick(m, (512, 256, 128, 8))
    return pl.pallas_call(
        _glu_body,
        grid=(m // tm,),
        in_specs=[pl.BlockSpec((tm, n), lambda i: (i, 0)),
                  pl.BlockSpec((None, n, n), lambda i: (l, 0, 0), pipeline_mode=pl.Buffered(1))],
        out_specs=pl.BlockSpec((tm, n), lambda i: (i, 0)),
        out_shape=jax.ShapeDtypeStruct((m, n), out_dtype),
        scratch_shapes=[pltpu.VMEM((n, n), BF16)],
        compiler_params=_params(),
        name="glu",
    )(g, w)


def _merge_body(a0_ref, a1_ref, w0_ref, w1_ref, g0_ref, g1_ref, o_ref, wb0, wb1):
    _cast_weight_once(pl.program_id(1), w0_ref, wb0)
    _cast_weight_once(pl.program_id(1), w1_ref, wb1)
    y0 = _dot(a0_ref[...].astype(BF16), wb0[...])
    y1 = _dot(a1_ref[...].astype(BF16), wb1[...])
    o_ref[...] = (_sigmoid(g0_ref[...]) * y0 + _sigmoid(g1_ref[...]) * y1).astype(o_ref.dtype)


def _merge(a0, a1, w_branch, l, proj, out_dtype):
    m, k = a0.shape
    n = w_branch.shape[3]
    tm = _pick(m, (512, 256, 128, 8))
    tn = _pick(n, (1024, 512, 256, 128))
    g0_blk = COL_GATE // tn
    g1_blk = (COL_GATE + n) // tn
    wspec = lambda br: pl.BlockSpec((None, None, k, tn), lambda j, i: (l, br, 0, j))
    return pl.pallas_call(
        _merge_body,
        grid=(n // tn, m // tm),
        in_specs=[
            pl.BlockSpec((tm, k), lambda j, i: (i, 0)),
            pl.BlockSpec((tm, k), lambda j, i: (i, 0)),
            wspec(0), wspec(1),
            pl.BlockSpec((tm, tn), lambda j, i: (i, g0_blk + j)),
            pl.BlockSpec((tm, tn), lambda j, i: (i, g1_blk + j)),
        ],
        out_specs=pl.BlockSpec((tm, tn), lambda j, i: (i, j)),
        out_shape=jax.ShapeDtypeStruct((m, n), out_dtype),
        scratch_shapes=[pltpu.VMEM((k, tn), BF16), pltpu.VMEM((k, tn), BF16)],
        compiler_params=_params(),
        name="merge",
    )(a0, a1, w_branch, w_branch, proj, proj)


def _out_norm_body(a_ref, w_ref, x_ref, g_ref, x1_ref, hn_ref, wb):
    _cast_weight_once(pl.program_id(0), w_ref, wb)
    x1 = x_ref[...] + _dot(a_ref[...].astype(BF16), wb[...])
    x1_ref[...] = x1
    ms = jnp.mean(x1 * x1, axis=-1, keepdims=True)
    hn_ref[...] = ((x1 * lax.rsqrt(ms + EPS)) * g_ref[...]).astype(hn_ref.dtype)


def _out_norm(a, w, l, x, g, hn_dtype):
    m, k = a.shape
    n = w.shape[2]
    tm = _pick(m, (256, 128, 8))
    return pl.pallas_call(
        _out_norm_body,
        grid=(m // tm,),
        in_specs=[
            pl.BlockSpec((tm, k), lambda i: (i, 0)),
            pl.BlockSpec((None, k, n), lambda i: (l, 0, 0), pipeline_mode=pl.Buffered(1)),
            pl.BlockSpec((tm, n), lambda i: (i, 0)),
            pl.BlockSpec((1, n), lambda i: (0, 0)),
        ],
        out_specs=[pl.BlockSpec((tm, n), lambda i: (i, 0)), pl.BlockSpec((tm, n), lambda i: (i, 0))],
        out_shape=[jax.ShapeDtypeStruct((m, n), F32), jax.ShapeDtypeStruct((m, n), hn_dtype)],
        scratch_shapes=[pltpu.VMEM((k, n), BF16)],
        compiler_params=_params(),
        name="out_proj_norm",
    )(a, w, x, g.reshape(1, n))


def _ffn_up_body(a_ref, wg_ref, wu_ref, o_ref, wgb, wub):
    _cast_weight_once(pl.program_id(1), wg_ref, wgb)
    _cast_weight_once(pl.program_id(1), wu_ref, wub)
    a = a_ref[...].astype(BF16)
    gate = _dot(a, wgb[...])
    up = _dot(a, wub[...])
    o_ref[...] = ((gate * _sigmoid(gate)) * up).astype(o_ref.dtype)


def _ffn_up(a, w_up, l, out_dtype):
    m, k = a.shape
    d_ff = w_up.shape[2] // 2
    tm = _pick(m, (2048, 1024, 512, 256, 128, 8))
    tn = _pick(d_ff, (512, 256, 128))
    up_blk = d_ff // tn
    return pl.pallas_call(
        _ffn_up_body,
        grid=(d_ff // tn, m // tm),
        in_specs=[
            pl.BlockSpec((tm, k), lambda j, i: (i, 0)),
            pl.BlockSpec((None, k, tn), lambda j, i: (l, 0, j)),
            pl.BlockSpec((None, k, tn), lambda j, i: (l, 0, up_blk + j)),
        ],
        out_specs=pl.BlockSpec((tm, tn), lambda j, i: (i, j)),
        out_shape=jax.ShapeDtypeStruct((m, d_ff), out_dtype),
        scratch_shapes=[pltpu.VMEM((k, tn), BF16), pltpu.VMEM((k, tn), BF16)],
        compiler_params=_params(),
        name="ffn_up",
    )(a, w_up, w_up)


def _ffn_down_body(a_ref, w_ref, x_ref, o_ref):
    o_ref[...] = x_ref[...] + _dot(a_ref[...].astype(BF16), w_ref[...])


def _ffn_down(a, w, l, x):
    m, k = a.shape
    n = w.shape[2]
    tm = _pick(m, (256, 128, 8))
    tn = _pick(n, (1024, 512, 256, 128))
    return pl.pallas_call(
        _ffn_down_body,
        grid=(n // tn, m // tm),
        in_specs=[
            pl.BlockSpec((tm, k), lambda j, i: (i, 0)),
            pl.BlockSpec((None, k, tn), lambda j, i: (l, 0, j)),
            pl.BlockSpec((tm, tn), lambda j, i: (i, j)),
        ],
        out_specs=pl.BlockSpec((tm, tn), lambda j, i: (i, j)),
        out_shape=jax.ShapeDtypeStruct((m, n), F32),
        compiler_params=_params(),
        name="ffn_down",
    )(a, w, x)


def _rope_tables(pos):
    def tables(dim):
        half = dim // 2
        freqs = ROPE_THETA ** (-jnp.arange(half, dtype=F32) / half)
        ang = pos.astype(F32)[:, None] * freqs[None, :]
        cos, sin = jnp.cos(ang), jnp.sin(ang)
        reps = LANES // dim
        cos_t = jnp.tile(jnp.concatenate([cos, cos], axis=1), (1, reps))
        sin_t = jnp.tile(jnp.concatenate([-sin, sin], axis=1), (1, reps))
        return cos_t, sin_t
    return tables(HEAD_DIM) + tables(IDX_DIM)


def _rope_head(x, cos_t, sin_t):
    return x * cos_t + pltpu.roll(x, HEAD_DIM // 2, axis=1) * sin_t


def _rope_idx(x, cos_t, sin_t, first_half):
    half = IDX_DIM // 2
    partner = jnp.where(first_half, pltpu.roll(x, LANES - half, axis=1), pltpu.roll(x, half, axis=1))
    return x * cos_t + partner * sin_t


def _head_norm(x, g):
    ms = jnp.mean(x * x, axis=-1, keepdims=True)
    return (x * lax.rsqrt(ms + EPS)) * g


def _qk_post_body(q_ref, k_ref, v_ref, qi_ref, kiw_ref, ch_ref, sh_ref, ci_ref, si_ref, qn_ref, kn_ref,
                  qo_ref, ko_ref, kb_ref, vo_ref, vb_ref, qio_ref, kio_ref, kilo_ref, kihi_ref):
    ch, sh, ci, si = ch_ref[...], sh_ref[...], ci_ref[...], si_ref[...]
    tm = ch.shape[0]
    lane = lax.broadcasted_iota(jnp.int32, ch.shape, 1)
    first_half = (lane & (IDX_DIM - 1)) < (IDX_DIM // 2)
    for h in range(N_HEADS):
        sl = slice(h * HEAD_DIM, (h + 1) * HEAD_DIM)
        qo_ref[:, sl] = _rope_head(_head_norm(q_ref[:, sl], qn_ref[...]), ch, sh).astype(qo_ref.dtype)
    for h in range(N_KV_HEADS):
        sl = slice(h * HEAD_DIM, (h + 1) * HEAD_DIM)
        kr = _rope_head(_head_norm(k_ref[:, sl], kn_ref[...]), ch, sh)
        ko_ref[pl.ds(h, tm, stride=N_KV_HEADS), :] = kr
        vo_ref[pl.ds(h, tm, stride=N_KV_HEADS), :] = v_ref[:, sl]
        kb_ref[:, sl] = kr.astype(kb_ref.dtype)
    vb_ref[...] = v_ref[...].astype(vb_ref.dtype)
    for h in range(IDX_W // LANES):
        sl = slice(h * LANES, (h + 1) * LANES)
        qio_ref[:, sl] = _rope_idx(qi_ref[:, sl], ci, si, first_half).astype(qio_ref.dtype)
    kir = _rope_idx(kiw_ref[...], ci, si, first_half)
    kio_ref[...] = kir[:, :IDX_DIM]
    zero = jnp.zeros_like(kir)
    kilo_ref[...] = jnp.where(lane < IDX_DIM, kir, zero).astype(kilo_ref.dtype)
    kihi_ref[...] = jnp.where(lane < IDX_DIM, zero, pltpu.roll(kir, IDX_DIM, axis=1)).astype(kihi_ref.dtype)


def _qk_post(proj, ki_col, tabs, tab_rows, q_norm, k_norm, act_dtype):
    m = proj.shape[0]
    tm = _pick(min(m, tab_rows), (512, 256, 128, 8))
    n_tab = tab_rows // tm
    col = lambda off, w: (lambda i: (i, off // w))
    tab_spec = pl.BlockSpec((tm, LANES), lambda i: (i % n_tab, 0))
    row = lambda w: pl.BlockSpec((tm, w), lambda i: (i, 0))
    head_rows = pl.BlockSpec((tm * N_KV_HEADS, HEAD_DIM), lambda i: (i, 0))
    ki_blk = ki_col // LANES
    outs = pl.pallas_call(
        _qk_post_body,
        grid=(m // tm,),
        in_specs=[
            pl.BlockSpec((tm, ATT_W), col(COL_Q, ATT_W)),
            pl.BlockSpec((tm, KV_W), col(COL_K, KV_W)),
            pl.BlockSpec((tm, KV_W), col(COL_V, KV_W)),
            pl.BlockSpec((tm, IDX_W), col(COL_QI, IDX_W)),
            pl.BlockSpec((tm, LANES), lambda i: (i, ki_blk)),
            tab_spec, tab_spec, tab_spec, tab_spec,
            pl.BlockSpec((1, HEAD_DIM), lambda i: (0, 0)),
            pl.BlockSpec((1, HEAD_DIM), lambda i: (0, 0)),
        ],
        out_specs=[row(ATT_W), head_rows, row(KV_W), head_rows, row(KV_W), row(IDX_W), row(IDX_DIM),
                   row(LANES), row(LANES)],
        out_shape=[
            jax.ShapeDtypeStruct((m, ATT_W), act_dtype),
            jax.ShapeDtypeStruct((m * N_KV_HEADS, HEAD_DIM), F32),
            jax.ShapeDtypeStruct((m, KV_W), act_dtype),
            jax.ShapeDtypeStruct((m * N_KV_HEADS, HEAD_DIM), F32),
            jax.ShapeDtypeStruct((m, KV_W), act_dtype),
            jax.ShapeDtypeStruct((m, IDX_W), act_dtype),
            jax.ShapeDtypeStruct((m, IDX_DIM), F32),
            jax.ShapeDtypeStruct((m, LANES), act_dtype),
            jax.ShapeDtypeStruct((m, LANES), act_dtype),
        ],
        compiler_params=_params(),
        name="qk_post",
    )(proj, proj, proj, proj, proj, *tabs, q_norm.reshape(1, HEAD_DIM), k_norm.reshape(1, HEAD_DIM))
    return outs


def _strict_upper(n):
    r = lax.broadcasted_iota(jnp.int32, (n, n), 0)
    c = lax.broadcasted_iota(jnp.int32, (n, n), 1)
    return jnp.where(r < c, 1.0, 0.0).astype(BF16)


def _strict_lower(n):
    r = lax.broadcasted_iota(jnp.int32, (n, n), 0)
    c = lax.broadcasted_iota(jnp.int32, (n, n), 1)
    return jnp.where(c < r, 1.0, 0.0).astype(BF16)


def _attn_prompt_body(q_ref, qi_ref, kiw_ref, k_ref, v_ref, kilo_ref, kihi_ref, o_ref,
                      key_scr, bias_scr, lg_scr, m_scr, l_scr, acc_scr, *, topk):
    i = pl.program_id(1)
    tq = q_ref.shape[0]
    wc = key_scr.shape[1]
    n_ch = i + 1
    wi_t = jnp.transpose(kiw_ref[...])
    q_pos = i * tq + lax.broadcasted_iota(jnp.int32, (wc, tq), 1)
    k_off = lax.broadcasted_iota(jnp.int32, (wc, tq), 0)

    def key_count(mask):
        return jnp.sum(jnp.where(mask, 1.0, 0.0), axis=0, keepdims=True)

    def score_chunk(kc, carry):
        r0 = pl.multiple_of(kc * wc, wc)
        ki_lo = kilo_ref[pl.ds(r0, wc), :]
        ki_hi = kihi_ref[pl.ds(r0, wc), :]
        score = jnp.zeros((wc, tq), F32)
        for pair in range(N_IDX_HEADS // 2):
            x = qi_ref[:, pair * LANES:(pair + 1) * LANES]
            w_lo = wi_t[IDX_DIM + 2 * pair:IDX_DIM + 2 * pair + 1, :]
            w_hi = wi_t[IDX_DIM + 2 * pair + 1:IDX_DIM + 2 * pair + 2, :]
            score = score + w_lo * jnp.maximum(_nt_dot(ki_lo, x), 0.0)
            score = score + w_hi * jnp.maximum(_nt_dot(ki_hi, x), 0.0)
        key_scr[kc] = jnp.where(kc * wc + k_off <= q_pos, _sort_key(score * IDX_SCALE), INT_MIN)
        return carry

    lax.fori_loop(0, n_ch, score_chunk, 0)

    def search(it, res):
        trial = res + jnp.left_shift(jnp.int32(1), 31 - it)

        def count_chunk(kc, acc):
            hit = jnp.where(key_scr[kc] >= trial, 1.0, 0.0)
            for j in range(wc // SUBLANES):
                acc = acc + hit[j * SUBLANES:(j + 1) * SUBLANES, :]
            return acc

        acc = lax.fori_loop(0, n_ch, count_chunk, jnp.zeros((SUBLANES, tq), F32))
        return jnp.where(jnp.sum(acc, axis=0, keepdims=True) >= float(topk), trial, res)

    thr = lax.fori_loop(0, 32, search, jnp.full((1, tq), INT_MIN, jnp.int32))

    def bias_chunk(kc, n_keep):
        key = key_scr[kc]
        keep = (key >= thr) & (key > KEY_NEG_INF)
        bias_scr[kc] = jnp.transpose(jnp.where(keep, 0.0, MASKED))
        return n_keep + key_count(keep)

    n_keep = lax.fori_loop(0, n_ch, bias_chunk, jnp.zeros((1, tq), F32))

    @pl.when(jnp.max(n_keep) > float(topk))
    def _():
        n_gt = lax.fori_loop(0, n_ch, lambda kc, n: n + key_count(key_scr[kc] > thr), jnp.zeros((1, tq), F32))
        need = float(topk) - n_gt
        lower = _strict_lower(wc)

        def tie_chunk(kc, before):
            key = key_scr[kc]
            eq = key == thr
            eqf = jnp.where(eq, 1.0, 0.0)
            rank = before + _dot(lower, eqf.astype(BF16))
            keep = ((key > thr) | (eq & (rank < need))) & (key > KEY_NEG_INF)
            bias_scr[kc] = jnp.transpose(jnp.where(keep, 0.0, MASKED))
            return before + jnp.sum(eqf, axis=0, keepdims=True)

        lax.fori_loop(0, n_ch, tie_chunk, jnp.zeros((1, tq), F32))

    m_scr[...] = jnp.full(m_scr.shape, MASKED, F32)
    l_scr[...] = jnp.zeros(l_scr.shape, F32)
    acc_scr[...] = jnp.zeros(acc_scr.shape, F32)
    lane_blocks = [slice(j * LANES, (j + 1) * LANES) for j in range(wc // LANES)]

    def logit_chunk(kc, carry):
        r0 = pl.multiple_of(kc * wc, wc)
        bias = bias_scr[kc]
        for h in range(N_HEADS):
            g = h // GQA_GROUP
            kg = k_ref[pl.ds(r0, wc), g * HEAD_DIM:(g + 1) * HEAD_DIM]
            lg = _nt_dot(q_ref[:, h * HEAD_DIM:(h + 1) * HEAD_DIM], kg) * ATT_SCALE + bias
            lg_scr[h, kc] = lg
            mx = m_scr[h]
            for lb in lane_blocks:
                mx = jnp.maximum(mx, lg[:, lb])
            m_scr[h] = mx
        return carry

    lax.fori_loop(0, n_ch, logit_chunk, 0)
    for h in range(N_HEADS):
        m_scr[h] = jnp.broadcast_to(jnp.max(m_scr[h], axis=1, keepdims=True), (tq, LANES))

    def prob_chunk(kc, carry):
        r0 = pl.multiple_of(kc * wc, wc)
        for h in range(N_HEADS):
            g = h // GQA_GROUP
            sl = slice(h * HEAD_DIM, (h + 1) * HEAD_DIM)
            vg = v_ref[pl.ds(r0, wc), g * HEAD_DIM:(g + 1) * HEAD_DIM]
            lg = lg_scr[h, kc]
            mx = m_scr[h]
            ps = [jnp.exp(lg[:, lb] - mx) for lb in lane_blocks]
            den = l_scr[h]
            for p in ps:
                den = den + p
            l_scr[h] = den
            acc_scr[:, sl] = acc_scr[:, sl] + _dot(jnp.concatenate(ps, axis=1).astype(BF16), vg)
        return carry

    lax.fori_loop(0, n_ch, prob_chunk, 0)
    for h in range(N_HEADS):
        sl = slice(h * HEAD_DIM, (h + 1) * HEAD_DIM)
        den = jnp.sum(l_scr[h], axis=1, keepdims=True)
        o_ref[:, sl] = (acc_scr[:, sl] / den).astype(o_ref.dtype)


def _attn_prompt(q, qi, proj, ki_col, k, v, ki_lo, ki_hi, nb, s):
    tq = _pick(s, (256, 128))
    nq = s // tq
    topk = min(TOPK_MAX, s // 4)
    ki_blk = ki_col // LANES
    qrow = lambda w: pl.BlockSpec((tq, w), lambda b, i: (b * nq + i, 0))
    full = lambda w: pl.BlockSpec((s, w), lambda b, i: (b, 0))
    return pl.pallas_call(
        functools.partial(_attn_prompt_body, topk=topk),
        grid=(nb, nq),
        in_specs=[qrow(ATT_W), qrow(IDX_W), pl.BlockSpec((tq, LANES), lambda b, i: (b * nq + i, ki_blk)),
                  full(KV_W), full(KV_W), full(LANES), full(LANES)],
        out_specs=qrow(ATT_W),
        out_shape=jax.ShapeDtypeStruct((nb * s, ATT_W), BF16),
        scratch_shapes=[pltpu.VMEM((nq, tq, tq), jnp.int32), pltpu.VMEM((nq, tq, tq), F32),
                        pltpu.VMEM((N_HEADS, nq, tq, tq), F32),
                        pltpu.VMEM((N_HEADS, tq, LANES), F32), pltpu.VMEM((N_HEADS, tq, LANES), F32),
                        pltpu.VMEM((tq, ATT_W), F32)],
        compiler_params=_params(),
        name="attn_prompt",
    )(q, qi, proj, k, v, ki_lo, ki_hi)


def _s5_discretize_body(are_ref, aim_ref, ldt_ref, bre_ref, bim_ref, abre_ref, abim_ref, bbre_ref, bbim_ref):
    dt = jnp.exp(ldt_ref[...])
    lr, li = are_ref[...], aim_ref[...]
    mag = jnp.exp(lr * dt)
    ab_re, ab_im = mag * jnp.cos(li * dt), mag * jnp.sin(li * dt)
    zr, zi = ab_re - 1.0, ab_im
    den = lr * lr + li * li
    fr = (zr * lr + zi * li) / den
    fi = (zi * lr - zr * li) / den
    abre_ref[...] = ab_re
    abim_ref[...] = ab_im
    br, bi = bre_ref[...], bim_ref[...]
    bbre_ref[...] = fr * br - fi * bi
    bbim_ref[...] = fr * bi + fi * br


def _s5_discretize(a_re, a_im, log_dt, b_re, b_im):
    depth = a_re.shape[0]
    gp = pl.BlockSpec((None, SSM_GROUPS, 1, SSM_P), lambda l: (l, 0, 0, 0))
    gwp = pl.BlockSpec((None, SSM_GROUPS, SSM_GROUP, SSM_P), lambda l: (l, 0, 0, 0))
    bt = lambda b: jnp.swapaxes(b, 2, 3)
    a4 = lambda a: a.reshape(depth, SSM_GROUPS, 1, SSM_P)
    return pl.pallas_call(
        _s5_discretize_body,
        grid=(depth,),
        in_specs=[gp, gp, pl.BlockSpec((None, SSM_GROUPS, 1, 1), lambda l: (l, 0, 0, 0)), gwp, gwp],
        out_specs=[gp, gp, gwp, gwp],
        out_shape=[jax.ShapeDtypeStruct((depth, SSM_GROUPS, 1, SSM_P), F32)] * 2
        + [jax.ShapeDtypeStruct((depth, SSM_GROUPS, SSM_GROUP, SSM_P), F32)] * 2,
        compiler_params=_params(),
        name="s5_discretize",
    )(a4(a_re), a4(a_im), log_dt.reshape(depth, SSM_GROUPS, 1, 1), bt(b_re), bt(b_im))


def _block_diag_in(bb_t):
    depth = bb_t.shape[0]
    gpb = SSM_GROUPS // SSM_BLOCKS
    x = bb_t.reshape(depth, SSM_BLOCKS, gpb, SSM_GROUP, SSM_P)
    eye = jnp.eye(gpb, dtype=bb_t.dtype)
    out = jnp.einsum('lkgwp,gh->lkgwhp', x, eye)
    return out.reshape(depth, SSM_BLOCKS, gpb * SSM_GROUP, gpb * SSM_P).astype(BF16)


def _block_diag_out(c):
    depth = c.shape[0]
    gpb = SSM_GROUPS // SSM_BLOCKS
    x = c.reshape(depth, SSM_BLOCKS, gpb, SSM_GROUP, SSM_P)
    eye = jnp.eye(gpb, dtype=c.dtype)
    out = jnp.einsum('lkgwp,gh->lkgphw', x, eye)
    return out.reshape(depth, SSM_BLOCKS, gpb * SSM_P, gpb * SSM_GROUP).astype(BF16)


def _s5_body(u_ref, bbre_ref, bbim_ref, cre_ref, cim_ref, abre_ref, abim_ref, d_ref, h0re_ref, h0im_ref,
             g_ref, htre_ref, htim_ref, bure, buim, hre, him, ust, gst, *, nb):
    c = pl.program_id(0)
    rows = bure.shape[0]
    n_t = rows // nb
    per_tile = SUBLANES // nb

    @pl.when(c == 0)
    def _():
        hre[...] = h0re_ref[...]
        him[...] = h0im_ref[...]

    for k in range(SSM_BLOCKS):
        ln = slice(k * LANES, (k + 1) * LANES)
        for b in range(nb):
            if n_t == 1:
                ust[k, b:b + 1, :] = u_ref[b, :, ln]
            else:
                ust[k, pl.ds(b, n_t, stride=nb), :] = u_ref[b, :, ln]
        uk = ust[k].astype(BF16)
        st = slice(k * SSM_BLOCK_STATE, (k + 1) * SSM_BLOCK_STATE)
        bure[:, st] = _dot(uk, bbre_ref[k])
        buim[:, st] = _dot(uk, bbim_ref[k])

    for cc in range(SSM_STATE // SCAN_LANES):
        sl = slice(cc * SCAN_LANES, (cc + 1) * SCAN_LANES)
        ar = jnp.broadcast_to(abre_ref[:, sl], (nb, SCAN_LANES))
        ai = jnp.broadcast_to(abim_ref[:, sl], (nb, SCAN_LANES))

        def step(j, carry, sl=sl, ar=ar, ai=ai):
            hr, hi = carry
            r0 = pl.multiple_of(j * SUBLANES, SUBLANES)
            xr = bure[pl.ds(r0, SUBLANES), sl]
            xi = buim[pl.ds(r0, SUBLANES), sl]
            out_r, out_i = [], []
            for t in range(per_tile):
                nr = ar * hr - ai * hi + xr[t * nb:(t + 1) * nb]
                ni = ar * hi + ai * hr + xi[t * nb:(t + 1) * nb]
                hr, hi = nr, ni
                out_r.append(hr)
                out_i.append(hi)
            bure[pl.ds(r0, SUBLANES), sl] = out_r[0] if per_tile == 1 else jnp.concatenate(out_r, axis=0)
            buim[pl.ds(r0, SUBLANES), sl] = out_i[0] if per_tile == 1 else jnp.concatenate(out_i, axis=0)
            return hr, hi

        hr, hi = lax.fori_loop(0, rows // SUBLANES, step, (hre[:, sl], him[:, sl]))
        hre[:, sl] = hr
        him[:, sl] = hi

    for k in range(SSM_BLOCKS):
        st = slice(k * SSM_BLOCK_STATE, (k + 1) * SSM_BLOCK_STATE)
        ln = slice(k * LANES, (k + 1) * LANES)
        y = _dot(bure[:, st].astype(BF16), cre_ref[k]) - _dot(buim[:, st].astype(BF16), cim_ref[k])
        y = y + d_ref[:, ln] * ust[k]
        gst[k] = _gelu_tanh(y)
        for b in range(nb):
            if n_t == 1:
                g_ref[b, :, ln] = gst[k, b:b + 1, :]
            else:
                g_ref[b, :, ln] = gst[k, pl.ds(b, n_t, stride=nb), :]

    @pl.when(c == pl.num_programs(0) - 1)
    def _():
        htre_ref[...] = hre[...]
        htim_ref[...] = him[...]


def _s5(proj, nb, s, l, bb_re, bb_im, c_re, c_im, ab_re, ab_im, d, h0_re, h0_im):
    n_t = min(s, 512 // nb)
    rows = n_t * nb
    blk4 = lambda a: pl.BlockSpec((None,) + a.shape[1:], lambda c: (l, 0, 0, 0))
    full2 = lambda a: pl.BlockSpec(a.shape, lambda c: (0, 0))
    vec = lambda a: a.reshape(1, -1)
    ab_re, ab_im, d = vec(ab_re), vec(ab_im), vec(d)
    u_spec = pl.BlockSpec((nb, n_t, SSM_W), lambda c: (0, c, 0))
    g, h_re, h_im = pl.pallas_call(
        functools.partial(_s5_body, nb=nb),
        grid=(s // n_t,),
        in_specs=[u_spec, blk4(bb_re), blk4(bb_im), blk4(c_re), blk4(c_im),
                  full2(ab_re), full2(ab_im), full2(d), full2(h0_re), full2(h0_im)],
        out_specs=[u_spec, full2(h0_re), full2(h0_im)],
        out_shape=[jax.ShapeDtypeStruct((nb, s, SSM_W), F32),
                   jax.ShapeDtypeStruct(h0_re.shape, F32), jax.ShapeDtypeStruct(h0_im.shape, F32)],
        scratch_shapes=[pltpu.VMEM((rows, SSM_STATE), F32), pltpu.VMEM((rows, SSM_STATE), F32),
                        pltpu.VMEM((nb, SSM_STATE), F32), pltpu.VMEM((nb, SSM_STATE), F32),
                        pltpu.VMEM((SSM_BLOCKS, rows, LANES), F32), pltpu.VMEM((SSM_BLOCKS, rows, LANES), F32)],
        compiler_params=_params(),
        name="s5_scan",
    )(proj.reshape(nb, s, proj.shape[1]), bb_re, bb_im, c_re, c_im, ab_re, ab_im, d, h0_re, h0_im)
    return g.reshape(nb * s, SSM_W), h_re, h_im


IDX_PAGES_PER_STEP = 16


def _sample_index_body(pt_ref, qi_ref, wi_ref, kin_ref, *rest, n_pg, topk):
    del pt_ref
    pages = rest[:n_pg]
    idx_ref, nsel_ref, bnew_ref, sc_scr = rest[n_pg:]
    c = pl.program_id(1)
    n_pages = sc_scr.shape[0]
    qi = qi_ref[...].astype(BF16)
    wi = wi_ref[...]

    def weighted(d):
        return jnp.sum(wi * jnp.maximum(d, 0.0), axis=0, keepdims=True) * IDX_SCALE

    for p in range(n_pg):
        sc_scr[pl.ds(c * n_pg + p, 1), :] = weighted(_dot(qi, pages[p][...].astype(BF16)))

    @pl.when(c == pl.num_programs(1) - 1)
    def _():
        kin = kin_ref[...].astype(BF16).astype(F32)
        d_new = jnp.sum(qi.astype(F32) * kin, axis=1, keepdims=True)
        key_new = _sort_key(jnp.broadcast_to(weighted(d_new), (1, LANES)))
        key = _sort_key(sc_scr[...])

        def count(mask, mask_new):
            per_lane = jnp.sum(jnp.where(mask, 1.0, 0.0), axis=0, keepdims=True)
            return jnp.sum(per_lane, axis=1, keepdims=True) + jnp.where(mask_new, 1.0, 0.0)

        def body(it, res):
            trial = res + jnp.left_shift(jnp.int32(1), 31 - it)
            return jnp.where(count(key >= trial, key_new >= trial) >= float(topk), trial, res)

        thr = lax.fori_loop(0, 32, body, jnp.full((1, LANES), INT_MIN, jnp.int32))
        need = float(topk) - count(key > thr, key_new > thr)
        eq = key == thr
        eqf = jnp.where(eq, 1.0, 0.0).astype(BF16)
        in_page = _dot(eqf, _strict_upper(LANES))
        lower = _strict_lower(n_pages)
        per_page = jnp.sum(jnp.where(eq, 1.0, 0.0), axis=1, keepdims=True)
        earlier = _dot(lower, jnp.broadcast_to(per_page, (n_pages, LANES)).astype(BF16))
        keep = ((key > thr) | (eq & (in_page + earlier < need))) & (key > KEY_NEG_INF)
        n_eq_past = jnp.sum(jnp.sum(jnp.where(eq, 1.0, 0.0), axis=0, keepdims=True), axis=1, keepdims=True)
        keep_new = ((key_new > thr) | ((key_new == thr) & (n_eq_past < need))) & (key_new > KEY_NEG_INF)
        bnew_ref[...] = jnp.where(keep_new, 0.0, MASKED)

        keepf = jnp.where(keep, 1.0, 0.0).astype(BF16)
        in_page_rank = jnp.where(keep, _dot(keepf, _strict_upper(LANES)), -1.0).astype(BF16)
        per_page = _nt_dot(jnp.ones((SUBLANES, LANES), BF16), keepf)[:1]
        pr = lax.broadcasted_iota(jnp.int32, (n_pages, n_pages), 0)
        pc = lax.broadcasted_iota(jnp.int32, (n_pages, n_pages), 1)
        upto = jnp.where(pr <= pc, 1.0, 0.0).astype(BF16)
        cum = _dot(jnp.broadcast_to(per_page, (SUBLANES, n_pages)).astype(BF16), upto)[:1]
        slot = lax.broadcasted_iota(jnp.int32, (topk, 1), 0).astype(F32)
        passed = cum <= slot
        page_of = jnp.sum(jnp.where(passed, 1.0, 0.0), axis=1, keepdims=True)
        start_of = jnp.sum(jnp.where(passed, per_page, 0.0), axis=1, keepdims=True)
        page_lane = lax.broadcasted_iota(jnp.int32, (topk, n_pages), 1).astype(F32)
        ranks = _dot(jnp.where(page_lane == page_of, 1.0, 0.0).astype(BF16), in_page_rank)
        lane = lax.broadcasted_iota(jnp.int32, (topk, LANES), 1).astype(F32)
        lane_of = jnp.sum(jnp.where(ranks == slot - start_of, lane, 0.0), axis=1, keepdims=True)
        n_kept = jnp.sum(per_page, axis=1, keepdims=True)
        idx_ref[...] = jnp.where(slot < n_kept, page_of * PAGE_SIZE + lane_of, 0.0).astype(jnp.int32)
        nsel_ref[...] = jnp.broadcast_to(n_kept, (1, LANES)).astype(jnp.int32)


def _sample_index(page_table, cache_kidx_t, layer, qi, wi, ki_new):
    nb, n_pages = page_table.shape
    n_pg = _pick(n_pages, (IDX_PAGES_PER_STEP, 8, 4, 2, 1))
    topk = min(TOPK_MAX, (n_pages * PAGE_SIZE + 1) // 4)
    page_spec = lambda p: pl.BlockSpec((None, None, IDX_DIM, PAGE_SIZE),
                                       lambda b, c, pt: (layer, pt[b, c * n_pg + p], 0, 0))
    per_b = lambda shape: pl.BlockSpec((None,) + shape, lambda b, c, pt: (b, 0, 0))
    grid_spec = pltpu.PrefetchScalarGridSpec(
        num_scalar_prefetch=1,
        grid=(nb, n_pages // n_pg),
        in_specs=[per_b((N_IDX_HEADS, IDX_DIM)), per_b((N_IDX_HEADS, 1)), per_b((1, IDX_DIM))]
        + [page_spec(p) for p in range(n_pg)],
        out_specs=[per_b((topk, 1)), per_b((1, LANES)), per_b((1, LANES))],
        scratch_shapes=[pltpu.VMEM((n_pages, LANES), F32)],
    )
    idx, n_sel, bias_new = pl.pallas_call(
        functools.partial(_sample_index_body, n_pg=n_pg, topk=topk),
        grid_spec=grid_spec,
        out_shape=[jax.ShapeDtypeStruct((nb, topk, 1), jnp.int32), jax.ShapeDtypeStruct((nb, 1, LANES), jnp.int32),
                   jax.ShapeDtypeStruct((nb, 1, LANES), F32)],
        compiler_params=_params(),
        name="sample_index",
    )(page_table, qi.reshape(nb, N_IDX_HEADS, IDX_DIM), wi.reshape(nb, N_IDX_HEADS, 1),
      ki_new.reshape(nb, 1, IDX_DIM), *([cache_kidx_t] * n_pg))
    return idx.reshape(nb, topk), n_sel[:, 0, 0], bias_new


SLOTS_PER_TILE = SUBLANES // N_KV_HEADS


def _group_rows(n_cols):
    head = lax.broadcasted_iota(jnp.int32, (N_HEADS, n_cols), 0)
    return [(head >= g * GQA_GROUP) & (head < (g + 1) * GQA_GROUP) for g in range(N_KV_HEADS)]


def _sample_attn_body(pt_ref, idx_ref, nsel_ref, q_ref, kn_ref, vn_ref, bnew_ref, odd_ref, ck_hbm, cv_hbm,
                      o_ref, kbuf, vbuf, sem, *, layer, topk):
    b = pl.program_id(0)

    def tile_copy(src_hbm, buf, which, phys, src_row, dst_row):
        return pltpu.make_async_copy(src_hbm.at[layer, phys, pl.ds(src_row, SUBLANES), :],
                                     buf.at[pl.ds(dst_row, SUBLANES), :], sem.at[which])

    def issue(r, carry):
        idx = idx_ref[b, r]
        phys = pt_ref[b, idx // PAGE_SIZE]
        src_row = pl.multiple_of(((idx % PAGE_SIZE) // SLOTS_PER_TILE) * SUBLANES, SUBLANES)
        dst_row = pl.multiple_of(r * SUBLANES, SUBLANES)
        tile_copy(ck_hbm, kbuf, 0, phys, src_row, dst_row).start()
        tile_copy(cv_hbm, vbuf, 1, phys, src_row, dst_row).start()
        return carry

    lax.fori_loop(0, topk, issue, 0)

    def drain(r, carry):
        tile_copy(ck_hbm, kbuf, 0, 0, 0, 0).wait()
        tile_copy(cv_hbm, vbuf, 1, 0, 0, 0).wait()
        return carry

    lax.fori_loop(0, topk, drain, 0)

    q = q_ref[...].astype(BF16)
    odd = odd_ref[...] == 1
    live = lax.broadcasted_iota(jnp.int32, (1, topk), 1) < nsel_ref[b]

    def gathered(buf, g):
        first = buf[pl.ds(g, topk, stride=SUBLANES), :]
        second = buf[pl.ds(N_KV_HEADS + g, topk, stride=SUBLANES), :]
        return jnp.where(odd, second, first).astype(BF16)

    in_group = _group_rows(topk)
    lg = jnp.zeros((N_HEADS, topk), F32)
    for g in range(N_KV_HEADS):
        lg = jnp.where(in_group[g], _nt_dot(q, gathered(kbuf, g)), lg)
    lg = jnp.where(live, lg * ATT_SCALE, MASKED)
    qf = q.astype(F32)
    kn = kn_ref[...].astype(BF16).astype(F32)
    vn = vn_ref[...].astype(BF16).astype(F32)
    lg_new = jnp.sum(qf * kn, axis=1, keepdims=True) * ATT_SCALE + bnew_ref[:, :1]
    mx = jnp.maximum(jnp.max(lg, axis=1, keepdims=True), lg_new)
    pr = jnp.where(live, jnp.exp(lg - mx), 0.0)
    pr_new = jnp.where(bnew_ref[:, :1] == 0.0, jnp.exp(lg_new - mx), 0.0)
    den = jnp.sum(pr, axis=1, keepdims=True) + pr_new
    prb = pr.astype(BF16)
    in_group_d = _group_rows(HEAD_DIM)
    pv = jnp.zeros((N_HEADS, HEAD_DIM), F32)
    for g in range(N_KV_HEADS):
        pv = jnp.where(in_group_d[g], _dot(prb, gathered(vbuf, g)), pv)
    o_ref[...] = ((pv + pr_new.astype(BF16).astype(F32) * vn) / den).astype(o_ref.dtype)


def _sample_attn(page_table, cache_k2, cache_v2, layer, idx, n_sel, bias_new, q, k_new, v_new):
    nb = page_table.shape[0]
    topk = idx.shape[1]
    per_b = lambda shape: pl.BlockSpec((None,) + shape, lambda b, pt, ix, ns: (b, 0, 0))
    per_head = lambda a: jnp.repeat(a.reshape(nb, N_KV_HEADS, HEAD_DIM), GQA_GROUP, axis=1)
    odd = (idx % SLOTS_PER_TILE).reshape(nb, topk, 1)
    grid_spec = pltpu.PrefetchScalarGridSpec(
        num_scalar_prefetch=3,
        grid=(nb,),
        in_specs=[per_b((N_HEADS, HEAD_DIM)), per_b((N_HEADS, HEAD_DIM)), per_b((N_HEADS, HEAD_DIM)),
                  per_b((1, LANES)), per_b((topk, 1)),
                  pl.BlockSpec(memory_space=pl.ANY), pl.BlockSpec(memory_space=pl.ANY)],
        out_specs=per_b((N_HEADS, HEAD_DIM)),
        scratch_shapes=[pltpu.VMEM((topk * SUBLANES, HEAD_DIM), F32), pltpu.VMEM((topk * SUBLANES, HEAD_DIM), F32),
                        pltpu.SemaphoreType.DMA((2,))],
    )
    out = pl.pallas_call(
        functools.partial(_sample_attn_body, layer=layer, topk=topk),
        grid_spec=grid_spec,
        out_shape=jax.ShapeDtypeStruct((nb, N_HEADS, HEAD_DIM), F32),
        compiler_params=_params(),
        name="sample_attn",
    )(page_table, idx, n_sel, q.reshape(nb, N_HEADS, HEAD_DIM), per_head(k_new), per_head(v_new), bias_new, odd,
      cache_k2, cache_v2)
    return out.reshape(nb, ATT_W)


def _layer_tail(proj, g, y_att, x, w, l, act_dtype):
    z = _glu(g, w["w_glu"], l, act_dtype)
    merged = _merge(z, y_att, w["w_branch"], l, proj, act_dtype)
    x1, hn = _out_norm(merged, w["w_out"], l, x, w["norm_ffn"][l], act_dtype)
    act = _ffn_up(hn, w["w_ffn_up"], l, act_dtype)
    return _ffn_down(act, w["w_ffn_down"], l, x1)


def _s5_layer(proj, nb, s, w, l, h0_re, h0_im):
    return _s5(proj, nb, s, l, w["bb_re"], w["bb_im"], w["c_re"], w["c_im"], w["ab_re"][l], w["ab_im"][l],
               w["ssm_d"][l], h0_re, h0_im)


def _prompt_layer(x, nb, s, tabs, w, l):
    xn = _rmsnorm(x, w["norm_mix"][l], BF16)
    proj = _in_proj(xn, w["w_in"], l, PROJ_TN, F32)
    q, k_out, k_bf, v_out, v_bf, qi, ki_out, ki_lo, ki_hi = _qk_post(
        proj, w["ki_col"], tabs, s, w["q_norm"][l], w["k_norm"][l], BF16)
    y_att = _attn_prompt(q, qi, proj, w["ki_col"], k_bf, v_bf, ki_lo, ki_hi, nb, s)
    zeros = jnp.zeros((nb, SSM_STATE), F32)
    g, h_re, h_im = _s5_layer(proj, nb, s, w, l, zeros, zeros)
    x2 = _layer_tail(proj, g, y_att, x, w, l, BF16)
    return x2, k_out, v_out, ki_out, h_re, h_im


def _sample_layer(x, nb, tabs, w, l, page_table, cache_k2, cache_v2, cache_kidx_t, h0_re, h0_im):
    xn = _rmsnorm(x, w["norm_mix"][l], F32)
    proj = _in_proj(xn, w["w_in"], l, PROJ_TN, F32)
    ki_col = w["ki_col"]
    q, k_out, _, v_out, _, qi, ki_out, _, _ = _qk_post(proj, ki_col, tabs, nb, w["q_norm"][l], w["k_norm"][l], F32)
    wi = proj[:, ki_col + IDX_DIM:ki_col + IDX_DIM + N_IDX_HEADS]
    idx, n_sel, bias_new = _sample_index(page_table, cache_kidx_t, l, qi, wi, ki_out)
    y_att = _sample_attn(page_table, cache_k2, cache_v2, l, idx, n_sel, bias_new, q, k_out, v_out)
    g, h_re, h_im = _s5_layer(proj, nb, 1, w, l, h0_re, h0_im)
    x2 = _layer_tail(proj, g, y_att, x, w, l, F32)
    return x2, k_out, v_out, ki_out, h_re, h_im


def kernel(x_prompt, x_sample, cache_k, cache_v, cache_kidx, state_ssm_re, state_ssm_im, page_table, norm_mix, w_in, q_norm, k_norm, ssm_a_re, ssm_a_im, ssm_log_dt, ssm_b_re, ssm_b_im, ssm_c_re, ssm_c_im, ssm_d, w_glu, w_branch, w_out, norm_ffn, w_ffn_up, w_ffn_down):
    nb_p, s, d_model = x_prompt.shape
    nb_s, dec_seq, _ = x_sample.shape
    assert dec_seq == 1, "the sample group decodes one token per sequence"
    depth = w_in.shape[0]
    past = page_table.shape[1] * PAGE_SIZE

    ki_orig = COL_GATE
    gate_orig = ki_orig + IDX_DIM + N_IDX_HEADS
    n_gate = w_in.shape[2] - gate_orig
    ki_col = COL_GATE + n_gate
    assert ki_col % LANES == 0
    cols = ki_col + LANES
    cols_pad = -(-cols // PROJ_TN) * PROJ_TN
    w_in_t = jnp.swapaxes(w_in, 1, 2)
    w_in_r = jnp.concatenate(
        [w_in_t[:, :COL_GATE], w_in_t[:, gate_orig:], w_in_t[:, ki_orig:gate_orig],
         jnp.zeros((depth, cols_pad - ki_col - IDX_DIM - N_IDX_HEADS, d_model), w_in.dtype)], axis=1).astype(BF16)

    ab_re, ab_im, bbt_re, bbt_im = _s5_discretize(ssm_a_re, ssm_a_im, ssm_log_dt, ssm_b_re, ssm_b_im)
    bb_re, bb_im = _block_diag_in(bbt_re), _block_diag_in(bbt_im)
    c_re, c_im = _block_diag_out(ssm_c_re), _block_diag_out(ssm_c_im)

    tabs_p = _rope_tables(jnp.arange(s, dtype=jnp.int32))
    tabs_s = _rope_tables(jnp.full((nb_s,), past, dtype=jnp.int32))

    xp = x_prompt.reshape(nb_p * s, d_model)
    xs = x_sample.reshape(nb_s, d_model)
    cache_k2 = cache_k.reshape(cache_k.shape[:2] + (PAGE_SIZE * N_KV_HEADS, HEAD_DIM))
    cache_v2 = cache_v.reshape(cache_v.shape[:2] + (PAGE_SIZE * N_KV_HEADS, HEAD_DIM))
    cache_kidx_t = jnp.swapaxes(cache_kidx, 2, 3)
    w = dict(ki_col=ki_col, norm_mix=norm_mix, w_in=w_in_r, q_norm=q_norm, k_norm=k_norm,
             bb_re=bb_re, bb_im=bb_im, c_re=c_re, c_im=c_im, ab_re=ab_re, ab_im=ab_im, ssm_d=ssm_d,
             w_glu=w_glu, w_branch=w_branch, w_out=w_out, norm_ffn=norm_ffn,
             w_ffn_up=w_ffn_up, w_ffn_down=w_ffn_down.astype(BF16))
    outs_p, outs_s = [], []
    for l in range(depth):
        xp, *rest_p = _prompt_layer(xp, nb_p, s, tabs_p, w, l)
        xs, *rest_s = _sample_layer(xs, nb_s, tabs_s, w, l, page_table, cache_k2, cache_v2, cache_kidx_t,
                                    state_ssm_re[l].reshape(nb_s, SSM_STATE), state_ssm_im[l].reshape(nb_s, SSM_STATE))
        outs_p.append(rest_p)
        outs_s.append(rest_s)

    def stack(outs, idx, shape):
        return jnp.stack([o[idx].reshape(shape) for o in outs])

    kv_p, kv_s = (nb_p, s, N_KV_HEADS, HEAD_DIM), (nb_s, dec_seq, N_KV_HEADS, HEAD_DIM)
    st_p, st_s = (nb_p, SSM_GROUPS, SSM_P), (nb_s, SSM_GROUPS, SSM_P)
    return (xp.reshape(nb_p, s, d_model), xs.reshape(nb_s, dec_seq, d_model),
            stack(outs_p, 0, kv_p), stack(outs_p, 1, kv_p), stack(outs_p, 2, (nb_p, s, IDX_DIM)),
            stack(outs_p, 3, st_p), stack(outs_p, 4, st_p),
            stack(outs_s, 0, kv_s), stack(outs_s, 1, kv_s), stack(outs_s, 2, (nb_s, dec_seq, IDX_DIM)),
            stack(outs_s, 3, st_s), stack(outs_s, 4, st_s))
```
